```python
import math
import jax, jax.numpy as jnp
from jax import lax
import numpy as np

D_MODEL = 1024
BATCH = 2
SEQ = 8192
DEPTH = 1
DEC_BATCH = 32
DEC_SEQ = 1
PAST_LEN = 16384
PAGE_SIZE = 128

HEAD_DIM = 64
MOBA_HEADS = D_MODEL // (2 * HEAD_DIM)
NSA_HEADS = D_MODEL // (2 * HEAD_DIM)
NSA_KV_HEADS = NSA_HEADS // 4
MIX_WIDTH = (MOBA_HEADS + NSA_HEADS) * HEAD_DIM
MOBA_BLOCK = 256
MOBA_TOPK = 3
CMP_LEN = 32
CMP_STRIDE = 16
CMP_HIDDEN = 256
SEL_BLOCK = 64
SEL_TOPN = 16
WINDOW = 512
N_BUCKETS = 32
MAX_DISTANCE = 1024
N_EXPERTS = 32
TOP_K = 4
D_FF = D_MODEL
SWIGLU_LIMIT = 7.0
SWIGLU_ALPHA = 1.702
PLE_DIM = 256
MOE_BLOCK = 128
QBLOCK = 128
RMS_EPS = 1e-6
NEG = -1e30
FORCE = 1e30
IN_SIZES = [MOBA_HEADS * HEAD_DIM] * 3 + [NSA_HEADS * HEAD_DIM] + [NSA_KV_HEADS * HEAD_DIM] * 6 + [NSA_HEADS * 3]
IN_OFFSETS = tuple(sum(IN_SIZES[:j]) for j in range(len(IN_SIZES) + 1))
IN_WIDTH = IN_OFFSETS[-1]

kernel_name = "hymba_moba_nsa_moe_decode_step"


def rmsnorm(x, g):
    xf = x.astype(jnp.float32)
    y = xf * lax.rsqrt(jnp.mean(xf * xf, axis=-1, keepdims=True) + RMS_EPS)
    return (y * g.astype(jnp.float32)).astype(x.dtype)


def t5_bucket(dist):
    max_exact = N_BUCKETS // 2
    d = jnp.maximum(dist, 0)
    df = jnp.maximum(d, max_exact).astype(jnp.float32)
    large = max_exact + (jnp.log(df / max_exact) / math.log(MAX_DISTANCE / max_exact) * (N_BUCKETS - max_exact)).astype(jnp.int32)
    return jnp.where(d < max_exact, d, jnp.minimum(large, N_BUCKETS - 1))


def masked_softmax(logits, mask):
    p = jax.nn.softmax(jnp.where(mask, logits, NEG), axis=-1)
    return jnp.where(mask, p, 0.0)


def pad_rows(x, mult):
    pad = (-x.shape[1]) % mult
    return jnp.pad(x, ((0, 0), (0, pad), (0, 0), (0, 0)))


def gather_paged(cache, layer, page_table, new, mult):
    past = cache[layer, page_table]
    bd, n_pages, page, h, d = past.shape
    past = past.reshape(bd, n_pages * page, h, d)
    pad = (-(n_pages * page + new.shape[1])) % mult
    return jnp.concatenate([past, new.astype(past.dtype), jnp.zeros((bd, pad, h, d), past.dtype)], axis=1)


def project(h, w_in):
    b, t, _ = h.shape
    z = h @ w_in
    o = IN_OFFSETS

    def part(j, heads):
        return z[..., o[j]:o[j + 1]].reshape(b, t, heads, HEAD_DIM)

    gates = jax.nn.sigmoid(z[..., o[10]:o[11]]).reshape(b, t, NSA_HEADS, 3)
    return (part(0, MOBA_HEADS), part(1, MOBA_HEADS), part(2, MOBA_HEADS), part(3, NSA_HEADS),
            part(4, NSA_KV_HEADS), part(5, NSA_KV_HEADS), part(6, NSA_KV_HEADS), part(7, NSA_KV_HEADS),
            part(8, NSA_KV_HEADS), part(9, NSA_KV_HEADS), gates)


def moba_attention(q, k_rows, v_rows, q_off, tab):
    b, tq, h, d = q.shape
    nb = k_rows.shape[1] // MOBA_BLOCK
    kb = k_rows.reshape(b, nb, MOBA_BLOCK, h, d)
    vb = v_rows.reshape(b, nb, MOBA_BLOCK, h, d)
    kmean = jnp.mean(kb.astype(jnp.float32), axis=2)
    n_sel = min(MOBA_TOPK, nb)
    qb = QBLOCK if tq % QBLOCK == 0 else tq
    kidx = jnp.arange(MOBA_BLOCK)
    bi = jnp.arange(b)[:, None, None, None, None]
    hi = jnp.arange(h)[None, :, None, None, None]
    tab_t = tab.T
    blk_ids = jnp.arange(nb)
    scale = HEAD_DIM ** -0.5
    n_keys = (n_sel + 1) * MOBA_BLOCK

    def one_block(qi):
        qs = lax.dynamic_slice_in_dim(q, qi * qb, qb, axis=1)
        pos = q_off + qi * qb + jnp.arange(qb)
        cur = pos // MOBA_BLOCK
        gate = jnp.einsum('bqhd,bnhd->bhqn', qs.astype(jnp.float32), kmean)
        gate = jnp.where(blk_ids < cur[:, None], gate, NEG)
        _, top = lax.top_k(gate, n_sel)
        blocks = jnp.concatenate([top, jnp.broadcast_to(cur[:, None], (b, h, qb, 1)).astype(top.dtype)], axis=-1)
        block_ok = jnp.concatenate([top < cur[:, None], jnp.ones((b, h, qb, 1), bool)], axis=-1)
        kg = kb[bi, blocks[..., None], kidx, hi]
        vg = vb[bi, blocks[..., None], kidx, hi]
        dist = pos[:, None, None] - (blocks[..., None] * MOBA_BLOCK + kidx)
        mask = block_ok[..., None] & (dist >= 0)
        logits = jnp.einsum('bqhd,bhqskd->bhqsk', qs, kg).astype(jnp.float32) * scale + tab_t[hi, t5_bucket(dist)]
        p = masked_softmax(logits.reshape(b, h, qb, n_keys), mask.reshape(b, h, qb, n_keys))
        return jnp.einsum('bhqk,bhqkd->bqhd', p.astype(vb.dtype), vg.reshape(b, h, qb, n_keys, d))

    out = lax.map(one_block, jnp.arange(tq // qb))
    return jnp.moveaxis(out, 0, 1).reshape(b, tq, h, d)


def compress(x, cidx, pe, w1, b1, w2):
    b, _, g, d = x.shape
    n = cidx.shape[0]
    blk = (x[:, cidx] + pe[:, None, :]).transpose(0, 1, 3, 2, 4).reshape(b, n, g, CMP_LEN * d)
    return jax.nn.gelu(blk @ w1 + b1) @ w2


def nsa_attention(q, gates, kc, vc, ks, vs, kw, vw, q_off, win_off, tab,
                  cw1k, cb1k, cw2k, cpk, cw1v, cb1v, cw2v, cpv):
    b, tq, hn, d = q.shape
    g = kc.shape[2]
    rr = hn // g
    length = kc.shape[1]
    n_cmp = (length - CMP_LEN) // CMP_STRIDE + 1
    cstart = jnp.arange(n_cmp) * CMP_STRIDE
    cidx = cstart[:, None] + jnp.arange(CMP_LEN)
    kcmp = compress(kc, cidx, cpk, cw1k, cb1k, cw2k)
    vcmp = compress(vc, cidx, cpv, cw1v, cb1v, cw2v)
    cmp_end = cstart + CMP_LEN - 1
    n_sel = ks.shape[1] // SEL_BLOCK
    sel_ids = jnp.arange(n_sel)
    sstart = sel_ids * SEL_BLOCK
    overlap = ((cstart[:, None] < sstart[None, :] + SEL_BLOCK) & (cstart[:, None] + CMP_LEN > sstart[None, :])).astype(jnp.float32)
    ksb = ks.reshape(b, n_sel, SEL_BLOCK, g, d)
    vsb = vs.reshape(b, n_sel, SEL_BLOCK, g, d)
    kwp = jnp.pad(kw, ((0, 0), (WINDOW, 0), (0, 0), (0, 0)))
    vwp = jnp.pad(vw, ((0, 0), (WINDOW, 0), (0, 0), (0, 0)))
    n_top = min(SEL_TOPN, n_sel)
    n_keys = n_top * SEL_BLOCK
    qb = QBLOCK if tq % QBLOCK == 0 else tq
    jidx = jnp.arange(SEL_BLOCK)
    bi = jnp.arange(b)[:, None, None, None, None]
    gi = jnp.arange(g)[None, :, None, None, None]
    tab_g = tab.T.reshape(g, rr, N_BUCKETS)
    g6 = jnp.arange(g)[None, :, None, None, None, None]
    r6 = jnp.arange(rr)[None, None, :, None, None, None]
    widx = jnp.arange(WINDOW + qb)
    scale = HEAD_DIM ** -0.5

    def one_block(qi):
        q0 = q_off + qi * qb
        qg = lax.dynamic_slice_in_dim(q, qi * qb, qb, axis=1).reshape(b, qb, g, rr, d)
        gs = lax.dynamic_slice_in_dim(gates, qi * qb, qb, axis=1).reshape(b, qb, g, rr, 3)
        pos = q0 + jnp.arange(qb)
        lc = jnp.einsum('bqgrd,bngd->bgrqn', qg, kcmp).astype(jnp.float32) * scale
        pc = masked_softmax(lc, cmp_end <= pos[:, None])
        oc = jnp.einsum('bgrqn,bngd->bqgrd', pc.astype(vcmp.dtype), vcmp)
        imp = jnp.einsum('bgrqn,ns->bgqs', pc, overlap)
        cur = pos // SEL_BLOCK
        sid = sel_ids[None, :]
        forced = (sid == 0) | (sid == cur[:, None]) | (sid == cur[:, None] - 1)
        score = jnp.where(sid * SEL_BLOCK <= pos[:, None], jnp.where(forced, FORCE, imp), NEG)
        _, top = lax.top_k(score, n_top)
        kg = ksb[bi, top[..., None], jidx, gi]
        vg = vsb[bi, top[..., None], jidx, gi]
        dist = pos[:, None, None] - (top[..., None] * SEL_BLOCK + jidx)
        ls = jnp.einsum('bqgrd,bgqskd->bgrqsk', qg, kg).astype(jnp.float32) * scale + tab_g[g6, r6, t5_bucket(dist)[:, :, None]]
        ps = masked_softmax(ls.reshape(b, g, rr, qb, n_keys), (dist >= 0).reshape(b, g, 1, qb, n_keys))
        osel = jnp.einsum('bgrqk,bgqkd->bqgrd', ps.astype(vsb.dtype), vg.reshape(b, g, qb, n_keys, d))
        kwin = lax.dynamic_slice_in_dim(kwp, q0 - win_off, WINDOW + qb, axis=1)
        vwin = lax.dynamic_slice_in_dim(vwp, q0 - win_off, WINDOW + qb, axis=1)
        wpos = q0 - WINDOW + widx
        wd = pos[:, None] - wpos[None, :]
        mw = (wpos >= 0)[None, :] & (wd >= 0) & (wd <= WINDOW)
        lw = jnp.einsum('bqgrd,bkgd->bgrqk', qg, kwin).astype(jnp.float32) * scale + tab_g[:, :, t5_bucket(wd)]
        pw = masked_softmax(lw, mw)
        ow = jnp.einsum('bgrqk,bkgd->bqgrd', pw.astype(vwp.dtype), vwin)
        o = gs[..., 0, None] * oc + gs[..., 1, None] * osel + gs[..., 2, None] * ow
        return o.reshape(b, qb, hn, d)

    out = lax.map(one_block, jnp.arange(tq // qb))
    return jnp.moveaxis(out, 0, 1).reshape(b, tq, hn, d)


def moe(h, w_router, b_router, w_gu, b_gu, w_dn, b_dn):
    t, d = h.shape
    logits = (h @ w_router + b_router).astype(jnp.float32)
    top_v, top_i = lax.top_k(logits, TOP_K)
    gates = jax.nn.softmax(top_v, axis=-1).astype(h.dtype)
    n_assign = t * TOP_K
    flat_e = top_i.reshape(n_assign)
    flat_t = jnp.repeat(jnp.arange(t, dtype=jnp.int32), TOP_K)
    flat_g = gates.reshape(n_assign)
    order = jnp.argsort(flat_e)
    se = flat_e[order]
    counts = jnp.zeros((N_EXPERTS,), jnp.int32).at[flat_e].add(1)
    padded = (counts + MOE_BLOCK - 1) // MOE_BLOCK * MOE_BLOCK
    pad_end = jnp.cumsum(padded)
    rank = jnp.arange(n_assign, dtype=jnp.int32) - (jnp.cumsum(counts) - counts)[se]
    dest = (pad_end - padded)[se] + rank
    n_blocks = -(-n_assign // MOE_BLOCK) + N_EXPERTS
    n_rows = n_blocks * MOE_BLOCK
    row_tok = jnp.full((n_rows,), t, jnp.int32).at[dest].set(flat_t[order])
    row_gate = jnp.zeros((n_rows,), h.dtype).at[dest].set(flat_g[order])
    blk_e = jnp.minimum(jnp.searchsorted(pad_end, jnp.arange(n_blocks) * MOE_BLOCK, side='right'), N_EXPERTS - 1)
    xs = jnp.concatenate([h, jnp.zeros((1, d), h.dtype)], axis=0)[row_tok].reshape(n_blocks, MOE_BLOCK, d)

    def expert_rows(args):
        xb, e = args
        gu = xb @ w_gu[e] + b_gu[e]
        glu = jnp.minimum(gu[:, 0::2], SWIGLU_LIMIT)
        lin = jnp.clip(gu[:, 1::2], -SWIGLU_LIMIT, SWIGLU_LIMIT)
        return (glu * jax.nn.sigmoid(SWIGLU_ALPHA * glu) * (lin + 1.0)) @ w_dn[e] + b_dn[e]

    ys = lax.map(expert_rows, (xs, blk_e)).reshape(n_rows, d)
    return jax.ops.segment_sum(ys * row_gate[:, None], row_tok, num_segments=t + 1)[:t]


def channel_and_ple(r, p_i, g, w_router, b_router, w_gu, b_gu, w_dn, b_dn, w_pp, w_pg):
    b, t, d = r.shape
    r = r + moe(rmsnorm(r, g).reshape(b * t, d), w_router, b_router, w_gu, b_gu, w_dn, b_dn).reshape(b, t, d)
    return r + jax.nn.sigmoid(r @ w_pg) * (p_i @ w_pp)


def merge_heads(om, on, w_out):
    b, t = om.shape[:2]
    return jnp.concatenate([om.reshape(b, t, -1), on.reshape(b, t, -1)], axis=-1) @ w_out


def setup_inputs(seed: int = 0) -> dict:
    key = jax.random.key(seed)
    ks = jax.random.split(key, 40)
    f32 = jnp.float32
    n_pages = PAST_LEN // PAGE_SIZE
    n_phys = (DEC_BATCH * n_pages * 5 + 3) // 4
    win_buf = min(WINDOW, PAST_LEN)
    nrm = lambda k, shape, s: jax.random.normal(k, shape, f32) * s
    page_table = jax.random.permutation(ks[0], n_phys)[:DEC_BATCH * n_pages].reshape(DEC_BATCH, n_pages).astype(jnp.int32)
    return {
        'x_prompt': nrm(ks[1], (BATCH, SEQ, D_MODEL), 1.0),
        'x_sample': nrm(ks[2], (DEC_BATCH, DEC_SEQ, D_MODEL), 1.0),
        'p_prompt': nrm(ks[3], (DEPTH, BATCH, SEQ, PLE_DIM), 1.0),
        'p_sample': nrm(ks[4], (DEPTH, DEC_BATCH, DEC_SEQ, PLE_DIM), 1.0),
        'cache_moba_k': nrm(ks[5], (DEPTH, n_phys, PAGE_SIZE, MOBA_HEADS, HEAD_DIM), 1.0),
        'cache_moba_v': nrm(ks[6], (DEPTH, n_phys, PAGE_SIZE, MOBA_HEADS, HEAD_DIM), 1.0),
        'cache_nsa_cmp_k': nrm(ks[7], (DEPTH, n_phys, PAGE_SIZE, NSA_KV_HEADS, HEAD_DIM), 1.0),
        'cache_nsa_cmp_v': nrm(ks[8], (DEPTH, n_phys, PAGE_SIZE, NSA_KV_HEADS, HEAD_DIM), 1.0),
        'cache_nsa_sel_k': nrm(ks[9], (DEPTH, n_phys, PAGE_SIZE, NSA_KV_HEADS, HEAD_DIM), 1.0),
        'cache_nsa_sel_v': nrm(ks[10], (DEPTH, n_phys, PAGE_SIZE, NSA_KV_HEADS, HEAD_DIM), 1.0),
        'state_nsa_win_k': nrm(ks[11], (DEPTH, DEC_BATCH, win_buf, NSA_KV_HEADS, HEAD_DIM), 1.0),
        'state_nsa_win_v': nrm(ks[12], (DEPTH, DEC_BATCH, win_buf, NSA_KV_HEADS, HEAD_DIM), 1.0),
        'page_table': page_table,
        'rel_bias': nrm(ks[13], (N_BUCKETS, MOBA_HEADS + NSA_HEADS), 0.5),
        'g_mix': 1.0 + nrm(ks[14], (DEPTH, D_MODEL), 0.05),
        'w_in': nrm(ks[15], (DEPTH, D_MODEL, IN_WIDTH), D_MODEL ** -0.5),
        'w_out': nrm(ks[16], (DEPTH, MIX_WIDTH, D_MODEL), MIX_WIDTH ** -0.5),
        'cmp_w1_k': nrm(ks[17], (DEPTH, CMP_LEN * HEAD_DIM, CMP_HIDDEN), (CMP_LEN * HEAD_DIM) ** -0.5),
        'cmp_b1_k': nrm(ks[18], (DEPTH, CMP_HIDDEN), 0.02),
        'cmp_w2_k': nrm(ks[19], (DEPTH, CMP_HIDDEN, HEAD_DIM), 2.0 * CMP_HIDDEN ** -0.5),
        'cmp_pos_k': nrm(ks[20], (DEPTH, CMP_LEN, HEAD_DIM), 0.1),
        'cmp_w1_v': nrm(ks[21], (DEPTH, CMP_LEN * HEAD_DIM, CMP_HIDDEN), (CMP_LEN * HEAD_DIM) ** -0.5),
        'cmp_b1_v': nrm(ks[22], (DEPTH, CMP_HIDDEN), 0.02),
        'cmp_w2_v': nrm(ks[23], (DEPTH, CMP_HIDDEN, HEAD_DIM), 2.0 * CMP_HIDDEN ** -0.5),
        'cmp_pos_v': nrm(ks[24], (DEPTH, CMP_LEN, HEAD_DIM), 0.1),
        'g_ffn': 1.0 + nrm(ks[25], (DEPTH, D_MODEL), 0.05),
        'w_router': nrm(ks[26], (DEPTH, D_MODEL, N_EXPERTS), D_MODEL ** -0.5),
        'b_router': nrm(ks[27], (DEPTH, N_EXPERTS), 0.01),
        'w_gate_up': nrm(ks[28], (DEPTH, N_EXPERTS, D_MODEL, 2 * D_FF), D_MODEL ** -0.5),
        'b_gate_up': nrm(ks[29], (DEPTH, N_EXPERTS, 2 * D_FF), 0.02),
        'w_down': nrm(ks[30], (DEPTH, N_EXPERTS, D_FF, D_MODEL), D_FF ** -0.5),
        'b_down': nrm(ks[31], (DEPTH, N_EXPERTS, D_MODEL), 0.02),
        'w_ple_proj': nrm(ks[32], (DEPTH, PLE_DIM, D_MODEL), PLE_DIM ** -0.5),
        'w_ple_gate': nrm(ks[33], (DEPTH, D_MODEL, D_MODEL), D_MODEL ** -0.5),
        'g_final': 1.0 + nrm(ks[34], (D_MODEL,), 0.05),
    }


def reference(x_prompt, x_sample, p_prompt, p_sample, cache_moba_k, cache_moba_v, cache_nsa_cmp_k, cache_nsa_cmp_v,
              cache_nsa_sel_k, cache_nsa_sel_v, state_nsa_win_k, state_nsa_win_v, page_table, rel_bias, g_mix, w_in, w_out,
              cmp_w1_k, cmp_b1_k, cmp_w2_k, cmp_pos_k, cmp_w1_v, cmp_b1_v, cmp_w2_v, cmp_pos_v, g_ffn, w_router, b_router,
              w_gate_up, b_gate_up, w_down, b_down, w_ple_proj, w_ple_gate, g_final):
    past_len = page_table.shape[1] * PAGE_SIZE
    win_buf = state_nsa_win_k.shape[2]
    tab_m = rel_bias[:, :MOBA_HEADS]
    tab_n = rel_bias[:, MOBA_HEADS:]
    r_p, r_s = x_prompt, x_sample
    st_p, st_s = [], []
    for i in range(DEPTH):
        cmp_w = (cmp_w1_k[i], cmp_b1_k[i], cmp_w2_k[i], cmp_pos_k[i], cmp_w1_v[i], cmp_b1_v[i], cmp_w2_v[i], cmp_pos_v[i])
        ffn_w = (g_ffn[i], w_router[i], b_router[i], w_gate_up[i], b_gate_up[i], w_down[i], b_down[i], w_ple_proj[i], w_ple_gate[i])
        qm, km, vm, qn, kc, vc, ks, vs, kw, vw, gt = project(rmsnorm(r_p, g_mix[i]), w_in[i])
        om = moba_attention(qm, pad_rows(km, MOBA_BLOCK), pad_rows(vm, MOBA_BLOCK), 0, tab_m)
        on = nsa_attention(qn, gt, kc, vc, pad_rows(ks, SEL_BLOCK), pad_rows(vs, SEL_BLOCK), kw, vw, 0, 0, tab_n, *cmp_w)
        r_p = r_p + merge_heads(om, on, w_out[i])
        r_p = channel_and_ple(r_p, p_prompt[i], *ffn_w)
        wp = min(WINDOW, kw.shape[1])
        st_p.append((km, vm, kc, vc, ks, vs, kw[:, -wp:], vw[:, -wp:]))
        qm, km, vm, qn, kc, vc, ks, vs, kw, vw, gt = project(rmsnorm(r_s, g_mix[i]), w_in[i])
        kw_all = jnp.concatenate([state_nsa_win_k[i].astype(kw.dtype), kw], axis=1)
        vw_all = jnp.concatenate([state_nsa_win_v[i].astype(vw.dtype), vw], axis=1)
        om = moba_attention(qm, gather_paged(cache_moba_k, i, page_table, km, MOBA_BLOCK),
                            gather_paged(cache_moba_v, i, page_table, vm, MOBA_BLOCK), past_len, tab_m)
        on = nsa_attention(qn, gt, gather_paged(cache_nsa_cmp_k, i, page_table, kc, 1),
                           gather_paged(cache_nsa_cmp_v, i, page_table, vc, 1),
                           gather_paged(cache_nsa_sel_k, i, page_table, ks, SEL_BLOCK),
                           gather_paged(cache_nsa_sel_v, i, page_table, vs, SEL_BLOCK),
                           kw_all, vw_all, past_len, past_len - win_buf, tab_n, *cmp_w)
        r_s = r_s + merge_heads(om, on, w_out[i])
        r_s = channel_and_ple(r_s, p_sample[i], *ffn_w)
        st_s.append((km, vm, kc, vc, ks, vs, kw_all[:, -win_buf:], vw_all[:, -win_buf:]))
    y_prompt = rmsnorm(r_p, g_final)
    y_sample = rmsnorm(r_s, g_final)
    mk_p, mv_p, ck_p, cv_p, sk_p, sv_p, wk_p, wv_p = [jnp.stack(a) for a in zip(*st_p)]
    mk_s, mv_s, ck_s, cv_s, sk_s, sv_s, wk_s, wv_s = [jnp.stack(a) for a in zip(*st_s)]
    return (y_prompt, y_sample, mk_p, mv_p, ck_p, cv_p, sk_p, sv_p, wk_p, wv_p,
            mk_s, mv_s, ck_s, cv_s, sk_s, sv_s, wk_s, wv_s)
```

```python
import functools
import math

import numpy as np
import jax
import jax.numpy as jnp
from jax import lax
from jax.experimental import pallas as pl
from jax.experimental.pallas import tpu as pltpu

PAGE_SIZE = 128
HEAD_DIM = 64
MOBA_BLOCK = 256
MOBA_TOPK = 3
CMP_LEN = 32
CMP_STRIDE = 16
CMP_HIDDEN = 256
SEL_BLOCK = 64
SEL_TOPN = 16
WINDOW = 512
N_BUCKETS = 32
MAX_DISTANCE = 1024
N_EXPERTS = 32
TOP_K = 4
SWIGLU_LIMIT = 7.0
SWIGLU_ALPHA = 1.702
RMS_EPS = 1e-6
NEG = -1e30
FORCE = 1e30
SCALE = HEAD_DIM ** -0.5
LANES = 128
VMEM_LIMIT = 56 * 1024 * 1024

F32 = jnp.float32
BF16 = jnp.bfloat16
HI = lax.Precision.HIGHEST
NT = (((1,), (1,)), ((), ()))


def _cparams(sem):
    return pltpu.CompilerParams(dimension_semantics=sem, vmem_limit_bytes=VMEM_LIMIT)


def _const_spec(shape):
    n = len(shape)
    return pl.BlockSpec(shape, lambda *_: (0,) * n)


def _t5_bucket(dist):
    max_exact = N_BUCKETS // 2
    d = jnp.maximum(dist, 0)
    df = jnp.maximum(d, max_exact).astype(jnp.float32)
    large = max_exact + (jnp.log(df / max_exact) / math.log(MAX_DISTANCE / max_exact)
                         * (N_BUCKETS - max_exact)).astype(jnp.int32)
    return jnp.where(d < max_exact, d, jnp.minimum(large, N_BUCKETS - 1))


def _topk_mask(score, valid, k, idx_iota, axis):
    n = score.shape[axis]
    low = -3e38
    work = jnp.where(valid, score, low)
    sel = jnp.zeros(score.shape, F32)
    ids = []
    for _ in range(k):
        mx = jnp.max(work, axis=axis, keepdims=True)
        cand = (work == mx) & (mx > low)
        idx = jnp.min(jnp.where(cand, idx_iota, n), axis=axis, keepdims=True)
        hit = idx_iota == idx
        sel = jnp.where(hit, 1.0, sel)
        work = jnp.where(hit, low, work)
        ids.append(idx)
    return sel, ids


C_QM, C_KM, C_VM, C_QN, C_KV, C_GT, C_END = 0, 512, 1024, 1536, 2560, 3328, 3456


def _proj_weight(w_in):
    d = w_in.shape[0]
    qn = w_in[:, 1536:2048].reshape(d, 8, HEAD_DIM)
    z = jnp.zeros((d, 8, HEAD_DIM), w_in.dtype)
    grp0 = (jnp.arange(8) < 4)[None, :, None]
    qn_pad = jnp.concatenate([jnp.where(grp0, qn, z), jnp.where(grp0, z, qn)], axis=-1).reshape(d, 1024)
    gates = jnp.pad(w_in[:, 2816:2840], ((0, 0), (0, LANES - 24)))
    return jnp.concatenate([w_in[:, :1536], qn_pad, w_in[:, 2048:2816], gates], axis=1).astype(BF16)


def _proj_kernel(x_ref, g_ref, w_ref, qm_ref, km_ref, vm_ref, kmb_ref, vmb_ref, qn_ref,
                 kv_ref, kvb_ref, gt_ref, *rest, with_kmean):
    x = x_ref[...]
    h = x * lax.rsqrt(jnp.mean(x * x, axis=-1, keepdims=True) + RMS_EPS) * g_ref[...]
    z = jnp.dot(h.astype(BF16), w_ref[...], preferred_element_type=F32)
    qm_ref[...] = z[:, C_QM:C_KM]
    km = z[:, C_KM:C_VM]
    vm = z[:, C_VM:C_QN]
    km_ref[...] = km
    vm_ref[...] = vm
    kmb_ref[...] = km.astype(BF16)
    vmb_ref[...] = vm.astype(BF16)
    for hd in range(8):
        qn_ref[hd] = (z[:, C_QN + hd * LANES:C_QN + (hd + 1) * LANES] * SCALE).astype(BF16)
    for j in range(6):
        blk = z[:, C_KV + j * LANES:C_KV + (j + 1) * LANES]
        kv_ref[j] = blk
        kvb_ref[j] = blk.astype(BF16)
    gt_ref[...] = jax.nn.sigmoid(z[:, C_GT:C_END])
    if with_kmean:
        (kmean_ref,) = rest
        nblk = km.shape[0] // MOBA_BLOCK
        for i in range(nblk):
            kmean_ref[0, i:i + 1, :] = jnp.mean(km[i * MOBA_BLOCK:(i + 1) * MOBA_BLOCK], axis=0, keepdims=True)


def _project(x, g, w, tm, with_kmean):
    t, d = x.shape
    nt = t // tm
    row = lambda w_: pl.BlockSpec((tm, w_), lambda i: (i, 0))
    out_shape = [
        jax.ShapeDtypeStruct((t, 512), F32), jax.ShapeDtypeStruct((t, 512), F32), jax.ShapeDtypeStruct((t, 512), F32),
        jax.ShapeDtypeStruct((t, 512), BF16), jax.ShapeDtypeStruct((t, 512), BF16),
        jax.ShapeDtypeStruct((8, t, LANES), BF16),
        jax.ShapeDtypeStruct((6, t, LANES), F32), jax.ShapeDtypeStruct((6, t, LANES), BF16),
        jax.ShapeDtypeStruct((t, LANES), F32),
    ]
    out_specs = [row(512), row(512), row(512), row(512), row(512),
                 pl.BlockSpec((8, tm, LANES), lambda i: (0, i, 0)),
                 pl.BlockSpec((6, tm, LANES), lambda i: (0, i, 0)),
                 pl.BlockSpec((6, tm, LANES), lambda i: (0, i, 0)),
                 row(LANES)]
    if with_kmean:
        nb = tm // MOBA_BLOCK
        out_shape.append(jax.ShapeDtypeStruct((nt, nb, 512), F32))
        out_specs.append(pl.BlockSpec((1, nb, 512), lambda i: (i, 0, 0)))
    return pl.pallas_call(
        functools.partial(_proj_kernel, with_kmean=with_kmean),
        grid=(nt,),
        in_specs=[row(d), _const_spec((1, d)), _const_spec(w.shape)],
        out_specs=out_specs, out_shape=out_shape,
        compiler_params=_cparams(("parallel",)), name="proj",
    )(x, g, w)


def _toeplitz_tiles(bv, rows, cols, n_tiles):
    r = np.arange(rows)[:, None]
    c = np.arange(cols)[None, :]
    idx = np.stack([rows * m + r - c for m in range(n_tiles)])
    tiles = jnp.take(bv, jnp.asarray(np.clip(idx, 0, MAX_DISTANCE)), axis=0)
    tiles = jnp.where(jnp.asarray(idx >= 0)[..., None], tiles, NEG)
    return jnp.transpose(tiles, (3, 0, 1, 2))


TQ_M = MOBA_BLOCK


def _moba_kernel(q_ref, k_ref, v_ref, kmean_ref, tb_ref, o_ref, *, nb, n_tb):
    qi = pl.program_id(2)
    q2 = q_ref[0]
    lane = lax.broadcasted_iota(jnp.int32, q2.shape, 1)
    blk_lane = lax.broadcasted_iota(jnp.int32, (TQ_M, nb), 1)
    outs = []
    for hh in range(2):
        qh = jnp.where((lane >= HEAD_DIM * hh) & (lane < HEAD_DIM * (hh + 1)), q2, 0.0)
        gate = lax.dot_general(qh, kmean_ref[0], NT, precision=HI, preferred_element_type=F32)
        sel, _ = _topk_mask(gate, blk_lane < qi, MOBA_TOPK, blk_lane, 1)
        qb = (qh * SCALE).astype(BF16)

        def logits(j, hh=hh, qb=qb):
            kj = k_ref[0, pl.ds(pl.multiple_of(j * MOBA_BLOCK, MOBA_BLOCK), MOBA_BLOCK), :]
            s = lax.dot_general(qb, kj, NT, preferred_element_type=F32)
            return s + tb_ref[hh, jnp.minimum(qi - j, n_tb - 1)]

        def pv(p, j):
            vj = v_ref[0, pl.ds(pl.multiple_of(j * MOBA_BLOCK, MOBA_BLOCK), MOBA_BLOCK), :]
            return jnp.dot(p.astype(BF16), vj, preferred_element_type=F32)

        s = logits(qi)
        m = jnp.max(s, axis=1, keepdims=True)
        p = jnp.exp(s - m)
        l = jnp.sum(p, axis=1, keepdims=True)
        acc = pv(p, qi)

        def body(j, carry, sel=sel, logits=logits, pv=pv):
            m, l, acc = carry
            selj = jnp.sum(jnp.where(blk_lane == j, sel, 0.0), axis=1, keepdims=True) > 0.5
            s = jnp.where(selj, logits(j), NEG)
            m_new = jnp.maximum(m, jnp.max(s, axis=1, keepdims=True))
            alpha = jnp.exp(m - m_new)
            p = jnp.exp(s - m_new)
            return m_new, alpha * l + jnp.sum(p, axis=1, keepdims=True), alpha * acc + pv(p, j)

        m, l, acc = lax.fori_loop(0, qi, body, (m, l, acc))
        outs.append(acc / l)
    o_ref[0] = jnp.where(lane < HEAD_DIM, outs[0], outs[1]).astype(o_ref.dtype)


def _moba_prompt(qm, kmb, vmb, kmean, tb):
    b, t, _ = qm.shape
    nb = t // MOBA_BLOCK
    n_tb = tb.shape[1]
    return pl.pallas_call(
        functools.partial(_moba_kernel, nb=nb, n_tb=n_tb),
        grid=(4, b, t // TQ_M),
        in_specs=[pl.BlockSpec((1, TQ_M, LANES), lambda p, bi, qi: (bi, qi, p)),
                  pl.BlockSpec((1, t, LANES), lambda p, bi, qi: (bi, 0, p)),
                  pl.BlockSpec((1, t, LANES), lambda p, bi, qi: (bi, 0, p)),
                  pl.BlockSpec((1, nb, LANES), lambda p, bi, qi: (bi, 0, p)),
                  pl.BlockSpec((2, n_tb, TQ_M, MOBA_BLOCK), lambda p, bi, qi: (p, 0, 0, 0))],
        out_specs=pl.BlockSpec((1, TQ_M, LANES), lambda p, bi, qi: (bi, qi, p)),
        out_shape=jax.ShapeDtypeStruct((b, t, 512), BF16),
        compiler_params=_cparams(("parallel", "parallel", "parallel")), name="moba_prompt",
    )(qm, kmb, vmb, kmean, tb)


CHUNKS_PER_PAGE = PAGE_SIZE // CMP_STRIDE
CMP_PAGES = 16


def _chunk_layout(rows):
    n = rows.shape[0]
    x = rows.reshape(n, CHUNKS_PER_PAGE, CMP_STRIDE, 2, HEAD_DIM)
    return jnp.transpose(x, (0, 3, 1, 2, 4)).reshape(n, 2 * CHUNKS_PER_PAGE, CMP_STRIDE * HEAD_DIM).astype(BF16)


def _cmp_const_kernel(pe_ref, w1_ref, b1_ref, o_ref):
    o_ref[...] = jnp.sum(w1_ref[...] * pe_ref[...], axis=0, keepdims=True) + b1_ref[...]


def _cmp_const(pe, w1, b1):
    return pl.pallas_call(
        _cmp_const_kernel, out_shape=jax.ShapeDtypeStruct((1, CMP_HIDDEN), F32), name="cmp_const",
    )(pe.reshape(CMP_LEN * HEAD_DIM, 1), w1, b1.reshape(1, CMP_HIDDEN))


def _gelu_tanh(x):
    return 0.5 * x * (1.0 + jnp.tanh(math.sqrt(2.0 / math.pi) * (x + 0.044715 * x * x * x)))


def _compress_kernel(pt_ref, *refs):
    x_refs = refs[:CMP_PAGES + 1]
    wab_ref, c_ref, w2_ref, o_ref = refs[CMP_PAGES + 1:]
    rows_pp = 2 * CHUNKS_PER_PAGE
    x = jnp.concatenate([r[0] for r in x_refs], axis=0)
    ab = jnp.dot(x, wab_ref[...], preferred_element_type=F32)
    a = ab[:CMP_PAGES * rows_pp, :CMP_HIDDEN]
    bm = ab[:, CMP_HIDDEN:]
    n = bm.shape[0]
    nxt = pltpu.roll(bm, n - 1, 0)
    nxt_page = pltpu.roll(bm, n - (rows_pp - CHUNKS_PER_PAGE + 1), 0)
    c_id = lax.broadcasted_iota(jnp.int32, bm.shape, 0) % CHUNKS_PER_PAGE
    b_next = jnp.where(c_id == CHUNKS_PER_PAGE - 1, nxt_page, nxt)[:CMP_PAGES * rows_pp]
    hid = _gelu_tanh(a + b_next + c_ref[...]).astype(BF16)
    hid = hid.reshape(CMP_PAGES, 2, CHUNKS_PER_PAGE, CMP_HIDDEN)
    out = None
    for g in range(2):
        hg = hid[:, g].reshape(CMP_PAGES * CHUNKS_PER_PAGE, CMP_HIDDEN)
        og = jnp.dot(hg, w2_ref[g], preferred_element_type=F32)
        out = og if out is None else out + og
    o_ref[0] = out.astype(o_ref.dtype)


def _compress(xc, page_table, w1, const, w2):
    b, n_pages = page_table.shape
    assert n_pages % CMP_PAGES == 0
    half = CMP_STRIDE * HEAD_DIM
    wab = jnp.concatenate([w1[:half], w1[half:]], axis=1).astype(BF16)
    z = jnp.zeros_like(w2)
    w2p = jnp.stack([jnp.concatenate([w2, z], axis=1), jnp.concatenate([z, w2], axis=1)]).astype(BF16)

    def page_spec(k):
        def imap(bi, i, pt):
            return (pt[bi, jnp.minimum(i * CMP_PAGES + k, n_pages - 1)], 0, 0)
        return pl.BlockSpec((1, 2 * CHUNKS_PER_PAGE, half), imap)

    tile = CMP_PAGES * CHUNKS_PER_PAGE
    grid_spec = pltpu.PrefetchScalarGridSpec(
        num_scalar_prefetch=1, grid=(b, n_pages // CMP_PAGES),
        in_specs=[page_spec(k) for k in range(CMP_PAGES + 1)] + [
            pl.BlockSpec(wab.shape, lambda bi, i, pt: (0, 0)),
            pl.BlockSpec(const.shape, lambda bi, i, pt: (0, 0)),
            pl.BlockSpec(w2p.shape, lambda bi, i, pt: (0, 0, 0))],
        out_specs=pl.BlockSpec((1, tile, LANES), lambda bi, i, pt: (bi, i, 0)))
    return pl.pallas_call(
        _compress_kernel, grid_spec=grid_spec,
        out_shape=jax.ShapeDtypeStruct((b, n_pages * CHUNKS_PER_PAGE, LANES), BF16),
        compiler_params=_cparams(("parallel", "parallel")), name="compress",
    )(page_table, *([xc] * (CMP_PAGES + 1)), wab, const, w2p)


TQ_N = 128
TK_N = 256
NSA_HEADS = 8
assert WINDOW % TK_N == 0 and TK_N % TQ_N == 0
WIN_TILES = WINDOW // TK_N + 1


def _overlap_matrix(n_cmp_pad, n_sel_pad):
    cstart = np.arange(n_cmp_pad)[:, None] * CMP_STRIDE
    sstart = np.arange(n_sel_pad)[None, :] * SEL_BLOCK
    return ((cstart < sstart + SEL_BLOCK) & (cstart + CMP_LEN > sstart)).astype(np.float32)


def _softmax_update(s, state, v, first):
    if first:
        m = jnp.max(s, axis=1, keepdims=True)
        p = jnp.exp(s - m)
        return m, jnp.sum(p, axis=1, keepdims=True), jnp.dot(p.astype(BF16), v, preferred_element_type=F32)
    m0, l0, acc0 = state
    m = jnp.maximum(m0, jnp.max(s, axis=1, keepdims=True))
    alpha = jnp.exp(m0 - m)
    p = jnp.exp(s - m)
    return (m, alpha * l0 + jnp.sum(p, axis=1, keepdims=True),
            alpha * acc0 + jnp.dot(p.astype(BF16), v, preferred_element_type=F32))


def _nsa_kernel(q_ref, gt_ref, kc_ref, vc_ref, ks_ref, vs_ref, kw_ref, vw_ref, ov_ref, tb_ref, wm_ref, o_ref,
                *, n_cmp, n_sel, n_tb):
    qi = pl.program_id(1)
    q0 = qi * TQ_N
    rows = NSA_HEADS * TQ_N
    q8 = q_ref[...].reshape(rows, LANES)
    n_cp = kc_ref.shape[1]
    pos = q0 + lax.broadcasted_iota(jnp.int32, (1, TQ_N, 1), 1)

    lc = lax.dot_general(q8, kc_ref[0], NT, preferred_element_type=F32).reshape(NSA_HEADS, TQ_N, n_cp)
    n_id = lax.broadcasted_iota(jnp.int32, (1, 1, n_cp), 2)
    ok = (n_id * CMP_STRIDE + (CMP_LEN - 1) <= pos) & (n_id < n_cmp)
    mc = jnp.max(jnp.where(ok, lc, NEG), axis=2, keepdims=True)
    pc = jnp.where(ok, jnp.exp(lc - mc), 0.0)
    lsum = jnp.sum(pc, axis=2, keepdims=True)
    pc = pc / jnp.where(lsum > 0.0, lsum, 1.0)
    oc = jnp.dot(pc.reshape(rows, n_cp).astype(BF16), vc_ref[0], preferred_element_type=F32)

    pcg = pc.reshape(2, 4, TQ_N, n_cp)
    pcg = (pcg[:, 0] + pcg[:, 1]) + (pcg[:, 2] + pcg[:, 3])
    imp = jnp.dot(pcg.reshape(2 * TQ_N, n_cp), ov_ref[...], precision=HI, preferred_element_type=F32)
    n_sp = imp.shape[1]
    imp = imp.reshape(2, TQ_N, n_sp)
    sid = lax.broadcasted_iota(jnp.int32, (1, 1, n_sp), 2)
    cur = pos // SEL_BLOCK
    forced = (sid == 0) | (sid == cur) | (sid == cur - 1)
    valid = (sid * SEL_BLOCK <= pos) & (sid < n_sel)
    score = jnp.where(forced, FORCE, imp)
    sid_full = lax.broadcasted_iota(jnp.int32, (2, TQ_N, n_sp), 2)
    sel, _ = _topk_mask(score, valid, min(SEL_TOPN, n_sel), sid_full, 2)
    selneg = jnp.where(sel > 0.5, 0.0, NEG).astype(BF16)
    selneg8 = jnp.broadcast_to(selneg[:, None], (2, 4, TQ_N, n_sp)).reshape(rows, n_sp)

    jmax = q0 // TK_N
    e_row = lax.broadcasted_iota(jnp.int32, (n_sp, TK_N), 0)
    e_col = lax.broadcasted_iota(jnp.int32, (n_sp, TK_N), 1) // SEL_BLOCK

    def bias(j):
        return tb_ref[jnp.minimum(qi - 2 * j, n_tb - 1)].reshape(rows, TK_N)

    def kslice(ref, j):
        return ref[0, pl.ds(pl.multiple_of(j * TK_N, TK_N), TK_N), :]

    def sel_logits(j):
        e_j = (e_row == e_col + j * (TK_N // SEL_BLOCK)).astype(BF16)
        s = lax.dot_general(q8, kslice(ks_ref, j), NT, preferred_element_type=F32)
        return s + jnp.dot(selneg8, e_j, preferred_element_type=F32) + bias(j)

    st = _softmax_update(sel_logits(jmax), None, kslice(vs_ref, jmax), True)
    st = lax.fori_loop(0, jmax, lambda j, st: _softmax_update(sel_logits(j), st, kslice(vs_ref, j), False), st)
    osel = st[2] / st[1]

    def win_logits(j):
        s = lax.dot_general(q8, kslice(kw_ref, j), NT, preferred_element_type=F32) + bias(j)
        wmask = wm_ref[jnp.minimum(qi - 2 * j, wm_ref.shape[0] - 1)]
        return (s.reshape(NSA_HEADS, TQ_N, TK_N) + wmask[None]).reshape(rows, TK_N)

    st = _softmax_update(win_logits(jmax), None, kslice(vw_ref, jmax), True)
    jlo = jnp.maximum(jmax - (WIN_TILES - 1), 0)
    st = lax.fori_loop(jlo, jmax, lambda j, st: _softmax_update(win_logits(j), st, kslice(vw_ref, j), False), st)
    ow = st[2] / st[1]

    gt = gt_ref[...]
    glane = lax.broadcasted_iota(jnp.int32, gt.shape, 1)

    def gate(hd, br):
        return jnp.sum(jnp.where(glane == hd * 3 + br, gt, 0.0), axis=1, keepdims=True)

    for hd in range(NSA_HEADS):
        r0 = hd * TQ_N
        o = (gate(hd, 0) * oc[r0:r0 + TQ_N] + gate(hd, 1) * osel[r0:r0 + TQ_N] + gate(hd, 2) * ow[r0:r0 + TQ_N])
        o_ref[hd] = o.astype(o_ref.dtype)


def _nsa_prompt(qn8, gates, kcmp, vcmp, ksb, vsb, kwb, vwb, tbn, wm, n_cmp):
    b, t, _ = ksb.shape
    n_cp = kcmp.shape[1]
    n_sel = t // SEL_BLOCK
    n_sp = -(-n_sel // LANES) * LANES
    ov = jnp.asarray(_overlap_matrix(n_cp, n_sp))
    n_tb = tbn.shape[0]
    nq = t // TQ_N
    full = lambda a: pl.BlockSpec((1,) + a.shape[1:], lambda bi, qi: (bi,) + (0,) * (a.ndim - 1))
    once = lambda a: pl.BlockSpec(a.shape, lambda bi, qi: (0,) * a.ndim, pipeline_mode=pl.Buffered(1))
    return pl.pallas_call(
        functools.partial(_nsa_kernel, n_cmp=n_cmp, n_sel=n_sel, n_tb=n_tb),
        grid=(b, nq),
        in_specs=[pl.BlockSpec((NSA_HEADS, TQ_N, LANES), lambda bi, qi: (0, bi * nq + qi, 0)),
                  pl.BlockSpec((TQ_N, LANES), lambda bi, qi: (bi * nq + qi, 0)),
                  full(kcmp), full(vcmp), full(ksb), full(vsb), full(kwb), full(vwb),
                  once(ov), once(tbn), once(wm)],
        out_specs=pl.BlockSpec((NSA_HEADS, TQ_N, LANES), lambda bi, qi: (0, bi * nq + qi, 0)),
        out_shape=jax.ShapeDtypeStruct((NSA_HEADS, b * t, LANES), BF16),
        compiler_params=_cparams(("parallel", "parallel")), name="nsa_prompt",
    )(qn8, gates, kcmp, vcmp, ksb, vsb, kwb, vwb, ov, tbn, wm)


def _mix_kernel(x_ref, om_ref, on_ref, wom_ref, won_ref, g_ref, wr_ref, br_ref,
                r_ref, h_ref, ti_ref, tg_ref):
    acc = x_ref[...] + jnp.dot(om_ref[...], wom_ref[...], preferred_element_type=F32)
    for hd in range(NSA_HEADS):
        acc = acc + jnp.dot(on_ref[hd], won_ref[hd], preferred_element_type=F32)
    r_ref[...] = acc
    h = acc * lax.rsqrt(jnp.mean(acc * acc, axis=-1, keepdims=True) + RMS_EPS) * g_ref[...]
    h_ref[...] = h
    logits = jnp.dot(h, wr_ref[...], precision=HI, preferred_element_type=F32) + br_ref[...]
    e_id = lax.broadcasted_iota(jnp.int32, logits.shape, 1)
    work = logits
    vals, ids = [], []
    for _ in range(TOP_K):
        mx = jnp.max(work, axis=1, keepdims=True)
        idx = jnp.min(jnp.where(work == mx, e_id, N_EXPERTS), axis=1, keepdims=True)
        work = jnp.where(e_id == idx, -3e38, work)
        vals.append(mx)
        ids.append(idx)
    ex = [jnp.exp(v - vals[0]) for v in vals]
    tot = ex[0]
    for e in ex[1:]:
        tot = tot + e
    k_id = lax.broadcasted_iota(jnp.int32, ti_ref.shape, 1)
    ti = jnp.zeros(ti_ref.shape, jnp.int32)
    tg = jnp.zeros(tg_ref.shape, F32)
    for k in range(TOP_K):
        ti = jnp.where(k_id == k, ids[k], ti)
        tg = jnp.where(k_id == k, ex[k] / tot, tg)
    ti_ref[...] = ti
    tg_ref[...] = tg


def _mix(x, om, on8, wom, won8, g_ffn, w_router, b_router, tm):
    t, d = x.shape
    row = lambda w_: pl.BlockSpec((tm, w_), lambda i: (i, 0))
    return pl.pallas_call(
        _mix_kernel, grid=(t // tm,),
        in_specs=[row(d), row(512), pl.BlockSpec((NSA_HEADS, tm, LANES), lambda i: (0, i, 0)),
                  _const_spec(wom.shape), _const_spec(won8.shape), _const_spec((1, d)),
                  _const_spec(w_router.shape), _const_spec((1, N_EXPERTS))],
        out_specs=[row(d), row(d), row(TOP_K), row(TOP_K)],
        out_shape=[jax.ShapeDtypeStruct((t, d), F32), jax.ShapeDtypeStruct((t, d), F32),
                   jax.ShapeDtypeStruct((t, TOP_K), jnp.int32), jax.ShapeDtypeStruct((t, TOP_K), F32)],
        compiler_params=_cparams(("parallel",)), name="mix_router",
    )(x, om, on8, wom, won8, g_ffn.reshape(1, d), w_router, b_router.reshape(1, N_EXPERTS))


TM_E = 256


def _route(top_i, tm):
    t = top_i.shape[0]
    n_assign = t * TOP_K
    flat_e = top_i.reshape(n_assign)
    order = jnp.argsort(flat_e)
    se = flat_e[order]
    counts = jnp.zeros((N_EXPERTS,), jnp.int32).at[flat_e].add(1)
    padded = (counts + tm - 1) // tm * tm
    pad_end = jnp.cumsum(padded)
    rank = jnp.arange(n_assign, dtype=jnp.int32) - (jnp.cumsum(counts) - counts)[se]
    dest_sorted = (pad_end - padded)[se] + rank
    n_blocks = -(-n_assign // tm) + N_EXPERTS
    row_tok = jnp.zeros((n_blocks * tm,), jnp.int32).at[dest_sorted].set((order // TOP_K).astype(jnp.int32))
    dest = jnp.zeros((n_assign,), jnp.int32).at[order].set(dest_sorted.astype(jnp.int32))
    blk_e = jnp.minimum(jnp.searchsorted(pad_end, jnp.arange(n_blocks) * tm, side='right'), N_EXPERTS - 1)
    n_used = (pad_end[-1] // tm).astype(jnp.int32).reshape(1)
    return row_tok, dest, blk_e.astype(jnp.int32), n_used, n_blocks


def _expert_kernel(blk_e_ref, n_used_ref, tok_cur_ref, tok_nxt_ref, h_hbm, wg_ref, wl_ref, bg_ref, bl_ref,
                   wd_ref, bd_ref, y_ref, xbuf, sem):
    i = pl.program_id(0)
    n_used = n_used_ref[0]
    tm = xbuf.shape[1]

    def gather(tok_ref, slot):
        def body(r, c):
            pltpu.make_async_copy(h_hbm.at[pl.ds(tok_ref[r], 1), :], xbuf.at[slot, pl.ds(r, 1), :], sem.at[slot]).start()
            return c
        lax.fori_loop(0, tm, body, 0)

    @pl.when(i == 0)
    def _():
        gather(tok_cur_ref, 0)

    @pl.when(i + 1 < n_used)
    def _():
        gather(tok_nxt_ref, (i + 1) % 2)

    @pl.when(i < n_used)
    def _():
        slot = i % 2
        pltpu.make_async_copy(h_hbm.at[pl.ds(0, tm), :], xbuf.at[slot], sem.at[slot]).wait()
        xb = xbuf[slot].astype(BF16)
        glu = jnp.dot(xb, wg_ref[0], preferred_element_type=F32) + bg_ref[0]
        lin = jnp.dot(xb, wl_ref[0], preferred_element_type=F32) + bl_ref[0]
        glu = jnp.minimum(glu, SWIGLU_LIMIT)
        lin = jnp.clip(lin, -SWIGLU_LIMIT, SWIGLU_LIMIT)
        act = glu * jax.nn.sigmoid(SWIGLU_ALPHA * glu) * (lin + 1.0)
        y_ref[...] = jnp.dot(act.astype(BF16), wd_ref[0], preferred_element_type=F32) + bd_ref[0]

    @pl.when(i >= n_used)
    def _():
        y_ref[...] = jnp.zeros_like(y_ref)


def _experts(h, row_tok, blk_e, n_used, n_blocks, wg, wl, bg, bl, wd, bd):
    t, d = h.shape
    dff = wg.shape[2]
    tm = TM_E
    e_map = lambda i, be, nu: (be[i], 0, 0)
    grid_spec = pltpu.PrefetchScalarGridSpec(
        num_scalar_prefetch=2, grid=(n_blocks,),
        in_specs=[pl.BlockSpec((tm,), lambda i, be, nu: (i,), memory_space=pltpu.SMEM),
                  pl.BlockSpec((tm,), lambda i, be, nu: (jnp.minimum(i + 1, n_blocks - 1),), memory_space=pltpu.SMEM),
                  pl.BlockSpec(memory_space=pl.ANY),
                  pl.BlockSpec((1, d, dff), e_map), pl.BlockSpec((1, d, dff), e_map),
                  pl.BlockSpec((1, 1, dff), e_map), pl.BlockSpec((1, 1, dff), e_map),
                  pl.BlockSpec((1, dff, d), e_map), pl.BlockSpec((1, 1, d), e_map)],
        out_specs=pl.BlockSpec((tm, d), lambda i, be, nu: (i, 0)),
        scratch_shapes=[pltpu.VMEM((2, tm, d), F32), pltpu.SemaphoreType.DMA((2,))])
    return pl.pallas_call(
        _expert_kernel, grid_spec=grid_spec,
        out_shape=jax.ShapeDtypeStruct((n_blocks * tm, d), F32),
        compiler_params=_cparams(("arbitrary",)), name="moe_experts",
    )(blk_e, n_used, row_tok, row_tok, h, wg, wl, bg, bl, wd, bd)


def _combine_kernel(d_cur_ref, d_nxt_ref, y_hbm, r_ref, tg_ref, p_ref, wpg_ref, wpp_ref, gf_ref, o_ref, ybuf, sem):
    i = pl.program_id(0)
    n = pl.num_programs(0)
    tm = ybuf.shape[2]

    def gather(d_ref, slot):
        def body(r, c):
            for k in range(TOP_K):
                pltpu.make_async_copy(y_hbm.at[pl.ds(d_ref[r * TOP_K + k], 1), :],
                                      ybuf.at[slot, k, pl.ds(r, 1), :], sem.at[slot]).start()
            return c
        lax.fori_loop(0, tm, body, 0)

    @pl.when(i == 0)
    def _():
        gather(d_cur_ref, 0)

    @pl.when(i + 1 < n)
    def _():
        gather(d_nxt_ref, (i + 1) % 2)

    slot = i % 2
    for k in range(TOP_K):
        pltpu.make_async_copy(y_hbm.at[pl.ds(0, tm), :], ybuf.at[slot, k], sem.at[slot]).wait()
    tg = tg_ref[...]
    r = r_ref[...]
    for k in range(TOP_K):
        r = r + tg[:, k:k + 1] * ybuf[slot, k]
    gate = jax.nn.sigmoid(jnp.dot(r.astype(BF16), wpg_ref[...], preferred_element_type=F32))
    r = r + gate * jnp.dot(p_ref[...].astype(BF16), wpp_ref[...], preferred_element_type=F32)
    o_ref[...] = r * lax.rsqrt(jnp.mean(r * r, axis=-1, keepdims=True) + RMS_EPS) * gf_ref[...]


def _combine(y, dest, r1, tg, p, wpg, wpp, g_final, tm):
    t, d = r1.shape
    nt = t // tm
    row = lambda w_: pl.BlockSpec((tm, w_), lambda i: (i, 0))
    return pl.pallas_call(
        _combine_kernel, grid=(nt,),
        in_specs=[pl.BlockSpec((tm * TOP_K,), lambda i: (i,), memory_space=pltpu.SMEM),
                  pl.BlockSpec((tm * TOP_K,), lambda i: (jnp.minimum(i + 1, nt - 1),), memory_space=pltpu.SMEM),
                  pl.BlockSpec(memory_space=pl.ANY),
                  row(d), row(TOP_K), row(p.shape[1]),
                  _const_spec(wpg.shape), _const_spec(wpp.shape), _const_spec((1, d))],
        out_specs=row(d),
        out_shape=jax.ShapeDtypeStruct((t, d), F32),
        scratch_shapes=[pltpu.VMEM((2, TOP_K, tm, d), F32), pltpu.SemaphoreType.DMA((2,))],
        compiler_params=_cparams(("arbitrary",)), name="moe_combine",
    )(dest, dest, y, r1, tg, p, wpg, wpp, g_final.reshape(1, d))


def _ffn(x, om, on8, p, wts, tm):
    r1, h2, ti, tg = _mix(x, om, on8, wts["wom"], wts["won8"], wts["g_ffn"], wts["w_router"], wts["b_router"], tm)
    row_tok, dest, blk_e, n_used, n_blocks = _route(ti, TM_E)
    y = _experts(h2, row_tok, blk_e, n_used, n_blocks, wts["wg"], wts["wl"], wts["bg"], wts["bl"], wts["wd"], wts["bd"])
    return _combine(y, dest, r1, tg, p, wts["wpg"], wts["wpp"], wts["g_final"], tm)


KMEAN_PAGES = 8
PAGES_PER_MOBA = MOBA_BLOCK // PAGE_SIZE
SEL_PER_PAGE = PAGE_SIZE // SEL_BLOCK


def _kmean_kernel(pt_ref, *refs):
    x_refs, o_ref = refs[:KMEAN_PAGES], refs[KMEAN_PAGES]
    for blk in range(KMEAN_PAGES // PAGES_PER_MOBA):
        s = None
        for k in range(PAGES_PER_MOBA):
            part = jnp.sum(x_refs[blk * PAGES_PER_MOBA + k][...], axis=0)
            s = part if s is None else s + part
        o_ref[blk] = s * (1.0 / MOBA_BLOCK)


def _moba_kmean(cache_k, page_table):
    b, n_pages = page_table.shape
    assert n_pages % KMEAN_PAGES == 0
    _, _, ps, h, d = cache_k.shape

    def page_spec(k):
        return pl.BlockSpec((None, None, ps, h, d), lambda bi, i, pt: (0, pt[bi, i * KMEAN_PAGES + k], 0, 0, 0))

    nb_step = KMEAN_PAGES // PAGES_PER_MOBA
    grid_spec = pltpu.PrefetchScalarGridSpec(
        num_scalar_prefetch=1, grid=(b, n_pages // KMEAN_PAGES),
        in_specs=[page_spec(k) for k in range(KMEAN_PAGES)],
        out_specs=pl.BlockSpec((None, nb_step, h, d), lambda bi, i, pt: (bi, i, 0, 0)))
    return pl.pallas_call(
        _kmean_kernel, grid_spec=grid_spec,
        out_shape=jax.ShapeDtypeStruct((b, n_pages // PAGES_PER_MOBA, h, d), F32),
        compiler_params=_cparams(("parallel", "parallel")), name="moba_kmean",
    )(page_table, *([cache_k] * KMEAN_PAGES))


def _dec_select_kernel(qm_ref, kmean_ref, qn_ref, kc_ref, vc_ref, mid_ref, oc_ref, pcg_ref, *, n_cmp):
    gate = jnp.sum(kmean_ref[...] * qm_ref[...][None], axis=-1)
    blk = lax.broadcasted_iota(jnp.int32, gate.shape, 0)
    _, ids = _topk_mask(gate, blk >= 0, MOBA_TOPK, blk, 0)
    for k in range(MOBA_TOPK):
        mid_ref[k:k + 1, :] = ids[k]
    lc = lax.dot_general(qn_ref[...], kc_ref[...], NT, preferred_element_type=F32)
    ok = lax.broadcasted_iota(jnp.int32, lc.shape, 1) < n_cmp
    mc = jnp.max(jnp.where(ok, lc, NEG), axis=1, keepdims=True)
    pc = jnp.where(ok, jnp.exp(lc - mc), 0.0)
    lsum = jnp.sum(pc, axis=1, keepdims=True)
    pc = pc / jnp.where(lsum > 0.0, lsum, 1.0)
    oc_ref[...] = jnp.dot(pc.astype(BF16), vc_ref[...], preferred_element_type=F32)
    for g in range(2):
        pcg_ref[g:g + 1, :] = jnp.sum(pc[4 * g:4 * g + 4], axis=0, keepdims=True)


def _dec_select(qm, kmean, qn8, kcmp, vcmp, n_cmp):
    b, n_blk = kmean.shape[:2]
    n_cp = kcmp.shape[1]
    per_b = lambda *s: pl.BlockSpec((None,) + s, lambda bi: (bi,) + (0,) * len(s))
    return pl.pallas_call(
        functools.partial(_dec_select_kernel, n_cmp=n_cmp), grid=(b,),
        in_specs=[per_b(8, HEAD_DIM), per_b(n_blk, 8, HEAD_DIM), per_b(8, LANES), per_b(n_cp, LANES), per_b(n_cp, LANES)],
        out_specs=[per_b(MOBA_TOPK, 8), per_b(8, LANES), per_b(2, n_cp)],
        out_shape=[jax.ShapeDtypeStruct((b, MOBA_TOPK, 8), jnp.int32), jax.ShapeDtypeStruct((b, 8, LANES), F32),
                   jax.ShapeDtypeStruct((b, 2, n_cp), F32)],
        compiler_params=_cparams(("parallel",)), name="dec_select",
    )(qm, kmean, qn8, kcmp, vcmp)


def _dec_topn_kernel(pcg_ref, ov_ref, o_ref, *, pos, n_sel):
    imp = jnp.dot(pcg_ref[...], ov_ref[...], precision=HI, preferred_element_type=F32)
    sid = lax.broadcasted_iota(jnp.int32, imp.shape, 1)
    cur = pos // SEL_BLOCK
    forced = (sid == 0) | (sid == cur) | (sid == cur - 1)
    valid = (sid * SEL_BLOCK <= pos) & (sid < n_sel)
    n_top = min(SEL_TOPN, n_sel)
    _, ids = _topk_mask(jnp.where(forced, FORCE, imp), valid, n_top, sid, 1)
    k_id = lax.broadcasted_iota(jnp.int32, o_ref.shape, 1)
    out = jnp.full(o_ref.shape, imp.shape[1], jnp.int32)
    for k in range(n_top):
        out = jnp.where(k_id == k, ids[k], out)
    o_ref[...] = out


def _dec_topn(pcg, pos, n_sel):
    rows, n_cp = pcg.shape
    n_sp = -(-n_sel // LANES) * LANES
    ov = jnp.asarray(_overlap_matrix(n_cp, n_sp))
    return pl.pallas_call(
        functools.partial(_dec_topn_kernel, pos=pos, n_sel=n_sel),
        out_shape=jax.ShapeDtypeStruct((rows, SEL_TOPN), jnp.int32), name="dec_topn",
        compiler_params=pltpu.CompilerParams(vmem_limit_bytes=VMEM_LIMIT),
    )(pcg, ov)


def _dec_attn_kernel(pt_ref, mid_ref, sid_ref, qm_ref, kmn_ref, vmn_ref, qn_ref, new_ref, gt_ref, oc_ref,
                     wk_ref, wv_ref, pbm_ref, pbn_ref, wb_ref, b0_ref,
                     mk_hbm, mv_hbm, sk_hbm, sv_hbm, om_ref, on_ref,
                     kmbuf, vmbuf, ksbuf, vsbuf, sem, *, n_blk, n_selp):
    bi = pl.program_id(0)
    n_top = ksbuf.shape[1]

    def moba_copies(hd, k):
        blk = jnp.minimum(mid_ref[bi, k * 8 + hd], n_blk - 1)
        out = []
        for half in range(PAGES_PER_MOBA):
            page = pt_ref[bi, blk * PAGES_PER_MOBA + half]
            rows = pl.ds(half * PAGE_SIZE, PAGE_SIZE)
            out.append(pltpu.make_async_copy(mk_hbm.at[0, page, :, hd, :], kmbuf.at[hd, k, rows, :], sem.at[0]))
            out.append(pltpu.make_async_copy(mv_hbm.at[0, page, :, hd, :], vmbuf.at[hd, k, rows, :], sem.at[0]))
        return out

    def sel_copies(g, k):
        s = jnp.minimum(sid_ref[bi, g * n_top + k], n_selp - 1)
        page = pt_ref[bi, s // SEL_PER_PAGE]
        rows = pl.ds((s % SEL_PER_PAGE) * SEL_BLOCK, SEL_BLOCK)
        return [pltpu.make_async_copy(sk_hbm.at[0, page, rows, g, :], ksbuf.at[g, k], sem.at[1]),
                pltpu.make_async_copy(sv_hbm.at[0, page, rows, g, :], vsbuf.at[g, k], sem.at[1])]

    copies = [c for hd in range(8) for k in range(MOBA_TOPK) for c in moba_copies(hd, k)]
    copies += [c for g in range(2) for k in range(n_top) for c in sel_copies(g, k)]
    for c in copies:
        c.start()
    for c in copies:
        c.wait()

    b0 = b0_ref[...]

    for hd in range(8):
        qf = qm_ref[hd:hd + 1, :] * SCALE
        qh = qf.astype(BF16)
        s_self = jnp.sum(qf * kmn_ref[hd:hd + 1, :], axis=1, keepdims=True) + b0[hd:hd + 1]
        ss = []
        for k in range(MOBA_TOPK):
            raw = mid_ref[bi, k * 8 + hd]
            blk = jnp.minimum(raw, n_blk - 1)
            s = lax.dot_general(qh, kmbuf[hd, k].astype(BF16), NT, preferred_element_type=F32)
            s = s + pbm_ref[blk][hd:hd + 1, :]
            ss.append(jnp.where(raw < n_blk, s, NEG))
        m = s_self
        for s in ss:
            m = jnp.maximum(m, jnp.max(s, axis=1, keepdims=True))
        p_self = jnp.exp(s_self - m)
        l = p_self
        acc = p_self * vmn_ref[hd:hd + 1, :]
        for k, s in enumerate(ss):
            p = jnp.exp(s - m)
            l = l + jnp.sum(p, axis=1, keepdims=True)
            acc = acc + jnp.dot(p.astype(BF16), vmbuf[hd, k].astype(BF16), preferred_element_type=F32)
        om_ref[hd:hd + 1, :] = acc / l

    gt = gt_ref[...]
    for g in range(2):
        lanes = slice(g * HEAD_DIM, (g + 1) * HEAD_DIM)
        hs = slice(4 * g, 4 * g + 4)
        qh = qn_ref[hs, lanes]
        qf = qh.astype(F32)
        ks_new, vs_new = new_ref[0:1, lanes], new_ref[1:2, lanes]
        kw_new, vw_new = new_ref[2:3, lanes], new_ref[3:4, lanes]

        def attend(logit_tiles, value_tiles, k_new, v_new):
            s_self = jnp.sum(qf * k_new, axis=1, keepdims=True) + b0[8 + 4 * g:12 + 4 * g]
            m = s_self
            for s in logit_tiles:
                m = jnp.maximum(m, jnp.max(s, axis=1, keepdims=True))
            p_self = jnp.exp(s_self - m)
            l = p_self
            acc = p_self * v_new
            for s, v in zip(logit_tiles, value_tiles):
                p = jnp.exp(s - m)
                l = l + jnp.sum(p, axis=1, keepdims=True)
                acc = acc + jnp.dot(p.astype(BF16), v, preferred_element_type=F32)
            return acc / l

        ss, vv = [], []
        for k in range(n_top):
            raw = sid_ref[bi, g * n_top + k]
            sblk = jnp.minimum(raw, n_selp - 1)
            s = lax.dot_general(qh, ksbuf[g, k].astype(BF16), NT, preferred_element_type=F32)
            s = s + pbn_ref[sblk][hs, :]
            ss.append(jnp.where(raw < n_selp, s, NEG))
            vv.append(vsbuf[g, k].astype(BF16))
        osel = attend(ss, vv, ks_new, vs_new)

        sw = lax.dot_general(qh, wk_ref[:, lanes].astype(BF16), NT, preferred_element_type=F32) + wb_ref[hs, :]
        ow = attend([sw], [wv_ref[:, lanes].astype(BF16)], kw_new, vw_new)

        oc = oc_ref[hs, lanes]
        for r in range(4):
            hd = 4 * g + r
            o = (gt[:, hd * 3:hd * 3 + 1] * oc[r:r + 1] + gt[:, hd * 3 + 1:hd * 3 + 2] * osel[r:r + 1]
                 + gt[:, hd * 3 + 2:hd * 3 + 3] * ow[r:r + 1])
            on_ref[hd:hd + 1, :] = o


def _dec_attn(page_table, mids, sids, qm, kmn, vmn, qn8, new_rows, gates, oc, wk, wv, pbm, pbn, wb, b0,
              cache_mk, cache_mv, cache_sk, cache_sv):
    b = page_table.shape[0]
    n_blk = pbm.shape[0]
    n_selp = pbn.shape[0]
    n_top = sids.shape[1] // 2
    win = wk.shape[1]
    per_b = lambda *s: pl.BlockSpec((None,) + s, lambda bi, *_: (bi,) + (0,) * len(s))
    const = lambda a: pl.BlockSpec(a.shape, lambda bi, *_: (0,) * a.ndim)
    any_spec = pl.BlockSpec(memory_space=pl.ANY)
    grid_spec = pltpu.PrefetchScalarGridSpec(
        num_scalar_prefetch=3, grid=(b,),
        in_specs=[per_b(8, HEAD_DIM), per_b(8, HEAD_DIM), per_b(8, HEAD_DIM), per_b(8, LANES), per_b(4, LANES),
                  per_b(1, LANES), per_b(8, LANES), per_b(win, LANES), per_b(win, LANES),
                  const(pbm), const(pbn), const(wb), const(b0), any_spec, any_spec, any_spec, any_spec],
        out_specs=[per_b(8, HEAD_DIM), per_b(8, HEAD_DIM)],
        scratch_shapes=[pltpu.VMEM((8, MOBA_TOPK, MOBA_BLOCK, HEAD_DIM), F32),
                        pltpu.VMEM((8, MOBA_TOPK, MOBA_BLOCK, HEAD_DIM), F32),
                        pltpu.VMEM((2, n_top, SEL_BLOCK, HEAD_DIM), F32),
                        pltpu.VMEM((2, n_top, SEL_BLOCK, HEAD_DIM), F32),
                        pltpu.SemaphoreType.DMA((2,))])
    return pl.pallas_call(
        functools.partial(_dec_attn_kernel, n_blk=n_blk, n_selp=n_selp), grid_spec=grid_spec,
        out_shape=[jax.ShapeDtypeStruct((b, 8, HEAD_DIM), F32), jax.ShapeDtypeStruct((b, 8, HEAD_DIM), F32)],
        compiler_params=_cparams(("arbitrary",)), name="dec_attn",
    )(page_table, mids, sids, qm, kmn, vmn, qn8, new_rows, gates, oc, wk, wv, pbm, pbn, wb, b0,
      cache_mk, cache_mv, cache_sk, cache_sv)


def kernel(x_prompt, x_sample, p_prompt, p_sample, cache_moba_k, cache_moba_v, cache_nsa_cmp_k, cache_nsa_cmp_v, cache_nsa_sel_k, cache_nsa_sel_v, state_nsa_win_k, state_nsa_win_v, page_table, rel_bias, g_mix, w_in, w_out, cmp_w1_k, cmp_b1_k, cmp_w2_k, cmp_pos_k, cmp_w1_v, cmp_b1_v, cmp_w2_v, cmp_pos_v, g_ffn, w_router, b_router, w_gate_up, b_gate_up, w_down, b_down, w_ple_proj, w_ple_gate, g_final):
    assert rel_bias.shape == (N_BUCKETS, 16) and g_mix.shape[0] == 1, "one layer, 8 MoBA + 8 NSA heads"
    b, t, d = x_prompt.shape
    bd, dec_seq, _ = x_sample.shape
    assert dec_seq == 1 and t % (2 * MOBA_BLOCK) == 0
    n_pages = page_table.shape[1]
    past = n_pages * PAGE_SIZE
    win = state_nsa_win_k.shape[2]

    bv = rel_bias[_t5_bucket(jnp.arange(MAX_DISTANCE + 1))]
    n_tbm = -(-(MAX_DISTANCE + MOBA_BLOCK - 1) // TQ_M) + 1
    tbm = _toeplitz_tiles(bv[:, :8], TQ_M, MOBA_BLOCK, n_tbm)
    n_tbn = -(-(MAX_DISTANCE + TK_N - 1) // TQ_N) + 1
    tbn = jnp.transpose(_toeplitz_tiles(bv[:, 8:], TQ_N, TK_N, n_tbn), (1, 0, 2, 3))
    n_wm = -(-(WINDOW + TK_N) // TQ_N)
    dist = np.stack([TQ_N * m + np.arange(TQ_N)[:, None] - np.arange(TK_N)[None, :] for m in range(n_wm)])
    wm = jnp.asarray(np.where(dist <= WINDOW, 0.0, NEG).astype(np.float32))
    w_proj = _proj_weight(w_in[0])
    wo = w_out[0]
    won = wo[512:].reshape(NSA_HEADS, HEAD_DIM, d)
    zn = jnp.zeros_like(won)
    grp0 = (jnp.arange(NSA_HEADS) < 4)[:, None, None]
    wgu, bgu = w_gate_up[0], b_gate_up[0]
    wts = dict(
        wom=wo[:512].astype(BF16),
        won8=jnp.concatenate([jnp.where(grp0, won, zn), jnp.where(grp0, zn, won)], axis=1).astype(BF16),
        g_ffn=g_ffn[0], w_router=w_router[0], b_router=b_router[0],
        wg=wgu[:, :, 0::2].astype(BF16), wl=wgu[:, :, 1::2].astype(BF16),
        bg=bgu[:, None, 0::2], bl=bgu[:, None, 1::2],
        wd=w_down[0].astype(BF16), bd=b_down[0][:, None, :],
        wpg=w_ple_gate[0].astype(BF16), wpp=w_ple_proj[0].astype(BF16), g_final=g_final)
    ck = _cmp_const(cmp_pos_k[0], cmp_w1_k[0], cmp_b1_k[0])
    cv = _cmp_const(cmp_pos_v[0], cmp_w1_v[0], cmp_b1_v[0])

    def pad_heads(o):
        z = jnp.zeros_like(o)
        return jnp.concatenate([jnp.where(grp0, o, z), jnp.where(grp0, z, o)], axis=-1).astype(BF16)

    xp = x_prompt.reshape(b * t, d)
    qm, km, vm, kmb, vmb, qn8, kv, kvb, gt, kmean = _project(xp, g_mix, w_proj, 2 * MOBA_BLOCK, True)
    om = _moba_prompt(qm.reshape(b, t, 512), kmb.reshape(b, t, 512), vmb.reshape(b, t, 512),
                      kmean.reshape(b, t // MOBA_BLOCK, 512), tbm)
    pt_p = jnp.arange(b * t // PAGE_SIZE, dtype=jnp.int32).reshape(b, t // PAGE_SIZE)
    kcmp = _compress(_chunk_layout(kv[0].reshape(-1, PAGE_SIZE, 2, HEAD_DIM)), pt_p, cmp_w1_k[0], ck, cmp_w2_k[0])
    vcmp = _compress(_chunk_layout(kv[1].reshape(-1, PAGE_SIZE, 2, HEAD_DIM)), pt_p, cmp_w1_v[0], cv, cmp_w2_v[0])
    seq = lambda a: a.reshape(b, t, LANES)
    on8 = _nsa_prompt(qn8, gt, kcmp, vcmp, seq(kvb[2]), seq(kvb[3]), seq(kvb[4]), seq(kvb[5]), tbn, wm,
                      (t - CMP_LEN) // CMP_STRIDE + 1)
    y_p = _ffn(xp, om.reshape(b * t, 512), on8, p_prompt[0].reshape(b * t, -1), wts, 256)

    xs = x_sample.reshape(bd, d)
    qm_s, km_s, vm_s, _, _, qn8_s, kv_s, _, gt_s = _project(xs, g_mix, w_proj, bd, False)
    kmean_s = _moba_kmean(cache_moba_k, page_table)
    kcmp_s = _compress(_chunk_layout(cache_nsa_cmp_k[0]), page_table, cmp_w1_k[0], ck, cmp_w2_k[0])
    vcmp_s = _compress(_chunk_layout(cache_nsa_cmp_v[0]), page_table, cmp_w1_v[0], cv, cmp_w2_v[0])
    heads = lambda a: a.reshape(bd, 8, HEAD_DIM)
    qn8_sb = jnp.transpose(qn8_s, (1, 0, 2))
    mids, oc_s, pcg = _dec_select(heads(qm_s), kmean_s, qn8_sb, kcmp_s, vcmp_s,
                                  (past + 1 - CMP_LEN) // CMP_STRIDE + 1)
    n_sel_s = -(-(past + 1) // SEL_BLOCK)
    sids = _dec_topn(pcg.reshape(bd * 2, -1), past, n_sel_s)
    n_blk = past // MOBA_BLOCK
    n_selp = past // SEL_BLOCK
    kpos_m = np.arange(n_blk * MOBA_BLOCK).reshape(n_blk, MOBA_BLOCK)
    pbm = jnp.transpose(jnp.take(bv[:, :8], jnp.asarray(np.minimum(past - kpos_m, MAX_DISTANCE)), axis=0), (0, 2, 1))
    kpos_n = np.arange(n_selp * SEL_BLOCK).reshape(n_selp, SEL_BLOCK)
    pbn = jnp.transpose(jnp.take(bv[:, 8:], jnp.asarray(np.minimum(past - kpos_n, MAX_DISTANCE)), axis=0), (0, 2, 1))
    wb = jnp.take(bv[:, 8:], jnp.asarray(np.minimum(win - np.arange(win), MAX_DISTANCE)), axis=0).T
    b0 = bv[0][:, None]
    new_rows = jnp.stack([kv_s[2], kv_s[3], kv_s[4], kv_s[5]], axis=1)
    om_s, on_s = _dec_attn(page_table, mids.reshape(bd, -1), sids.reshape(bd, -1), heads(qm_s), heads(km_s),
                           heads(vm_s), qn8_sb, new_rows, gt_s.reshape(bd, 1, LANES), oc_s,
                           state_nsa_win_k[0].reshape(bd, win, LANES), state_nsa_win_v[0].reshape(bd, win, LANES),
                           pbm, pbn, wb, b0, cache_moba_k, cache_moba_v, cache_nsa_sel_k, cache_nsa_sel_v)
    y_s = _ffn(xs, om_s.reshape(bd, 512).astype(BF16), pad_heads(jnp.transpose(on_s, (1, 0, 2))),
               p_sample[0].reshape(bd, -1), wts, bd)

    wp = min(WINDOW, t)
    mh = lambda a, n, h: a.reshape(1, n, -1, h, HEAD_DIM)
    kw_new = kv_s[4].reshape(bd, 1, 2, HEAD_DIM)
    vw_new = kv_s[5].reshape(bd, 1, 2, HEAD_DIM)
    win_k = jnp.concatenate([state_nsa_win_k[0], kw_new], axis=1)[:, -win:][None]
    win_v = jnp.concatenate([state_nsa_win_v[0], vw_new], axis=1)[:, -win:][None]
    return (y_p.reshape(b, t, d), y_s.reshape(bd, 1, d),
            mh(km, b, 8), mh(vm, b, 8), mh(kv[0], b, 2), mh(kv[1], b, 2), mh(kv[2], b, 2), mh(kv[3], b, 2),
            mh(kv[4], b, 2)[:, :, -wp:], mh(kv[5], b, 2)[:, :, -wp:],
            mh(km_s, bd, 8), mh(vm_s, bd, 8), mh(kv_s[0], bd, 2), mh(kv_s[1], bd, 2), mh(kv_s[2], bd, 2),
            mh(kv_s[3], bd, 2), win_k, win_v)
```

```python
import functools
import math

import numpy as np
import jax
import jax.numpy as jnp
from jax import lax
from jax.experimental import pallas as pl
from jax.experimental.pallas import tpu as pltpu

PAGE_SIZE = 128
HEAD_DIM = 64
MOBA_BLOCK = 256
MOBA_TOPK = 3
CMP_LEN = 32
CMP_STRIDE = 16
CMP_HIDDEN = 256
SEL_BLOCK = 64
SEL_TOPN = 16
WINDOW = 512
N_BUCKETS = 32
MAX_DISTANCE = 1024
N_EXPERTS = 32
TOP_K = 4
SWIGLU_LIMIT = 7.0
SWIGLU_ALPHA = 1.702
RMS_EPS = 1e-6
NEG = -1e30
FORCE = 1e30
SCALE = HEAD_DIM ** -0.5
LANES = 128
VMEM_LIMIT = 56 * 1024 * 1024

F32 = jnp.float32
BF16 = jnp.bfloat16
HI = lax.Precision.HIGHEST
NT = (((1,), (1,)), ((), ()))


def _cparams(sem):
    return pltpu.CompilerParams(dimension_semantics=sem, vmem_limit_bytes=VMEM_LIMIT)


def _const_spec(shape):
    n = len(shape)
    return pl.BlockSpec(shape, lambda *_: (0,) * n)


def _t5_bucket(dist):
    max_exact = N_BUCKETS // 2
    d = jnp.maximum(dist, 0)
    df = jnp.maximum(d, max_exact).astype(jnp.float32)
    large = max_exact + (jnp.log(df / max_exact) / math.log(MAX_DISTANCE / max_exact)
                         * (N_BUCKETS - max_exact)).astype(jnp.int32)
    return jnp.where(d < max_exact, d, jnp.minimum(large, N_BUCKETS - 1))


def _topk_mask(score, valid, k, idx_iota, axis):
    n = float(score.shape[axis])
    low = -3e38
    pos_f = idx_iota.astype(F32)
    work = jnp.where(valid, score, low)
    sel = jnp.zeros(score.shape, F32)
    ids = []
    for _ in range(k):
        mx = jnp.max(work, axis=axis, keepdims=True)
        cand = (work == mx) & (mx > low)
        idx = jnp.min(jnp.where(cand, pos_f, n), axis=axis, keepdims=True)
        hit = pos_f == idx
        sel = jnp.where(hit, 1.0, sel)
        work = jnp.where(hit, low, work)
        ids.append(idx.astype(jnp.int32))
    return sel, ids


C_QM, C_KM, C_VM, C_QN, C_KV, C_GT, C_END = 0, 512, 1024, 1536, 2560, 3328, 3456


def _proj_weight(w_in):
    d = w_in.shape[0]
    qn = w_in[:, 1536:2048].reshape(d, 8, HEAD_DIM)
    z = jnp.zeros((d, 8, HEAD_DIM), w_in.dtype)
    grp0 = (jnp.arange(8) < 4)[None, :, None]
    qn_pad = jnp.concatenate([jnp.where(grp0, qn, z), jnp.where(grp0, z, qn)], axis=-1).reshape(d, 1024)
    gates = jnp.pad(w_in[:, 2816:2840], ((0, 0), (0, LANES - 24)))
    return jnp.concatenate([w_in[:, :1536], qn_pad, w_in[:, 2048:2816], gates], axis=1).astype(BF16)


def _proj_kernel(x_ref, g_ref, w_ref, qm_ref, km_ref, vm_ref, kmb_ref, vmb_ref, qn_ref,
                 kv_ref, kvb_ref, gt_ref, *rest, with_kmean):
    x = x_ref[...]
    h = x * lax.rsqrt(jnp.mean(x * x, axis=-1, keepdims=True) + RMS_EPS) * g_ref[...]
    z = jnp.dot(h.astype(BF16), w_ref[...], preferred_element_type=F32)
    qm_ref[...] = z[:, C_QM:C_KM]
    km = z[:, C_KM:C_VM]
    vm = z[:, C_VM:C_QN]
    km_ref[...] = km
    vm_ref[...] = vm
    kmb_ref[...] = km.astype(BF16)
    vmb_ref[...] = vm.astype(BF16)
    for hd in range(8):
        qn_ref[hd] = (z[:, C_QN + hd * LANES:C_QN + (hd + 1) * LANES] * SCALE).astype(BF16)
    for j in range(6):
        blk = z[:, C_KV + j * LANES:C_KV + (j + 1) * LANES]
        kv_ref[j] = blk
        kvb_ref[j] = blk.astype(BF16)
    gt_ref[...] = jax.nn.sigmoid(z[:, C_GT:C_END])
    if with_kmean:
        (kmean_ref,) = rest
        nblk = km.shape[0] // MOBA_BLOCK
        for i in range(nblk):
            kmean_ref[0, i:i + 1, :] = jnp.mean(km[i * MOBA_BLOCK:(i + 1) * MOBA_BLOCK], axis=0, keepdims=True)


def _project(x, g, w, tm, with_kmean):
    t, d = x.shape
    nt = t // tm
    row = lambda w_: pl.BlockSpec((tm, w_), lambda i: (i, 0))
    out_shape = [
        jax.ShapeDtypeStruct((t, 512), F32), jax.ShapeDtypeStruct((t, 512), F32), jax.ShapeDtypeStruct((t, 512), F32),
        jax.ShapeDtypeStruct((t, 512), BF16), jax.ShapeDtypeStruct((t, 512), BF16),
        jax.ShapeDtypeStruct((8, t, LANES), BF16),
        jax.ShapeDtypeStruct((6, t, LANES), F32), jax.ShapeDtypeStruct((6, t, LANES), BF16),
        jax.ShapeDtypeStruct((t, LANES), F32),
    ]
    out_specs = [row(512), row(512), row(512), row(512), row(512),
                 pl.BlockSpec((8, tm, LANES), lambda i: (0, i, 0)),
                 pl.BlockSpec((6, tm, LANES), lambda i: (0, i, 0)),
                 pl.BlockSpec((6, tm, LANES), lambda i: (0, i, 0)),
                 row(LANES)]
    if with_kmean:
        nb = tm // MOBA_BLOCK
        out_shape.append(jax.ShapeDtypeStruct((nt, nb, 512), F32))
        out_specs.append(pl.BlockSpec((1, nb, 512), lambda i: (i, 0, 0)))
    return pl.pallas_call(
        functools.partial(_proj_kernel, with_kmean=with_kmean),
        grid=(nt,),
        in_specs=[row(d), _const_spec((1, d)), _const_spec(w.shape)],
        out_specs=out_specs, out_shape=out_shape,
        compiler_params=_cparams(("parallel",)), name="proj",
    )(x, g, w)


def _toeplitz_kernel(v_ref, o_ref):
    rows, cols = o_ref.shape[1:]
    x = jnp.broadcast_to(v_ref[0], (rows, v_ref.shape[2]))
    o_ref[0] = pltpu.roll(x, 0, 1, stride=1, stride_axis=0)[:, :cols]


def _toeplitz_tiles(bv, rows, cols, n_tiles, head_major):
    length = rows + cols
    assert length % LANES == 0
    k = np.arange(length)
    off = np.where(k < cols, -k, length - k)
    dist = np.stack([rows * m + off for m in range(n_tiles)])
    v = jnp.take(bv, jnp.asarray(np.clip(dist, 0, MAX_DISTANCE)), axis=0)
    v = jnp.where(jnp.asarray(dist >= 0)[..., None], v, NEG)
    v = jnp.transpose(v, (2, 0, 1) if head_major else (0, 2, 1))
    lead = v.shape[:2]
    n = lead[0] * lead[1]
    tiles = pl.pallas_call(
        _toeplitz_kernel, grid=(n,),
        in_specs=[pl.BlockSpec((1, 1, length), lambda i: (i, 0, 0))],
        out_specs=pl.BlockSpec((1, rows, cols), lambda i: (i, 0, 0)),
        out_shape=jax.ShapeDtypeStruct((n, rows, cols), F32),
        compiler_params=_cparams(("parallel",)), name="bias_tiles",
    )(v.reshape(n, 1, length))
    return tiles.reshape(lead + (rows, cols))


TQ_M = MOBA_BLOCK


def _moba_kernel(q_ref, k_ref, v_ref, kmean_ref, tb_ref, o_ref, *, nb, n_tb):
    qi = pl.program_id(2)
    q2 = q_ref[0]
    lane = lax.broadcasted_iota(jnp.int32, q2.shape, 1)
    q_st = jnp.concatenate([jnp.where(lane < HEAD_DIM, q2, 0.0), jnp.where(lane >= HEAD_DIM, q2, 0.0)], axis=0)
    blk_lane = lax.broadcasted_iota(jnp.int32, (2 * TQ_M, nb), 1)
    gate = lax.dot_general(q_st, kmean_ref[0], NT, precision=HI, preferred_element_type=F32)
    sel, _ = _topk_mask(gate, blk_lane < qi, MOBA_TOPK, blk_lane, 1)
    qb = (q_st * SCALE).astype(BF16)

    def kv_rows(ref, j):
        return ref[0, pl.ds(pl.multiple_of(j * MOBA_BLOCK, MOBA_BLOCK), MOBA_BLOCK), :]

    def logits(j):
        s = lax.dot_general(qb, kv_rows(k_ref, j), NT, preferred_element_type=F32)
        m_id = jnp.minimum(qi - j, n_tb - 1)
        return s + jnp.concatenate([tb_ref[0, m_id], tb_ref[1, m_id]], axis=0)

    st = _softmax_update(logits(qi), None, kv_rows(v_ref, qi), True)

    def body(j, st):
        selj = jnp.sum(jnp.where(blk_lane == j, sel, 0.0), axis=1, keepdims=True) > 0.5
        return _softmax_update(jnp.where(selj, logits(j), NEG), st, kv_rows(v_ref, j), False)

    _, l, acc = lax.fori_loop(0, qi, body, st)
    o = acc / l
    o_ref[0] = jnp.where(lane < HEAD_DIM, o[:TQ_M], o[TQ_M:]).astype(o_ref.dtype)


def _moba_prompt(qm, kmb, vmb, kmean, tb):
    b, t, _ = qm.shape
    nb = t // MOBA_BLOCK
    n_tb = tb.shape[1]
    return pl.pallas_call(
        functools.partial(_moba_kernel, nb=nb, n_tb=n_tb),
        grid=(4, b, t // TQ_M),
        in_specs=[pl.BlockSpec((1, TQ_M, LANES), lambda p, bi, qi: (bi, qi, p)),
                  pl.BlockSpec((1, t, LANES), lambda p, bi, qi: (bi, 0, p)),
                  pl.BlockSpec((1, t, LANES), lambda p, bi, qi: (bi, 0, p)),
                  pl.BlockSpec((1, nb, LANES), lambda p, bi, qi: (bi, 0, p)),
                  pl.BlockSpec((2, n_tb, TQ_M, MOBA_BLOCK), lambda p, bi, qi: (p, 0, 0, 0))],
        out_specs=pl.BlockSpec((1, TQ_M, LANES), lambda p, bi, qi: (bi, qi, p)),
        out_shape=jax.ShapeDtypeStruct((b, t, 512), BF16),
        compiler_params=_cparams(("parallel", "parallel", "parallel")), name="moba_prompt",
    )(qm, kmb, vmb, kmean, tb)


CHUNKS_PER_PAGE = PAGE_SIZE // CMP_STRIDE
CMP_PAGES = 16


def _chunk_layout(rows):
    n = rows.shape[0]
    x = rows.reshape(n, CHUNKS_PER_PAGE, CMP_STRIDE, 2, HEAD_DIM)
    return jnp.transpose(x, (0, 3, 1, 2, 4)).reshape(n, 2 * CHUNKS_PER_PAGE, CMP_STRIDE * HEAD_DIM).astype(BF16)


def _cmp_const_kernel(pe_ref, w1_ref, b1_ref, o_ref):
    o_ref[...] = jnp.sum(w1_ref[...] * pe_ref[...], axis=0, keepdims=True) + b1_ref[...]


def _cmp_const(pe, w1, b1):
    return pl.pallas_call(
        _cmp_const_kernel, out_shape=jax.ShapeDtypeStruct((1, CMP_HIDDEN), F32), name="cmp_const",
    )(pe.reshape(CMP_LEN * HEAD_DIM, 1), w1, b1.reshape(1, CMP_HIDDEN))


def _gelu_tanh(x):
    return 0.5 * x * (1.0 + jnp.tanh(math.sqrt(2.0 / math.pi) * (x + 0.044715 * x * x * x)))


def _compress_kernel(pt_ref, *refs):
    x_refs = refs[:CMP_PAGES + 1]
    wab_ref, c_ref, w2_ref, o_ref = refs[CMP_PAGES + 1:]
    rows_pp = 2 * CHUNKS_PER_PAGE
    x = jnp.concatenate([r[0] for r in x_refs], axis=0)
    ab = jnp.dot(x, wab_ref[...], preferred_element_type=F32)
    a = ab[:CMP_PAGES * rows_pp, :CMP_HIDDEN]
    bm = ab[:, CMP_HIDDEN:]
    n = bm.shape[0]
    nxt = pltpu.roll(bm, n - 1, 0)
    nxt_page = pltpu.roll(bm, n - (rows_pp - CHUNKS_PER_PAGE + 1), 0)
    c_id = lax.broadcasted_iota(jnp.int32, bm.shape, 0) % CHUNKS_PER_PAGE
    b_next = jnp.where(c_id == CHUNKS_PER_PAGE - 1, nxt_page, nxt)[:CMP_PAGES * rows_pp]
    hid = _gelu_tanh(a + b_next + c_ref[...]).astype(BF16)
    hid = hid.reshape(CMP_PAGES, 2, CHUNKS_PER_PAGE, CMP_HIDDEN)
    out = None
    for g in range(2):
        hg = hid[:, g].reshape(CMP_PAGES * CHUNKS_PER_PAGE, CMP_HIDDEN)
        og = jnp.dot(hg, w2_ref[g], preferred_element_type=F32)
        out = og if out is None else out + og
    o_ref[0] = out.astype(o_ref.dtype)


def _compress(xc, page_table, w1, const, w2):
    b, n_pages = page_table.shape
    assert n_pages % CMP_PAGES == 0
    half = CMP_STRIDE * HEAD_DIM
    wab = jnp.concatenate([w1[:half], w1[half:]], axis=1).astype(BF16)
    z = jnp.zeros_like(w2)
    w2p = jnp.stack([jnp.concatenate([w2, z], axis=1), jnp.concatenate([z, w2], axis=1)]).astype(BF16)

    def page_spec(k):
        def imap(bi, i, pt):
            return (pt[bi, jnp.minimum(i * CMP_PAGES + k, n_pages - 1)], 0, 0)
        return pl.BlockSpec((1, 2 * CHUNKS_PER_PAGE, half), imap)

    tile = CMP_PAGES * CHUNKS_PER_PAGE
    grid_spec = pltpu.PrefetchScalarGridSpec(
        num_scalar_prefetch=1, grid=(b, n_pages // CMP_PAGES),
        in_specs=[page_spec(k) for k in range(CMP_PAGES + 1)] + [
            pl.BlockSpec(wab.shape, lambda bi, i, pt: (0, 0)),
            pl.BlockSpec(const.shape, lambda bi, i, pt: (0, 0)),
            pl.BlockSpec(w2p.shape, lambda bi, i, pt: (0, 0, 0))],
        out_specs=pl.BlockSpec((1, tile, LANES), lambda bi, i, pt: (bi, i, 0)))
    return pl.pallas_call(
        _compress_kernel, grid_spec=grid_spec,
        out_shape=jax.ShapeDtypeStruct((b, n_pages * CHUNKS_PER_PAGE, LANES), BF16),
        compiler_params=_cparams(("parallel", "parallel")), name="compress",
    )(page_table, *([xc] * (CMP_PAGES + 1)), wab, const, w2p)


TQ_N = 128
TK_N = 256
NSA_HEADS = 8
assert WINDOW % TK_N == 0 and TK_N % TQ_N == 0
WIN_TILES = WINDOW // TK_N + 1


def _overlap_matrix(n_cmp_pad, n_sel_pad):
    cstart = np.arange(n_cmp_pad)[:, None] * CMP_STRIDE
    sstart = np.arange(n_sel_pad)[None, :] * SEL_BLOCK
    return ((cstart < sstart + SEL_BLOCK) & (cstart + CMP_LEN > sstart)).astype(np.float32)


def _softmax_update(s, state, v, first):
    if first:
        m = jnp.max(s, axis=1, keepdims=True)
        p = jnp.exp(s - m)
        return m, jnp.sum(p, axis=1, keepdims=True), jnp.dot(p.astype(BF16), v, preferred_element_type=F32)
    m0, l0, acc0 = state
    m = jnp.maximum(m0, jnp.max(s, axis=1, keepdims=True))
    alpha = jnp.exp(m0 - m)
    p = jnp.exp(s - m)
    return (m, alpha * l0 + jnp.sum(p, axis=1, keepdims=True),
            alpha * acc0 + jnp.dot(p.astype(BF16), v, preferred_element_type=F32))


def _nsa_kernel(q_ref, gt_ref, kc_ref, vc_ref, ks_ref, vs_ref, kw_ref, vw_ref, ov_ref, tb_ref, wm_ref, o_ref,
                *, n_cmp, n_sel, n_tb):
    qi = pl.program_id(1)
    q0 = qi * TQ_N
    rows = NSA_HEADS * TQ_N
    q8 = q_ref[...].reshape(rows, LANES)
    n_cp = kc_ref.shape[1]
    pos = q0 + lax.broadcasted_iota(jnp.int32, (1, TQ_N, 1), 1)

    lc = lax.dot_general(q8, kc_ref[0], NT, preferred_element_type=F32).reshape(NSA_HEADS, TQ_N, n_cp)
    n_id = lax.broadcasted_iota(jnp.int32, (1, 1, n_cp), 2)
    ok = (n_id * CMP_STRIDE + (CMP_LEN - 1) <= pos) & (n_id < n_cmp)
    mc = jnp.max(jnp.where(ok, lc, NEG), axis=2, keepdims=True)
    pc = jnp.where(ok, jnp.exp(lc - mc), 0.0)
    lsum = jnp.sum(pc, axis=2, keepdims=True)
    pc = pc / jnp.where(lsum > 0.0, lsum, 1.0)
    oc = jnp.dot(pc.reshape(rows, n_cp).astype(BF16), vc_ref[0], preferred_element_type=F32)

    pcg = pc.reshape(2, 4, TQ_N, n_cp)
    pcg = (pcg[:, 0] + pcg[:, 1]) + (pcg[:, 2] + pcg[:, 3])
    imp = jnp.dot(pcg.reshape(2 * TQ_N, n_cp), ov_ref[...], precision=HI, preferred_element_type=F32)
    n_sp = imp.shape[1]
    imp = imp.reshape(2, TQ_N, n_sp)
    sid = lax.broadcasted_iota(jnp.int32, (1, 1, n_sp), 2)
    cur = pos // SEL_BLOCK
    forced = (sid == 0) | (sid == cur) | (sid == cur - 1)
    valid = (sid * SEL_BLOCK <= pos) & (sid < n_sel)
    score = jnp.where(forced, FORCE, imp)
    sid_full = lax.broadcasted_iota(jnp.int32, (2, TQ_N, n_sp), 2)
    sel, _ = _topk_mask(score, valid, min(SEL_TOPN, n_sel), sid_full, 2)
    selneg = jnp.where(sel > 0.5, 0.0, NEG).astype(BF16)
    selneg8 = jnp.broadcast_to(selneg[:, None], (2, 4, TQ_N, n_sp)).reshape(rows, n_sp)

    jmax = q0 // TK_N
    e_row = lax.broadcasted_iota(jnp.int32, (n_sp, TK_N), 0)
    e_col = lax.broadcasted_iota(jnp.int32, (n_sp, TK_N), 1) // SEL_BLOCK

    def bias(j):
        return tb_ref[jnp.minimum(qi - 2 * j, n_tb - 1)].reshape(rows, TK_N)

    def kslice(ref, j):
        return ref[0, pl.ds(pl.multiple_of(j * TK_N, TK_N), TK_N), :]

    def sel_logits(j):
        e_j = (e_row == e_col + j * (TK_N // SEL_BLOCK)).astype(BF16)
        s = lax.dot_general(q8, kslice(ks_ref, j), NT, preferred_element_type=F32)
        return s + jnp.dot(selneg8, e_j, preferred_element_type=F32) + bias(j)

    st = _softmax_update(sel_logits(jmax), None, kslice(vs_ref, jmax), True)
    st = lax.fori_loop(0, jmax, lambda j, st: _softmax_update(sel_logits(j), st, kslice(vs_ref, j), False), st)
    osel = st[2] / st[1]

    def win_logits(j):
        s = lax.dot_general(q8, kslice(kw_ref, j), NT, preferred_element_type=F32) + bias(j)
        wmask = wm_ref[jnp.minimum(qi - 2 * j, wm_ref.shape[0] - 1)]
        return (s.reshape(NSA_HEADS, TQ_N, TK_N) + wmask[None]).reshape(rows, TK_N)

    st = _softmax_update(win_logits(jmax), None, kslice(vw_ref, jmax), True)
    jlo = jnp.maximum(jmax - (WIN_TILES - 1), 0)
    st = lax.fori_loop(jlo, jmax, lambda j, st: _softmax_update(win_logits(j), st, kslice(vw_ref, j), False), st)
    ow = st[2] / st[1]

    gt = gt_ref[...]
    glane = lax.broadcasted_iota(jnp.int32, gt.shape, 1)

    def gate(hd, br):
        return jnp.sum(jnp.where(glane == hd * 3 + br, gt, 0.0), axis=1, keepdims=True)

    for hd in range(NSA_HEADS):
        r0 = hd * TQ_N
        o = (gate(hd, 0) * oc[r0:r0 + TQ_N] + gate(hd, 1) * osel[r0:r0 + TQ_N] + gate(hd, 2) * ow[r0:r0 + TQ_N])
        o_ref[hd] = o.astype(o_ref.dtype)


def _nsa_prompt(qn8, gates, kcmp, vcmp, ksb, vsb, kwb, vwb, tbn, wm, n_cmp):
    b, t, _ = ksb.shape
    n_cp = kcmp.shape[1]
    n_sel = t // SEL_BLOCK
    n_sp = -(-n_sel // LANES) * LANES
    ov = jnp.asarray(_overlap_matrix(n_cp, n_sp))
    n_tb = tbn.shape[0]
    nq = t // TQ_N
    full = lambda a: pl.BlockSpec((1,) + a.shape[1:], lambda bi, qi: (bi,) + (0,) * (a.ndim - 1))
    once = lambda a: pl.BlockSpec(a.shape, lambda bi, qi: (0,) * a.ndim, pipeline_mode=pl.Buffered(1))
    return pl.pallas_call(
        functools.partial(_nsa_kernel, n_cmp=n_cmp, n_sel=n_sel, n_tb=n_tb),
        grid=(b, nq),
        in_specs=[pl.BlockSpec((NSA_HEADS, TQ_N, LANES), lambda bi, qi: (0, bi * nq + qi, 0)),
                  pl.BlockSpec((TQ_N, LANES), lambda bi, qi: (bi * nq + qi, 0)),
                  full(kcmp), full(vcmp), full(ksb), full(vsb), full(kwb), full(vwb),
                  once(ov), once(tbn), once(wm)],
        out_specs=pl.BlockSpec((NSA_HEADS, TQ_N, LANES), lambda bi, qi: (0, bi * nq + qi, 0)),
        out_shape=jax.ShapeDtypeStruct((NSA_HEADS, b * t, LANES), BF16),
        compiler_params=_cparams(("parallel", "parallel")), name="nsa_prompt",
    )(qn8, gates, kcmp, vcmp, ksb, vsb, kwb, vwb, ov, tbn, wm)


def _mix_kernel(x_ref, om_ref, on_ref, wom_ref, won_ref, g_ref, wr_ref, br_ref,
                r_ref, h_ref, ti_ref, tg_ref):
    acc = x_ref[...] + jnp.dot(om_ref[...], wom_ref[...], preferred_element_type=F32)
    for hd in range(NSA_HEADS):
        acc = acc + jnp.dot(on_ref[hd], won_ref[hd], preferred_element_type=F32)
    r_ref[...] = acc
    h = acc * lax.rsqrt(jnp.mean(acc * acc, axis=-1, keepdims=True) + RMS_EPS) * g_ref[...]
    h_ref[...] = h
    logits = jnp.dot(h, wr_ref[...], precision=HI, preferred_element_type=F32) + br_ref[...]
    e_id = lax.broadcasted_iota(jnp.int32, logits.shape, 1)
    work = logits
    vals, ids = [], []
    for _ in range(TOP_K):
        mx = jnp.max(work, axis=1, keepdims=True)
        idx = jnp.min(jnp.where(work == mx, e_id, N_EXPERTS), axis=1, keepdims=True)
        work = jnp.where(e_id == idx, -3e38, work)
        vals.append(mx)
        ids.append(idx)
    ex = [jnp.exp(v - vals[0]) for v in vals]
    tot = ex[0]
    for e in ex[1:]:
        tot = tot + e
    k_id = lax.broadcasted_iota(jnp.int32, ti_ref.shape, 1)
    ti = jnp.zeros(ti_ref.shape, jnp.int32)
    tg = jnp.zeros(tg_ref.shape, F32)
    for k in range(TOP_K):
        ti = jnp.where(k_id == k, ids[k], ti)
        tg = jnp.where(k_id == k, ex[k] / tot, tg)
    ti_ref[...] = ti
    tg_ref[...] = tg


def _mix(x, om, on8, wom, won8, g_ffn, w_router, b_router, tm):
    t, d = x.shape
    row = lambda w_: pl.BlockSpec((tm, w_), lambda i: (i, 0))
    return pl.pallas_call(
        _mix_kernel, grid=(t // tm,),
        in_specs=[row(d), row(512), pl.BlockSpec((NSA_HEADS, tm, LANES), lambda i: (0, i, 0)),
                  _const_spec(wom.shape), _const_spec(won8.shape), _const_spec((1, d)),
                  _const_spec(w_router.shape), _const_spec((1, N_EXPERTS))],
        out_specs=[row(d), row(d), row(TOP_K), row(TOP_K)],
        out_shape=[jax.ShapeDtypeStruct((t, d), F32), jax.ShapeDtypeStruct((t, d), F32),
                   jax.ShapeDtypeStruct((t, TOP_K), jnp.int32), jax.ShapeDtypeStruct((t, TOP_K), F32)],
        compiler_params=_cparams(("parallel",)), name="mix_router",
    )(x, om, on8, wom, won8, g_ffn.reshape(1, d), w_router, b_router.reshape(1, N_EXPERTS))


TM_E = 256


def _route(top_i, tm):
    t = top_i.shape[0]
    n_assign = t * TOP_K
    flat_e = top_i.reshape(n_assign)
    order = jnp.argsort(flat_e)
    se = flat_e[order]
    counts = jnp.zeros((N_EXPERTS,), jnp.int32).at[flat_e].add(1)
    padded = (counts + tm - 1) // tm * tm
    pad_end = jnp.cumsum(padded)
    rank = jnp.arange(n_assign, dtype=jnp.int32) - (jnp.cumsum(counts) - counts)[se]
    dest_sorted = (pad_end - padded)[se] + rank
    n_blocks = -(-n_assign // tm) + N_EXPERTS
    row_tok = jnp.zeros((n_blocks * tm,), jnp.int32).at[dest_sorted].set((order // TOP_K).astype(jnp.int32))
    dest = jnp.zeros((n_assign,), jnp.int32).at[order].set(dest_sorted.astype(jnp.int32))
    blk_e = jnp.minimum(jnp.searchsorted(pad_end, jnp.arange(n_blocks) * tm, side='right'), N_EXPERTS - 1)
    n_used = (pad_end[-1] // tm).astype(jnp.int32).reshape(1)
    return row_tok, dest, blk_e.astype(jnp.int32), n_used, n_blocks


def _expert_kernel(blk_e_ref, n_used_ref, tok_cur_ref, tok_nxt_ref, h_hbm, wgu_ref, bgu_ref,
                   wd_ref, bd_ref, y_ref, xbuf, sem):
    i = pl.program_id(0)
    n_used = n_used_ref[0]
    tm = xbuf.shape[1]

    def start_gather(tok_ref, slot):
        for r in range(tm):
            pltpu.make_async_copy(h_hbm.at[pl.ds(tok_ref[r], 1), :], xbuf.at[slot, pl.ds(r, 1), :], sem.at[slot]).start()

    def wait_gather(slot):
        pltpu.make_async_copy(h_hbm.at[pl.ds(0, tm), :], xbuf.at[slot], sem.at[slot]).wait()

    @pl.when(i == 0)
    def _():
        start_gather(tok_cur_ref, 0)

    @pl.when(i < n_used)
    def _():
        slot = i % 2
        wait_gather(slot)
        start_gather(tok_nxt_ref, 1 - slot)
        xb = xbuf[slot].astype(BF16)
        gu = jnp.dot(xb, wgu_ref[0], preferred_element_type=F32) + bgu_ref[0]
        glu = jnp.minimum(gu, SWIGLU_LIMIT)
        lin = jnp.clip(gu, -SWIGLU_LIMIT, SWIGLU_LIMIT) + 1.0
        lin_next = pltpu.roll(lin, lin.shape[1] - 1, 1)
        act = glu * jax.nn.sigmoid(SWIGLU_ALPHA * glu) * lin_next
        y_ref[...] = jnp.dot(act.astype(BF16), wd_ref[0], preferred_element_type=F32) + bd_ref[0]

        @pl.when(i == n_used - 1)
        def _():
            wait_gather(1 - slot)

    @pl.when(i >= n_used)
    def _():
        y_ref[...] = jnp.zeros_like(y_ref)


def _experts(h, row_tok, blk_e, n_used, n_blocks, wgu, bgu, wd, bd):
    t, d = h.shape
    dff2 = wgu.shape[2]
    tm = TM_E
    e_map = lambda i, be, nu: (be[i], 0, 0)
    grid_spec = pltpu.PrefetchScalarGridSpec(
        num_scalar_prefetch=2, grid=(n_blocks,),
        in_specs=[pl.BlockSpec((tm,), lambda i, be, nu: (i,), memory_space=pltpu.SMEM),
                  pl.BlockSpec((tm,), lambda i, be, nu: (jnp.minimum(i + 1, n_blocks - 1),), memory_space=pltpu.SMEM),
                  pl.BlockSpec(memory_space=pl.ANY),
                  pl.BlockSpec((1, d, dff2), e_map), pl.BlockSpec((1, 1, dff2), e_map),
                  pl.BlockSpec((1, dff2, d), e_map), pl.BlockSpec((1, 1, d), e_map)],
        out_specs=pl.BlockSpec((tm, d), lambda i, be, nu: (i, 0)),
        scratch_shapes=[pltpu.VMEM((2, tm, d), F32), pltpu.SemaphoreType.DMA((2,))])
    return pl.pallas_call(
        _expert_kernel, grid_spec=grid_spec,
        out_shape=jax.ShapeDtypeStruct((n_blocks * tm, d), F32),
        compiler_params=_cparams(("arbitrary",)), name="moe_experts",
    )(blk_e, n_used, row_tok, row_tok, h, wgu, bgu, wd, bd)


def _combine_kernel(d_cur_ref, d_nxt_ref, y_hbm, r_ref, tg_ref, p_ref, wpg_ref, wpp_ref, gf_ref, o_ref, ybuf, sem):
    i = pl.program_id(0)
    n = pl.num_programs(0)
    tm = ybuf.shape[2]

    def start_gather(d_ref, slot):
        for r in range(tm):
            for k in range(TOP_K):
                pltpu.make_async_copy(y_hbm.at[pl.ds(d_ref[r * TOP_K + k], 1), :],
                                      ybuf.at[slot, k, pl.ds(r, 1), :], sem.at[slot]).start()

    def wait_gather(slot):
        for k in range(TOP_K):
            pltpu.make_async_copy(y_hbm.at[pl.ds(0, tm), :], ybuf.at[slot, k], sem.at[slot]).wait()

    @pl.when(i == 0)
    def _():
        start_gather(d_cur_ref, 0)

    slot = i % 2
    wait_gather(slot)
    start_gather(d_nxt_ref, 1 - slot)
    tg = tg_ref[...]
    r = r_ref[...]
    for k in range(TOP_K):
        r = r + tg[:, k:k + 1] * ybuf[slot, k]
    gate = jax.nn.sigmoid(jnp.dot(r.astype(BF16), wpg_ref[...], preferred_element_type=F32))
    r = r + gate * jnp.dot(p_ref[...].astype(BF16), wpp_ref[...], preferred_element_type=F32)
    o_ref[...] = r * lax.rsqrt(jnp.mean(r * r, axis=-1, keepdims=True) + RMS_EPS) * gf_ref[...]

    @pl.when(i == n - 1)
    def _():
        wait_gather(1 - slot)


def _combine(y, dest, r1, tg, p, wpg, wpp, g_final, tm):
    t, d = r1.shape
    nt = t // tm
    row = lambda w_: pl.BlockSpec((tm, w_), lambda i: (i, 0))
    return pl.pallas_call(
        _combine_kernel, grid=(nt,),
        in_specs=[pl.BlockSpec((tm * TOP_K,), lambda i: (i,), memory_space=pltpu.SMEM),
                  pl.BlockSpec((tm * TOP_K,), lambda i: (jnp.minimum(i + 1, nt - 1),), memory_space=pltpu.SMEM),
                  pl.BlockSpec(memory_space=pl.ANY),
                  row(d), row(TOP_K), row(p.shape[1]),
                  _const_spec(wpg.shape), _const_spec(wpp.shape), _const_spec((1, d))],
        out_specs=row(d),
        out_shape=jax.ShapeDtypeStruct((t, d), F32),
        scratch_shapes=[pltpu.VMEM((2, TOP_K, tm, d), F32), pltpu.SemaphoreType.DMA((2,))],
        compiler_params=_cparams(("arbitrary",)), name="moe_combine",
    )(dest, dest, y, r1, tg, p, wpg, wpp, g_final.reshape(1, d))


def _ffn(x, om, on8, p, wts, tm):
    r1, h2, ti, tg = _mix(x, om, on8, wts["wom"], wts["won8"], wts["g_ffn"], wts["w_router"], wts["b_router"], tm)
    row_tok, dest, blk_e, n_used, n_blocks = _route(ti, TM_E)
    y = _experts(h2, row_tok, blk_e, n_used, n_blocks, wts["wgu"], wts["bgu"], wts["wd"], wts["bd"])
    return _combine(y, dest, r1, tg, p, wts["wpg"], wts["wpp"], wts["g_final"], tm)


KMEAN_PAGES = 8
PAGES_PER_MOBA = MOBA_BLOCK // PAGE_SIZE
SEL_PER_PAGE = PAGE_SIZE // SEL_BLOCK


def _kmean_kernel(pt_ref, *refs):
    x_refs, o_ref = refs[:KMEAN_PAGES], refs[KMEAN_PAGES]
    i = pl.program_id(1)
    nb_step = KMEAN_PAGES // PAGES_PER_MOBA

    @pl.when(i == 0)
    def _():
        o_ref[...] = jnp.zeros_like(o_ref)

    blk_lane = lax.broadcasted_iota(jnp.int32, o_ref.shape, 2)
    acc = o_ref[...]
    for blk in range(nb_step):
        s = x_refs[blk * PAGES_PER_MOBA][...]
        for k in range(1, PAGES_PER_MOBA):
            s = s + x_refs[blk * PAGES_PER_MOBA + k][...]
        col = jnp.sum(s, axis=-1, keepdims=True) * (1.0 / MOBA_BLOCK)
        acc = jnp.where(blk_lane == i * nb_step + blk, col, acc)
    o_ref[...] = acc


def _moba_kmean(cache_t, page_table):
    b, n_pages = page_table.shape
    assert n_pages % KMEAN_PAGES == 0
    _, _, h, d, ps = cache_t.shape

    def page_spec(k):
        return pl.BlockSpec((None, None, h, d, ps), lambda bi, i, pt: (0, pt[bi, i * KMEAN_PAGES + k], 0, 0, 0))

    n_blk = n_pages // PAGES_PER_MOBA
    grid_spec = pltpu.PrefetchScalarGridSpec(
        num_scalar_prefetch=1, grid=(b, n_pages // KMEAN_PAGES),
        in_specs=[page_spec(k) for k in range(KMEAN_PAGES)],
        out_specs=pl.BlockSpec((None, h, d, n_blk), lambda bi, i, pt: (bi, 0, 0, 0)))
    return pl.pallas_call(
        _kmean_kernel, grid_spec=grid_spec,
        out_shape=jax.ShapeDtypeStruct((b, h, d, n_blk), F32),
        compiler_params=_cparams(("parallel", "arbitrary")), name="moba_kmean",
    )(page_table, *([cache_t] * KMEAN_PAGES))


def _dec_select_kernel(qm_ref, kmean_ref, qn_ref, kc_ref, vc_ref, mid_ref, oc_ref, pcg_ref, *, n_cmp):
    for hd in range(8):
        gate = jnp.sum(kmean_ref[hd] * qm_ref[hd], axis=0, keepdims=True)
        blk = lax.broadcasted_iota(jnp.int32, gate.shape, 1)
        _, ids = _topk_mask(gate, blk >= 0, MOBA_TOPK, blk, 1)
        for k in range(MOBA_TOPK):
            mid_ref[k:k + 1, hd:hd + 1] = ids[k]
    lc = lax.dot_general(qn_ref[...], kc_ref[...], NT, preferred_element_type=F32)
    ok = lax.broadcasted_iota(jnp.int32, lc.shape, 1) < n_cmp
    mc = jnp.max(jnp.where(ok, lc, NEG), axis=1, keepdims=True)
    pc = jnp.where(ok, jnp.exp(lc - mc), 0.0)
    lsum = jnp.sum(pc, axis=1, keepdims=True)
    pc = pc / jnp.where(lsum > 0.0, lsum, 1.0)
    oc_ref[...] = jnp.dot(pc.astype(BF16), vc_ref[...], preferred_element_type=F32)
    for g in range(2):
        pcg_ref[g:g + 1, :] = jnp.sum(pc[4 * g:4 * g + 4], axis=0, keepdims=True)


def _dec_select(qm_col, kmean_t, qn8, kcmp, vcmp, n_cmp):
    b, _, _, n_blk = kmean_t.shape
    n_cp = kcmp.shape[1]
    per_b = lambda *s: pl.BlockSpec((None,) + s, lambda bi: (bi,) + (0,) * len(s))
    return pl.pallas_call(
        functools.partial(_dec_select_kernel, n_cmp=n_cmp), grid=(b,),
        in_specs=[per_b(8, HEAD_DIM, 1), per_b(8, HEAD_DIM, n_blk), per_b(8, LANES), per_b(n_cp, LANES),
                  per_b(n_cp, LANES)],
        out_specs=[per_b(MOBA_TOPK, 8), per_b(8, LANES), per_b(2, n_cp)],
        out_shape=[jax.ShapeDtypeStruct((b, MOBA_TOPK, 8), jnp.int32), jax.ShapeDtypeStruct((b, 8, LANES), F32),
                   jax.ShapeDtypeStruct((b, 2, n_cp), F32)],
        compiler_params=_cparams(("parallel",)), name="dec_select",
    )(qm_col, kmean_t, qn8, kcmp, vcmp)


def _dec_topn_kernel(pcg_ref, ov_ref, o_ref, *, pos, n_sel):
    imp = jnp.dot(pcg_ref[...], ov_ref[...], precision=HI, preferred_element_type=F32)
    sid = lax.broadcasted_iota(jnp.int32, imp.shape, 1)
    cur = pos // SEL_BLOCK
    forced = (sid == 0) | (sid == cur) | (sid == cur - 1)
    valid = (sid * SEL_BLOCK <= pos) & (sid < n_sel)
    n_top = min(SEL_TOPN, n_sel)
    _, ids = _topk_mask(jnp.where(forced, FORCE, imp), valid, n_top, sid, 1)
    k_id = lax.broadcasted_iota(jnp.int32, o_ref.shape, 1)
    out = jnp.full(o_ref.shape, imp.shape[1], jnp.int32)
    for k in range(n_top):
        out = jnp.where(k_id == k, ids[k], out)
    o_ref[...] = out


def _dec_topn(pcg, pos, n_sel):
    rows, n_cp = pcg.shape
    n_sp = -(-n_sel // LANES) * LANES
    ov = jnp.asarray(_overlap_matrix(n_cp, n_sp))
    return pl.pallas_call(
        functools.partial(_dec_topn_kernel, pos=pos, n_sel=n_sel),
        out_shape=jax.ShapeDtypeStruct((rows, SEL_TOPN), jnp.int32), name="dec_topn",
        compiler_params=pltpu.CompilerParams(vmem_limit_bytes=VMEM_LIMIT),
    )(pcg, ov)


def _attend_t(s_self, v_self, logit_tiles, value_tiles_t):
    m = s_self
    for s in logit_tiles:
        m = jnp.maximum(m, jnp.max(s, axis=1, keepdims=True))
    p_self = jnp.exp(s_self - m)
    l = p_self
    acc = p_self * v_self
    for s, vt in zip(logit_tiles, value_tiles_t):
        p = jnp.exp(s - m)
        l = l + jnp.sum(p, axis=1, keepdims=True)
        acc = acc + lax.dot_general(p.astype(BF16), vt, NT, preferred_element_type=F32)
    return acc / l


def _dec_attn_kernel(pt_ref, mid_ref, sid_ref, qm_ref, kmn_ref, vmn_ref, qn_ref, new_ref, gt_ref, oc_ref,
                     wk_ref, wv_ref, pbm_ref, pbn_ref, wb_ref, b0_ref,
                     mk_hbm, mv_hbm, sk_hbm, sv_hbm, om_ref, on_ref,
                     kmbuf, vmbuf, ksbuf, vsbuf, sem, *, n_blk, n_selp):
    bi = pl.program_id(0)
    n_top = ksbuf.shape[1]

    def moba_copies(hd, k):
        blk = jnp.minimum(mid_ref[bi, k * 8 + hd], n_blk - 1)
        out = []
        for half in range(PAGES_PER_MOBA):
            page = pt_ref[bi, blk * PAGES_PER_MOBA + half]
            out.append(pltpu.make_async_copy(mk_hbm.at[0, page, hd], kmbuf.at[hd, k, half], sem.at[0]))
            out.append(pltpu.make_async_copy(mv_hbm.at[0, page, hd], vmbuf.at[hd, k, half], sem.at[0]))
        return out

    def sel_copies(g, k):
        s = jnp.minimum(sid_ref[bi, g * n_top + k], n_selp - 1)
        page = pt_ref[bi, s // SEL_PER_PAGE]
        return [pltpu.make_async_copy(sk_hbm.at[0, page, g], ksbuf.at[g, k], sem.at[1]),
                pltpu.make_async_copy(sv_hbm.at[0, page, g], vsbuf.at[g, k], sem.at[1])]

    copies = [c for hd in range(8) for k in range(MOBA_TOPK) for c in moba_copies(hd, k)]
    copies += [c for g in range(2) for k in range(n_top) for c in sel_copies(g, k)]
    for c in copies:
        c.start()
    for c in copies:
        c.wait()

    b0 = b0_ref[...]

    for hd in range(8):
        qf = qm_ref[hd:hd + 1, :] * SCALE
        qh = qf.astype(BF16)
        s_self = jnp.sum(qf * kmn_ref[hd:hd + 1, :], axis=1, keepdims=True) + b0[hd:hd + 1]
        ss, vv = [], []
        for k in range(MOBA_TOPK):
            raw = mid_ref[bi, k * 8 + hd]
            blk = jnp.minimum(raw, n_blk - 1)
            for half in range(PAGES_PER_MOBA):
                s = jnp.dot(qh, kmbuf[hd, k, half].astype(BF16), preferred_element_type=F32)
                s = s + pbm_ref[blk][hd:hd + 1, half * PAGE_SIZE:(half + 1) * PAGE_SIZE]
                ss.append(jnp.where(raw < n_blk, s, NEG))
                vv.append(vmbuf[hd, k, half].astype(BF16))
        om_ref[hd:hd + 1, :] = _attend_t(s_self, vmn_ref[hd:hd + 1, :], ss, vv)

    gt = gt_ref[...]
    for g in range(2):
        lanes = slice(g * HEAD_DIM, (g + 1) * HEAD_DIM)
        hs = slice(4 * g, 4 * g + 4)
        qh = qn_ref[hs, lanes]
        qf = qh.astype(F32)
        ks_new, vs_new = new_ref[0:1, lanes], new_ref[1:2, lanes]
        kw_new, vw_new = new_ref[2:3, lanes], new_ref[3:4, lanes]

        b0g = b0[8 + 4 * g:12 + 4 * g]
        row_half = lax.broadcasted_iota(jnp.int32, (4, PAGE_SIZE), 1) // SEL_BLOCK
        ss, vv = [], []
        for k in range(n_top):
            raw = sid_ref[bi, g * n_top + k]
            sblk = jnp.minimum(raw, n_selp - 1)
            s = jnp.dot(qh, ksbuf[g, k].astype(BF16), preferred_element_type=F32)
            s = s + pbn_ref[sblk // SEL_PER_PAGE][hs, :]
            ss.append(jnp.where((row_half == sblk % SEL_PER_PAGE) & (raw < n_selp), s, NEG))
            vv.append(vsbuf[g, k].astype(BF16))
        s_self = jnp.sum(qf * ks_new, axis=1, keepdims=True) + b0g
        osel = _attend_t(s_self, vs_new, ss, vv)

        sw = jnp.dot(qh, wk_ref[g].astype(BF16), preferred_element_type=F32) + wb_ref[hs, :]
        s_self = jnp.sum(qf * kw_new, axis=1, keepdims=True) + b0g
        ow = _attend_t(s_self, vw_new, [sw], [wv_ref[g].astype(BF16)])

        oc = oc_ref[hs, lanes]
        for r in range(4):
            hd = 4 * g + r
            o = (gt[:, hd * 3:hd * 3 + 1] * oc[r:r + 1] + gt[:, hd * 3 + 1:hd * 3 + 2] * osel[r:r + 1]
                 + gt[:, hd * 3 + 2:hd * 3 + 3] * ow[r:r + 1])
            on_ref[hd:hd + 1, :] = o


def _dec_attn(page_table, mids, sids, qm, kmn, vmn, qn8, new_rows, gates, oc, wk, wv, pbm, pbn, wb, b0,
              cache_mk, cache_mv, cache_sk, cache_sv):
    b = page_table.shape[0]
    n_blk = pbm.shape[0]
    n_selp = pbn.shape[0] * SEL_PER_PAGE
    n_top = sids.shape[1] // 2
    win = wk.shape[3]
    per_b = lambda *s: pl.BlockSpec((None,) + s, lambda bi, *_: (bi,) + (0,) * len(s))
    const = lambda a: pl.BlockSpec(a.shape, lambda bi, *_: (0,) * a.ndim)
    any_spec = pl.BlockSpec(memory_space=pl.ANY)
    grid_spec = pltpu.PrefetchScalarGridSpec(
        num_scalar_prefetch=3, grid=(b,),
        in_specs=[per_b(8, HEAD_DIM), per_b(8, HEAD_DIM), per_b(8, HEAD_DIM), per_b(8, LANES), per_b(4, LANES),
                  per_b(1, LANES), per_b(8, LANES), per_b(2, HEAD_DIM, win), per_b(2, HEAD_DIM, win),
                  const(pbm), const(pbn), const(wb), const(b0), any_spec, any_spec, any_spec, any_spec],
        out_specs=[per_b(8, HEAD_DIM), per_b(8, HEAD_DIM)],
        scratch_shapes=[pltpu.VMEM((8, MOBA_TOPK, PAGES_PER_MOBA, HEAD_DIM, PAGE_SIZE), F32),
                        pltpu.VMEM((8, MOBA_TOPK, PAGES_PER_MOBA, HEAD_DIM, PAGE_SIZE), F32),
                        pltpu.VMEM((2, n_top, HEAD_DIM, PAGE_SIZE), F32),
                        pltpu.VMEM((2, n_top, HEAD_DIM, PAGE_SIZE), F32),
                        pltpu.SemaphoreType.DMA((2,))])
    return pl.pallas_call(
        functools.partial(_dec_attn_kernel, n_blk=n_blk, n_selp=n_selp), grid_spec=grid_spec,
        out_shape=[jax.ShapeDtypeStruct((b, 8, HEAD_DIM), F32), jax.ShapeDtypeStruct((b, 8, HEAD_DIM), F32)],
        compiler_params=_cparams(("arbitrary",)), name="dec_attn",
    )(page_table, mids, sids, qm, kmn, vmn, qn8, new_rows, gates, oc, wk, wv, pbm, pbn, wb, b0,
      cache_mk, cache_mv, cache_sk, cache_sv)


def kernel(x_prompt, x_sample, p_prompt, p_sample, cache_moba_k, cache_moba_v, cache_nsa_cmp_k, cache_nsa_cmp_v, cache_nsa_sel_k, cache_nsa_sel_v, state_nsa_win_k, state_nsa_win_v, page_table, rel_bias, g_mix, w_in, w_out, cmp_w1_k, cmp_b1_k, cmp_w2_k, cmp_pos_k, cmp_w1_v, cmp_b1_v, cmp_w2_v, cmp_pos_v, g_ffn, w_router, b_router, w_gate_up, b_gate_up, w_down, b_down, w_ple_proj, w_ple_gate, g_final):
    assert rel_bias.shape == (N_BUCKETS, 16) and g_mix.shape[0] == 1, "one layer, 8 MoBA + 8 NSA heads"
    b, t, d = x_prompt.shape
    bd, dec_seq, _ = x_sample.shape
    assert dec_seq == 1 and t % (2 * MOBA_BLOCK) == 0
    n_pages = page_table.shape[1]
    past = n_pages * PAGE_SIZE
    win = state_nsa_win_k.shape[2]

    bv = rel_bias[_t5_bucket(jnp.arange(MAX_DISTANCE + 1))]
    n_tbm = -(-(MAX_DISTANCE + MOBA_BLOCK - 1) // TQ_M) + 1
    tbm = _toeplitz_tiles(bv[:, :8], TQ_M, MOBA_BLOCK, n_tbm, True)
    n_tbn = -(-(MAX_DISTANCE + TK_N - 1) // TQ_N) + 1
    tbn = _toeplitz_tiles(bv[:, 8:], TQ_N, TK_N, n_tbn, False)
    n_wm = -(-(WINDOW + TK_N) // TQ_N)
    dist = np.stack([TQ_N * m + np.arange(TQ_N)[:, None] - np.arange(TK_N)[None, :] for m in range(n_wm)])
    wm = jnp.asarray(np.where(dist <= WINDOW, 0.0, NEG).astype(np.float32))
    w_proj = _proj_weight(w_in[0])
    wo = w_out[0]
    won = wo[512:].reshape(NSA_HEADS, HEAD_DIM, d)
    zn = jnp.zeros_like(won)
    grp0 = (jnp.arange(NSA_HEADS) < 4)[:, None, None]
    wd_bf = w_down[0].astype(BF16)
    wts = dict(
        wom=wo[:512].astype(BF16),
        won8=jnp.concatenate([jnp.where(grp0, won, zn), jnp.where(grp0, zn, won)], axis=1).astype(BF16),
        g_ffn=g_ffn[0], w_router=w_router[0], b_router=b_router[0],
        wgu=w_gate_up[0].astype(BF16), bgu=b_gate_up[0][:, None, :],
        wd=jnp.stack([wd_bf, jnp.zeros_like(wd_bf)], axis=2).reshape(N_EXPERTS, -1, d), bd=b_down[0][:, None, :],
        wpg=w_ple_gate[0].astype(BF16), wpp=w_ple_proj[0].astype(BF16), g_final=g_final)
    ck = _cmp_const(cmp_pos_k[0], cmp_w1_k[0], cmp_b1_k[0])
    cv = _cmp_const(cmp_pos_v[0], cmp_w1_v[0], cmp_b1_v[0])

    def pad_heads(o):
        z = jnp.zeros_like(o)
        return jnp.concatenate([jnp.where(grp0, o, z), jnp.where(grp0, z, o)], axis=-1).astype(BF16)

    xp = x_prompt.reshape(b * t, d)
    qm, km, vm, kmb, vmb, qn8, kv, kvb, gt, kmean = _project(xp, g_mix, w_proj, 2 * MOBA_BLOCK, True)
    om = _moba_prompt(qm.reshape(b, t, 512), kmb.reshape(b, t, 512), vmb.reshape(b, t, 512),
                      kmean.reshape(b, t // MOBA_BLOCK, 512), tbm)
    pt_p = jnp.arange(b * t // PAGE_SIZE, dtype=jnp.int32).reshape(b, t // PAGE_SIZE)
    kcmp = _compress(_chunk_layout(kv[0].reshape(-1, PAGE_SIZE, 2, HEAD_DIM)), pt_p, cmp_w1_k[0], ck, cmp_w2_k[0])
    vcmp = _compress(_chunk_layout(kv[1].reshape(-1, PAGE_SIZE, 2, HEAD_DIM)), pt_p, cmp_w1_v[0], cv, cmp_w2_v[0])
    seq = lambda a: a.reshape(b, t, LANES)
    on8 = _nsa_prompt(qn8, gt, kcmp, vcmp, seq(kvb[2]), seq(kvb[3]), seq(kvb[4]), seq(kvb[5]), tbn, wm,
                      (t - CMP_LEN) // CMP_STRIDE + 1)
    y_p = _ffn(xp, om.reshape(b * t, 512), on8, p_prompt[0].reshape(b * t, -1), wts, 256)

    xs = x_sample.reshape(bd, d)
    qm_s, km_s, vm_s, _, _, qn8_s, kv_s, _, gt_s = _project(xs, g_mix, w_proj, bd, False)
    rows_minor = lambda a: jnp.transpose(a, (0, 1, 3, 4, 2))
    kmean_s = _moba_kmean(rows_minor(cache_moba_k), page_table)
    kcmp_s = _compress(_chunk_layout(cache_nsa_cmp_k[0]), page_table, cmp_w1_k[0], ck, cmp_w2_k[0])
    vcmp_s = _compress(_chunk_layout(cache_nsa_cmp_v[0]), page_table, cmp_w1_v[0], cv, cmp_w2_v[0])
    heads = lambda a: a.reshape(bd, 8, HEAD_DIM)
    qn8_sb = jnp.transpose(qn8_s, (1, 0, 2))
    mids, oc_s, pcg = _dec_select(qm_s.reshape(bd, 8, HEAD_DIM, 1), kmean_s, qn8_sb, kcmp_s, vcmp_s,
                                  (past + 1 - CMP_LEN) // CMP_STRIDE + 1)
    n_sel_s = -(-(past + 1) // SEL_BLOCK)
    sids = _dec_topn(pcg.reshape(bd * 2, -1), past, n_sel_s)
    n_blk = past // MOBA_BLOCK
    kpos_m = np.arange(n_blk * MOBA_BLOCK).reshape(n_blk, MOBA_BLOCK)
    pbm = jnp.transpose(jnp.take(bv[:, :8], jnp.asarray(np.minimum(past - kpos_m, MAX_DISTANCE)), axis=0), (0, 2, 1))
    kpos_n = np.arange(past).reshape(n_pages, PAGE_SIZE)
    pbn = jnp.transpose(jnp.take(bv[:, 8:], jnp.asarray(np.minimum(past - kpos_n, MAX_DISTANCE)), axis=0), (0, 2, 1))
    wb = jnp.take(bv[:, 8:], jnp.asarray(np.minimum(win - np.arange(win), MAX_DISTANCE)), axis=0).T
    b0 = bv[0][:, None]
    new_rows = jnp.stack([kv_s[2], kv_s[3], kv_s[4], kv_s[5]], axis=1)
    om_s, on_s = _dec_attn(page_table, mids.reshape(bd, -1), sids.reshape(bd, -1), heads(qm_s), heads(km_s),
                           heads(vm_s), qn8_sb, new_rows, gt_s.reshape(bd, 1, LANES), oc_s,
                           rows_minor(state_nsa_win_k)[0], rows_minor(state_nsa_win_v)[0],
                           pbm, pbn, wb, b0, rows_minor(cache_moba_k), rows_minor(cache_moba_v),
                           rows_minor(cache_nsa_sel_k), rows_minor(cache_nsa_sel_v))
    y_s = _ffn(xs, om_s.reshape(bd, 512).astype(BF16), pad_heads(jnp.transpose(on_s, (1, 0, 2))),
               p_sample[0].reshape(bd, -1), wts, bd)

    wp = min(WINDOW, t)
    mh = lambda a, n, h: a.reshape(1, n, -1, h, HEAD_DIM)
    kw_new = kv_s[4].reshape(bd, 1, 2, HEAD_DIM)
    vw_new = kv_s[5].reshape(bd, 1, 2, HEAD_DIM)
    win_k = jnp.concatenate([state_nsa_win_k[0], kw_new], axis=1)[:, -win:][None]
    win_v = jnp.concatenate([state_nsa_win_v[0], vw_new], axis=1)[:, -win:][None]
    return (y_p.reshape(b, t, d), y_s.reshape(bd, 1, d),
            mh(km, b, 8), mh(vm, b, 8), mh(kv[0], b, 2), mh(kv[1], b, 2), mh(kv[2], b, 2), mh(kv[3], b, 2),
            mh(kv[4], b, 2)[:, :, -wp:], mh(kv[5], b, 2)[:, :, -wp:],
            mh(km_s, bd, 8), mh(vm_s, bd, 8), mh(kv_s[0], bd, 2), mh(kv_s[1], bd, 2), mh(kv_s[2], bd, 2),
            mh(kv_s[3], bd, 2), win_k, win_v)
```

```python
import functools
import math

import numpy as np
import jax
import jax.numpy as jnp
from jax import lax
from jax.experimental import pallas as pl
from jax.experimental.pallas import tpu as pltpu

PAGE_SIZE = 128
HEAD_DIM = 64
MOBA_BLOCK = 256
MOBA_TOPK = 3
CMP_LEN = 32
CMP_STRIDE = 16
CMP_HIDDEN = 256
SEL_BLOCK = 64
SEL_TOPN = 16
WINDOW = 512
N_BUCKETS = 32
MAX_DISTANCE = 1024
N_EXPERTS = 32
TOP_K = 4
SWIGLU_LIMIT = 7.0
SWIGLU_ALPHA = 1.702
RMS_EPS = 1e-6
NEG = -1e30
FORCE = 1e30
SCALE = HEAD_DIM ** -0.5
LANES = 128
VMEM_LIMIT = 56 * 1024 * 1024

F32 = jnp.float32
BF16 = jnp.bfloat16
HI = lax.Precision.HIGHEST
NT = (((1,), (1,)), ((), ()))


def _cparams(sem):
    return pltpu.CompilerParams(dimension_semantics=sem, vmem_limit_bytes=VMEM_LIMIT)


def _const_spec(shape):
    n = len(shape)
    return pl.BlockSpec(shape, lambda *_: (0,) * n)


def _t5_bucket(dist):
    max_exact = N_BUCKETS // 2
    d = jnp.maximum(dist, 0)
    df = jnp.maximum(d, max_exact).astype(jnp.float32)
    large = max_exact + (jnp.log(df / max_exact) / math.log(MAX_DISTANCE / max_exact)
                         * (N_BUCKETS - max_exact)).astype(jnp.int32)
    return jnp.where(d < max_exact, d, jnp.minimum(large, N_BUCKETS - 1))


def _topk_mask(score, valid, k, idx_iota, axis):
    n = float(score.shape[axis])
    low = -3e38
    pos_f = idx_iota.astype(F32)
    work = jnp.where(valid, score, low)
    sel = jnp.zeros(score.shape, F32)
    ids = []
    for _ in range(k):
        mx = jnp.max(work, axis=axis, keepdims=True)
        cand = (work == mx) & (mx > low)
        idx = jnp.min(jnp.where(cand, pos_f, n), axis=axis, keepdims=True)
        hit = pos_f == idx
        sel = jnp.where(hit, 1.0, sel)
        work = jnp.where(hit, low, work)
        ids.append(idx.astype(jnp.int32))
    return sel, ids


C_QM, C_KM, C_VM, C_QN, C_KV, C_GT, C_END = 0, 512, 1024, 1536, 2560, 3328, 3456


def _proj_weight(w_in):
    d = w_in.shape[0]
    qn = w_in[:, 1536:2048].reshape(d, 8, HEAD_DIM)
    z = jnp.zeros((d, 8, HEAD_DIM), w_in.dtype)
    grp0 = (jnp.arange(8) < 4)[None, :, None]
    qn_pad = jnp.concatenate([jnp.where(grp0, qn, z), jnp.where(grp0, z, qn)], axis=-1).reshape(d, 1024)
    gates = jnp.pad(w_in[:, 2816:2840], ((0, 0), (0, LANES - 24)))
    return jnp.concatenate([w_in[:, :1536], qn_pad, w_in[:, 2048:2816], gates], axis=1).astype(BF16)


def _proj_kernel(x_ref, g_ref, w_ref, qm_ref, km_ref, vm_ref, kmb_ref, vmb_ref, qn_ref,
                 kv_ref, kvb_ref, gt_ref, *rest, with_kmean):
    x = x_ref[...]
    h = x * lax.rsqrt(jnp.mean(x * x, axis=-1, keepdims=True) + RMS_EPS) * g_ref[...]
    z = jnp.dot(h.astype(BF16), w_ref[...], preferred_element_type=F32)
    qm_ref[...] = z[:, C_QM:C_KM]
    km = z[:, C_KM:C_VM]
    vm = z[:, C_VM:C_QN]
    km_ref[...] = km
    vm_ref[...] = vm
    kmb_ref[...] = km.astype(BF16)
    vmb_ref[...] = vm.astype(BF16)
    for hd in range(8):
        qn_ref[hd] = (z[:, C_QN + hd * LANES:C_QN + (hd + 1) * LANES] * SCALE).astype(BF16)
    for j in range(6):
        blk = z[:, C_KV + j * LANES:C_KV + (j + 1) * LANES]
        kv_ref[j] = blk
        kvb_ref[j] = blk.astype(BF16)
    gt_ref[...] = jax.nn.sigmoid(z[:, C_GT:C_END])
    if with_kmean:
        (kmean_ref,) = rest
        nblk = km.shape[0] // MOBA_BLOCK
        for i in range(nblk):
            kmean_ref[0, i:i + 1, :] = jnp.mean(km[i * MOBA_BLOCK:(i + 1) * MOBA_BLOCK], axis=0, keepdims=True)


def _project(x, g, w, tm, with_kmean):
    t, d = x.shape
    nt = t // tm
    row = lambda w_: pl.BlockSpec((tm, w_), lambda i: (i, 0))
    out_shape = [
        jax.ShapeDtypeStruct((t, 512), F32), jax.ShapeDtypeStruct((t, 512), F32), jax.ShapeDtypeStruct((t, 512), F32),
        jax.ShapeDtypeStruct((t, 512), BF16), jax.ShapeDtypeStruct((t, 512), BF16),
        jax.ShapeDtypeStruct((8, t, LANES), BF16),
        jax.ShapeDtypeStruct((6, t, LANES), F32), jax.ShapeDtypeStruct((6, t, LANES), BF16),
        jax.ShapeDtypeStruct((t, LANES), F32),
    ]
    out_specs = [row(512), row(512), row(512), row(512), row(512),
                 pl.BlockSpec((8, tm, LANES), lambda i: (0, i, 0)),
                 pl.BlockSpec((6, tm, LANES), lambda i: (0, i, 0)),
                 pl.BlockSpec((6, tm, LANES), lambda i: (0, i, 0)),
                 row(LANES)]
    if with_kmean:
        nb = tm // MOBA_BLOCK
        out_shape.append(jax.ShapeDtypeStruct((nt, nb, 512), F32))
        out_specs.append(pl.BlockSpec((1, nb, 512), lambda i: (i, 0, 0)))
    return pl.pallas_call(
        functools.partial(_proj_kernel, with_kmean=with_kmean),
        grid=(nt,),
        in_specs=[row(d), _const_spec((1, d)), _const_spec(w.shape)],
        out_specs=out_specs, out_shape=out_shape,
        compiler_params=_cparams(("parallel",)), name="proj",
    )(x, g, w)


def _toeplitz_kernel(v_ref, o_ref):
    rows, cols = o_ref.shape[1:]
    x = jnp.broadcast_to(v_ref[0], (rows, v_ref.shape[2]))
    o_ref[0] = pltpu.roll(x, 0, 1, stride=1, stride_axis=0)[:, :cols]


def _toeplitz_tiles(bv, rows, cols, n_tiles, head_major):
    length = rows + cols
    assert length % LANES == 0
    k = np.arange(length)
    off = np.where(k < cols, -k, length - k)
    dist = np.stack([rows * m + off for m in range(n_tiles)])
    v = jnp.take(bv, jnp.asarray(np.clip(dist, 0, MAX_DISTANCE)), axis=0)
    v = jnp.where(jnp.asarray(dist >= 0)[..., None], v, NEG)
    v = jnp.transpose(v, (2, 0, 1) if head_major else (0, 2, 1))
    lead = v.shape[:2]
    n = lead[0] * lead[1]
    tiles = pl.pallas_call(
        _toeplitz_kernel, grid=(n,),
        in_specs=[pl.BlockSpec((1, 1, length), lambda i: (i, 0, 0))],
        out_specs=pl.BlockSpec((1, rows, cols), lambda i: (i, 0, 0)),
        out_shape=jax.ShapeDtypeStruct((n, rows, cols), F32),
        compiler_params=_cparams(("parallel",)), name="bias_tiles",
    )(v.reshape(n, 1, length))
    return tiles.reshape(lead + (rows, cols))


TQ_M = MOBA_BLOCK


def _two_pass_attention(lo, hi, logits, values, rm_ref, l_ref, acc_ref):
    rm_ref[...] = jnp.full(rm_ref.shape, -3e38, F32)
    n_pairs = (hi - lo + 2) // 2

    def pass1(jj, c):
        j0 = lo + 2 * jj
        s0 = logits(j0)
        s1 = logits(jnp.minimum(j0 + 1, hi))
        m0 = jnp.maximum(s0[:, :LANES], s0[:, LANES:])
        m1 = jnp.maximum(s1[:, :LANES], s1[:, LANES:])
        rm_ref[...] = jnp.maximum(rm_ref[...], jnp.maximum(m0, m1))
        return c

    lax.fori_loop(0, n_pairs, pass1, 0)
    rm_ref[...] = jnp.broadcast_to(jnp.max(rm_ref[...], axis=1, keepdims=True), rm_ref.shape)
    l_ref[...] = jnp.zeros(l_ref.shape, F32)
    acc_ref[...] = jnp.zeros(acc_ref.shape, F32)

    def probs(j):
        s = logits(j)
        m = rm_ref[...]
        pa = jnp.exp(s[:, :LANES] - m)
        pb = jnp.exp(s[:, LANES:] - m)
        pv = jnp.dot(jnp.concatenate([pa.astype(BF16), pb.astype(BF16)], axis=1), values(j),
                     preferred_element_type=F32)
        return pa + pb, pv

    def pass2(jj, c):
        j0 = lo + 2 * jj
        l0, pv0 = probs(j0)
        l1, pv1 = probs(jnp.minimum(j0 + 1, hi))
        w1 = (j0 + 1 <= hi).astype(F32)
        l_ref[...] += l0 + w1 * l1
        acc_ref[...] += pv0 + w1 * pv1
        return c

    lax.fori_loop(0, n_pairs, pass2, 0)
    return acc_ref[...] / jnp.sum(l_ref[...], axis=1, keepdims=True)


def _moba_kernel(q_ref, k_ref, v_ref, kmean_ref, tb_ref, o_ref, rm_ref, l_ref, acc_ref, *, n_tb):
    qi = pl.program_id(2)
    q2 = q_ref[0]
    lane = lax.broadcasted_iota(jnp.int32, q2.shape, 1)
    q_st = jnp.concatenate([jnp.where(lane < HEAD_DIM, q2, 0.0), jnp.where(lane >= HEAD_DIM, q2, 0.0)], axis=0)
    rows = 2 * TQ_M
    blk_lane = lax.broadcasted_iota(jnp.int32, (rows, LANES), 1)
    gate = lax.dot_general(q_st, kmean_ref[0], NT, precision=HI, preferred_element_type=F32)
    sel, _ = _topk_mask(gate, blk_lane < qi, MOBA_TOPK, blk_lane, 1)
    selneg = jnp.where((sel > 0.5) | (blk_lane == qi), 0.0, NEG)
    q_aug = jnp.concatenate([(q_st * SCALE).astype(BF16), selneg.astype(BF16)], axis=1)
    key_lane = lax.broadcasted_iota(jnp.int32, (MOBA_BLOCK, LANES), 1)

    def kv_rows(ref, j):
        return ref[0, pl.ds(pl.multiple_of(j * MOBA_BLOCK, MOBA_BLOCK), MOBA_BLOCK), :]

    def logits(j):
        k_aug = jnp.concatenate([kv_rows(k_ref, j), (key_lane == j).astype(BF16)], axis=1)
        s = lax.dot_general(q_aug, k_aug, NT, preferred_element_type=F32)
        m_id = jnp.minimum(qi - j, n_tb - 1)
        return s + jnp.concatenate([tb_ref[0, m_id], tb_ref[1, m_id]], axis=0)

    o = _two_pass_attention(0, qi, logits, lambda j: kv_rows(v_ref, j), rm_ref, l_ref, acc_ref)
    o_ref[0] = jnp.where(lane < HEAD_DIM, o[:TQ_M], o[TQ_M:]).astype(o_ref.dtype)


def _moba_prompt(qm, kmb, vmb, kmean, tb):
    b, t, _ = qm.shape
    nb = kmean.shape[1]
    assert t // MOBA_BLOCK <= nb == LANES
    n_tb = tb.shape[1]
    return pl.pallas_call(
        functools.partial(_moba_kernel, n_tb=n_tb),
        grid=(4, b, t // TQ_M),
        in_specs=[pl.BlockSpec((1, TQ_M, LANES), lambda p, bi, qi: (bi, qi, p)),
                  pl.BlockSpec((1, t, LANES), lambda p, bi, qi: (bi, 0, p)),
                  pl.BlockSpec((1, t, LANES), lambda p, bi, qi: (bi, 0, p)),
                  pl.BlockSpec((1, nb, LANES), lambda p, bi, qi: (bi, 0, p)),
                  pl.BlockSpec((2, n_tb, TQ_M, MOBA_BLOCK), lambda p, bi, qi: (p, 0, 0, 0))],
        out_specs=pl.BlockSpec((1, TQ_M, LANES), lambda p, bi, qi: (bi, qi, p)),
        out_shape=jax.ShapeDtypeStruct((b, t, 512), BF16),
        scratch_shapes=[pltpu.VMEM((2 * TQ_M, LANES), F32)] * 3,
        compiler_params=_cparams(("parallel", "parallel", "parallel")), name="moba_prompt",
    )(qm, kmb, vmb, kmean, tb)


CHUNKS_PER_PAGE = PAGE_SIZE // CMP_STRIDE
CMP_PAGES = 16


def _chunk_layout(rows):
    n = rows.shape[0]
    x = rows.reshape(n, CHUNKS_PER_PAGE, CMP_STRIDE, 2, HEAD_DIM)
    return jnp.transpose(x, (0, 3, 1, 2, 4)).reshape(n, 2 * CHUNKS_PER_PAGE, CMP_STRIDE * HEAD_DIM).astype(BF16)


def _cmp_const_kernel(pe_ref, w1_ref, b1_ref, o_ref):
    o_ref[...] = jnp.sum(w1_ref[...] * pe_ref[...], axis=0, keepdims=True) + b1_ref[...]


def _cmp_const(pe, w1, b1):
    return pl.pallas_call(
        _cmp_const_kernel, out_shape=jax.ShapeDtypeStruct((1, CMP_HIDDEN), F32), name="cmp_const",
    )(pe.reshape(CMP_LEN * HEAD_DIM, 1), w1, b1.reshape(1, CMP_HIDDEN))


def _gelu_tanh(x):
    return 0.5 * x * (1.0 + jnp.tanh(math.sqrt(2.0 / math.pi) * (x + 0.044715 * x * x * x)))


def _compress_kernel(pt_ref, *refs):
    x_refs = refs[:CMP_PAGES + 1]
    wab_ref, c_ref, w2_ref, o_ref = refs[CMP_PAGES + 1:]
    rows_pp = 2 * CHUNKS_PER_PAGE
    x = jnp.concatenate([r[0] for r in x_refs], axis=0)
    ab = jnp.dot(x, wab_ref[...], preferred_element_type=F32)
    a = ab[:CMP_PAGES * rows_pp, :CMP_HIDDEN]
    bm = ab[:, CMP_HIDDEN:]
    n = bm.shape[0]
    nxt = pltpu.roll(bm, n - 1, 0)
    nxt_page = pltpu.roll(bm, n - (rows_pp - CHUNKS_PER_PAGE + 1), 0)
    c_id = lax.broadcasted_iota(jnp.int32, bm.shape, 0) % CHUNKS_PER_PAGE
    b_next = jnp.where(c_id == CHUNKS_PER_PAGE - 1, nxt_page, nxt)[:CMP_PAGES * rows_pp]
    hid = _gelu_tanh(a + b_next + c_ref[...]).astype(BF16)
    hid = hid.reshape(CMP_PAGES, 2, CHUNKS_PER_PAGE, CMP_HIDDEN)
    out = None
    for g in range(2):
        hg = hid[:, g].reshape(CMP_PAGES * CHUNKS_PER_PAGE, CMP_HIDDEN)
        og = jnp.dot(hg, w2_ref[g], preferred_element_type=F32)
        out = og if out is None else out + og
    o_ref[0] = out.astype(o_ref.dtype)


def _compress(xc, page_table, w1, const, w2):
    b, n_pages = page_table.shape
    assert n_pages % CMP_PAGES == 0
    half = CMP_STRIDE * HEAD_DIM
    wab = jnp.concatenate([w1[:half], w1[half:]], axis=1).astype(BF16)
    z = jnp.zeros_like(w2)
    w2p = jnp.stack([jnp.concatenate([w2, z], axis=1), jnp.concatenate([z, w2], axis=1)]).astype(BF16)

    def page_spec(k):
        def imap(bi, i, pt):
            return (pt[bi, jnp.minimum(i * CMP_PAGES + k, n_pages - 1)], 0, 0)
        return pl.BlockSpec((1, 2 * CHUNKS_PER_PAGE, half), imap)

    tile = CMP_PAGES * CHUNKS_PER_PAGE
    grid_spec = pltpu.PrefetchScalarGridSpec(
        num_scalar_prefetch=1, grid=(b, n_pages // CMP_PAGES),
        in_specs=[page_spec(k) for k in range(CMP_PAGES + 1)] + [
            pl.BlockSpec(wab.shape, lambda bi, i, pt: (0, 0)),
            pl.BlockSpec(const.shape, lambda bi, i, pt: (0, 0)),
            pl.BlockSpec(w2p.shape, lambda bi, i, pt: (0, 0, 0))],
        out_specs=pl.BlockSpec((1, tile, LANES), lambda bi, i, pt: (bi, i, 0)))
    return pl.pallas_call(
        _compress_kernel, grid_spec=grid_spec,
        out_shape=jax.ShapeDtypeStruct((b, n_pages * CHUNKS_PER_PAGE, LANES), BF16),
        compiler_params=_cparams(("parallel", "parallel")), name="compress",
    )(page_table, *([xc] * (CMP_PAGES + 1)), wab, const, w2p)


TQ_N = 128
TK_N = 256
NSA_HEADS = 8
assert WINDOW % TK_N == 0 and TK_N % TQ_N == 0 and TK_N == 2 * LANES == MOBA_BLOCK
WIN_TILES = WINDOW // TK_N + 1


def _overlap_matrix(n_cmp_pad, n_sel_pad):
    cstart = np.arange(n_cmp_pad)[:, None] * CMP_STRIDE
    sstart = np.arange(n_sel_pad)[None, :] * SEL_BLOCK
    return ((cstart < sstart + SEL_BLOCK) & (cstart + CMP_LEN > sstart)).astype(np.float32)


def _nsa_kernel(q_ref, gt_ref, kc_ref, vc_ref, ks_ref, vs_ref, kw_ref, vw_ref, ov_ref, tb_ref, wm_ref, o_ref,
                rm_ref, l_ref, acc_ref, *, n_cmp, n_sel, n_tb):
    qi = pl.program_id(1)
    q0 = qi * TQ_N
    rows = NSA_HEADS * TQ_N
    q8 = q_ref[...].reshape(rows, LANES)
    n_cp = kc_ref.shape[1]
    pos = q0 + lax.broadcasted_iota(jnp.int32, (1, TQ_N, 1), 1)

    lc = lax.dot_general(q8, kc_ref[0], NT, preferred_element_type=F32).reshape(NSA_HEADS, TQ_N, n_cp)
    n_id = lax.broadcasted_iota(jnp.int32, (1, 1, n_cp), 2)
    ok = (n_id * CMP_STRIDE + (CMP_LEN - 1) <= pos) & (n_id < n_cmp)
    mc = jnp.max(jnp.where(ok, lc, NEG), axis=2, keepdims=True)
    pc = jnp.where(ok, jnp.exp(lc - mc), 0.0)
    lsum = jnp.sum(pc, axis=2, keepdims=True)
    pc = pc / jnp.where(lsum > 0.0, lsum, 1.0)
    oc = jnp.dot(pc.reshape(rows, n_cp).astype(BF16), vc_ref[0], preferred_element_type=F32)

    pcg = pc.reshape(2, 4, TQ_N, n_cp)
    pcg = (pcg[:, 0] + pcg[:, 1]) + (pcg[:, 2] + pcg[:, 3])
    imp = jnp.dot(pcg.reshape(2 * TQ_N, n_cp), ov_ref[...], precision=HI, preferred_element_type=F32)
    n_sp = imp.shape[1]
    imp = imp.reshape(2, TQ_N, n_sp)
    sid = lax.broadcasted_iota(jnp.int32, (1, 1, n_sp), 2)
    cur = pos // SEL_BLOCK
    forced = (sid == 0) | (sid == cur) | (sid == cur - 1)
    valid = (sid * SEL_BLOCK <= pos) & (sid < n_sel)
    score = jnp.where(forced, FORCE, imp)
    sid_full = lax.broadcasted_iota(jnp.int32, (2, TQ_N, n_sp), 2)
    sel, _ = _topk_mask(score, valid, min(SEL_TOPN, n_sel), sid_full, 2)
    selneg = jnp.where(sel > 0.5, 0.0, NEG).astype(BF16)
    selneg8 = jnp.broadcast_to(selneg[:, None], (2, 4, TQ_N, n_sp)).reshape(rows, n_sp)
    q_aug = jnp.concatenate([q8, selneg8], axis=1)

    jmax = q0 // TK_N
    e_lane = lax.broadcasted_iota(jnp.int32, (TK_N, n_sp), 1)
    e_blk = lax.broadcasted_iota(jnp.int32, (TK_N, n_sp), 0) // SEL_BLOCK

    def bias(j):
        return tb_ref[jnp.minimum(qi - 2 * j, n_tb - 1)].reshape(rows, TK_N)

    def kslice(ref, j):
        return ref[0, pl.ds(pl.multiple_of(j * TK_N, TK_N), TK_N), :]

    def sel_logits(j):
        k_aug = jnp.concatenate([kslice(ks_ref, j), (e_lane == e_blk + j * (TK_N // SEL_BLOCK)).astype(BF16)], axis=1)
        return lax.dot_general(q_aug, k_aug, NT, preferred_element_type=F32) + bias(j)

    osel = _two_pass_attention(0, jmax, sel_logits, lambda j: kslice(vs_ref, j), rm_ref, l_ref, acc_ref)

    def win_logits(j):
        s = lax.dot_general(q8, kslice(kw_ref, j), NT, preferred_element_type=F32) + bias(j)
        wmask = wm_ref[jnp.minimum(qi - 2 * j, wm_ref.shape[0] - 1)]
        return (s.reshape(NSA_HEADS, TQ_N, TK_N) + wmask[None]).reshape(rows, TK_N)

    jlo = jnp.maximum(jmax - (WIN_TILES - 1), 0)
    ow = _two_pass_attention(jlo, jmax, win_logits, lambda j: kslice(vw_ref, j), rm_ref, l_ref, acc_ref)

    gt = gt_ref[...]
    glane = lax.broadcasted_iota(jnp.int32, gt.shape, 1)

    def gate(hd, br):
        return jnp.sum(jnp.where(glane == hd * 3 + br, gt, 0.0), axis=1, keepdims=True)

    for hd in range(NSA_HEADS):
        r0 = hd * TQ_N
        o = (gate(hd, 0) * oc[r0:r0 + TQ_N] + gate(hd, 1) * osel[r0:r0 + TQ_N] + gate(hd, 2) * ow[r0:r0 + TQ_N])
        o_ref[hd] = o.astype(o_ref.dtype)


def _nsa_prompt(qn8, gates, kcmp, vcmp, ksb, vsb, kwb, vwb, tbn, wm, n_cmp):
    b, t, _ = ksb.shape
    n_cp = kcmp.shape[1]
    n_sel = t // SEL_BLOCK
    n_sp = -(-n_sel // LANES) * LANES
    ov = jnp.asarray(_overlap_matrix(n_cp, n_sp))
    n_tb = tbn.shape[0]
    nq = t // TQ_N
    full = lambda a: pl.BlockSpec((1,) + a.shape[1:], lambda bi, qi: (bi,) + (0,) * (a.ndim - 1))
    once = lambda a: pl.BlockSpec(a.shape, lambda bi, qi: (0,) * a.ndim, pipeline_mode=pl.Buffered(1))
    return pl.pallas_call(
        functools.partial(_nsa_kernel, n_cmp=n_cmp, n_sel=n_sel, n_tb=n_tb),
        grid=(b, nq),
        in_specs=[pl.BlockSpec((NSA_HEADS, TQ_N, LANES), lambda bi, qi: (0, bi * nq + qi, 0)),
                  pl.BlockSpec((TQ_N, LANES), lambda bi, qi: (bi * nq + qi, 0)),
                  full(kcmp), full(vcmp), full(ksb), full(vsb), full(kwb), full(vwb),
                  once(ov), once(tbn), once(wm)],
        out_specs=pl.BlockSpec((NSA_HEADS, TQ_N, LANES), lambda bi, qi: (0, bi * nq + qi, 0)),
        out_shape=jax.ShapeDtypeStruct((NSA_HEADS, b * t, LANES), BF16),
        scratch_shapes=[pltpu.VMEM((NSA_HEADS * TQ_N, LANES), F32)] * 3,
        compiler_params=_cparams(("parallel", "parallel")), name="nsa_prompt",
    )(qn8, gates, kcmp, vcmp, ksb, vsb, kwb, vwb, ov, tbn, wm)


def _mix_kernel(x_ref, om_ref, on_ref, wom_ref, won_ref, g_ref, wr_ref, br_ref,
                r_ref, h_ref, ti_ref, tg_ref):
    acc = x_ref[...] + jnp.dot(om_ref[...], wom_ref[...], preferred_element_type=F32)
    for hd in range(NSA_HEADS):
        acc = acc + jnp.dot(on_ref[hd], won_ref[hd], preferred_element_type=F32)
    r_ref[...] = acc
    h = acc * lax.rsqrt(jnp.mean(acc * acc, axis=-1, keepdims=True) + RMS_EPS) * g_ref[...]
    h_ref[...] = h
    logits = jnp.dot(h, wr_ref[...], precision=HI, preferred_element_type=F32) + br_ref[...]
    e_id = lax.broadcasted_iota(jnp.int32, logits.shape, 1)
    work = logits
    vals, ids = [], []
    for _ in range(TOP_K):
        mx = jnp.max(work, axis=1, keepdims=True)
        idx = jnp.min(jnp.where(work == mx, e_id, N_EXPERTS), axis=1, keepdims=True)
        work = jnp.where(e_id == idx, -3e38, work)
        vals.append(mx)
        ids.append(idx)
    ex = [jnp.exp(v - vals[0]) for v in vals]
    tot = ex[0]
    for e in ex[1:]:
        tot = tot + e
    k_id = lax.broadcasted_iota(jnp.int32, ti_ref.shape, 1)
    ti = jnp.zeros(ti_ref.shape, jnp.int32)
    tg = jnp.zeros(tg_ref.shape, F32)
    for k in range(TOP_K):
        ti = jnp.where(k_id == k, ids[k], ti)
        tg = jnp.where(k_id == k, ex[k] / tot, tg)
    ti_ref[...] = ti
    tg_ref[...] = tg


def _mix(x, om, on8, wom, won8, g_ffn, w_router, b_router, tm):
    t, d = x.shape
    row = lambda w_: pl.BlockSpec((tm, w_), lambda i: (i, 0))
    return pl.pallas_call(
        _mix_kernel, grid=(t // tm,),
        in_specs=[row(d), row(512), pl.BlockSpec((NSA_HEADS, tm, LANES), lambda i: (0, i, 0)),
                  _const_spec(wom.shape), _const_spec(won8.shape), _const_spec((1, d)),
                  _const_spec(w_router.shape), _const_spec((1, N_EXPERTS))],
        out_specs=[row(d), row(d), row(TOP_K), row(TOP_K)],
        out_shape=[jax.ShapeDtypeStruct((t, d), F32), jax.ShapeDtypeStruct((t, d), F32),
                   jax.ShapeDtypeStruct((t, TOP_K), jnp.int32), jax.ShapeDtypeStruct((t, TOP_K), F32)],
        compiler_params=_cparams(("parallel",)), name="mix_router",
    )(x, om, on8, wom, won8, g_ffn.reshape(1, d), w_router, b_router.reshape(1, N_EXPERTS))


TM_E = 256


def _route(top_i, tm):
    t = top_i.shape[0]
    n_assign = t * TOP_K
    flat_e = top_i.reshape(n_assign)
    order = jnp.argsort(flat_e)
    se = flat_e[order]
    counts = jnp.zeros((N_EXPERTS,), jnp.int32).at[flat_e].add(1)
    padded = (counts + tm - 1) // tm * tm
    pad_end = jnp.cumsum(padded)
    rank = jnp.arange(n_assign, dtype=jnp.int32) - (jnp.cumsum(counts) - counts)[se]
    dest_sorted = (pad_end - padded)[se] + rank
    n_blocks = -(-n_assign // tm) + N_EXPERTS
    row_tok = jnp.zeros((n_blocks * tm,), jnp.int32).at[dest_sorted].set((order // TOP_K).astype(jnp.int32))
    dest = jnp.zeros((n_assign,), jnp.int32).at[order].set(dest_sorted.astype(jnp.int32))
    blk_e = jnp.minimum(jnp.searchsorted(pad_end, jnp.arange(n_blocks) * tm, side='right'), N_EXPERTS - 1)
    n_used = (pad_end[-1] // tm).astype(jnp.int32).reshape(1)
    return row_tok, dest, blk_e.astype(jnp.int32), n_used, n_blocks


def _expert_kernel(blk_e_ref, n_used_ref, tok_cur_ref, tok_nxt_ref, h_hbm, wgu_ref, bgu_ref,
                   wd_ref, bd_ref, y_ref, xbuf, sem):
    i = pl.program_id(0)
    n_used = n_used_ref[0]
    tm = xbuf.shape[1]

    def start_gather(tok_ref, slot):
        for r in range(tm):
            pltpu.make_async_copy(h_hbm.at[pl.ds(tok_ref[r], 1), :], xbuf.at[slot, pl.ds(r, 1), :], sem.at[slot]).start()

    def wait_gather(slot):
        pltpu.make_async_copy(h_hbm.at[pl.ds(0, tm), :], xbuf.at[slot], sem.at[slot]).wait()

    @pl.when(i == 0)
    def _():
        start_gather(tok_cur_ref, 0)

    def step(slot):
        wait_gather(slot)
        start_gather(tok_nxt_ref, 1 - slot)
        xb = xbuf[slot].astype(BF16)
        gu = jnp.dot(xb, wgu_ref[0], preferred_element_type=F32) + bgu_ref[0]
        glu = jnp.minimum(gu, SWIGLU_LIMIT)
        lin = jnp.clip(gu, -SWIGLU_LIMIT, SWIGLU_LIMIT) + 1.0
        lin_next = pltpu.roll(lin, lin.shape[1] - 1, 1)
        act = glu * jax.nn.sigmoid(SWIGLU_ALPHA * glu) * lin_next
        y_ref[...] = jnp.dot(act.astype(BF16), wd_ref[0], preferred_element_type=F32) + bd_ref[0]

        @pl.when(i == n_used - 1)
        def _():
            wait_gather(1 - slot)

    for parity in range(2):
        pl.when((i < n_used) & (i % 2 == parity))(functools.partial(step, parity))

    @pl.when(i >= n_used)
    def _():
        y_ref[...] = jnp.zeros_like(y_ref)


def _experts(h, row_tok, blk_e, n_used, n_blocks, wgu, bgu, wd, bd):
    t, d = h.shape
    dff2 = wgu.shape[2]
    tm = TM_E
    e_map = lambda i, be, nu: (be[i], 0, 0)
    grid_spec = pltpu.PrefetchScalarGridSpec(
        num_scalar_prefetch=2, grid=(n_blocks,),
        in_specs=[pl.BlockSpec((tm,), lambda i, be, nu: (i,), memory_space=pltpu.SMEM),
                  pl.BlockSpec((tm,), lambda i, be, nu: (jnp.minimum(i + 1, n_blocks - 1),), memory_space=pltpu.SMEM),
                  pl.BlockSpec(memory_space=pl.ANY),
                  pl.BlockSpec((1, d, dff2), e_map), pl.BlockSpec((1, 1, dff2), e_map),
                  pl.BlockSpec((1, dff2, d), e_map), pl.BlockSpec((1, 1, d), e_map)],
        out_specs=pl.BlockSpec((tm, d), lambda i, be, nu: (i, 0)),
        scratch_shapes=[pltpu.VMEM((2, tm, d), F32), pltpu.SemaphoreType.DMA((2,))])
    return pl.pallas_call(
        _expert_kernel, grid_spec=grid_spec,
        out_shape=jax.ShapeDtypeStruct((n_blocks * tm, d), F32),
        compiler_params=_cparams(("arbitrary",)), name="moe_experts",
    )(blk_e, n_used, row_tok, row_tok, h, wgu, bgu, wd, bd)


def _combine_kernel(d_cur_ref, d_nxt_ref, y_hbm, r_ref, tg_ref, p_ref, wpg_ref, wpp_ref, gf_ref, o_ref, ybuf, sem):
    i = pl.program_id(0)
    n = pl.num_programs(0)
    tm = ybuf.shape[2]

    def start_gather(d_ref, slot):
        for r in range(tm):
            for k in range(TOP_K):
                pltpu.make_async_copy(y_hbm.at[pl.ds(d_ref[r * TOP_K + k], 1), :],
                                      ybuf.at[slot, k, pl.ds(r, 1), :], sem.at[slot]).start()

    def wait_gather(slot):
        for k in range(TOP_K):
            pltpu.make_async_copy(y_hbm.at[pl.ds(0, tm), :], ybuf.at[slot, k], sem.at[slot]).wait()

    @pl.when(i == 0)
    def _():
        start_gather(d_cur_ref, 0)

    def step(slot):
        wait_gather(slot)
        start_gather(d_nxt_ref, 1 - slot)
        tg = tg_ref[...]
        r = r_ref[...]
        for k in range(TOP_K):
            r = r + tg[:, k:k + 1] * ybuf[slot, k]
        gate = jax.nn.sigmoid(jnp.dot(r.astype(BF16), wpg_ref[...], preferred_element_type=F32))
        r = r + gate * jnp.dot(p_ref[...].astype(BF16), wpp_ref[...], preferred_element_type=F32)
        o_ref[...] = r * lax.rsqrt(jnp.mean(r * r, axis=-1, keepdims=True) + RMS_EPS) * gf_ref[...]

        @pl.when(i == n - 1)
        def _():
            wait_gather(1 - slot)

    for parity in range(2):
        pl.when(i % 2 == parity)(functools.partial(step, parity))


def _combine(y, dest, r1, tg, p, wpg, wpp, g_final, tm):
    t, d = r1.shape
    nt = t // tm
    row = lambda w_: pl.BlockSpec((tm, w_), lambda i: (i, 0))
    return pl.pallas_call(
        _combine_kernel, grid=(nt,),
        in_specs=[pl.BlockSpec((tm * TOP_K,), lambda i: (i,), memory_space=pltpu.SMEM),
                  pl.BlockSpec((tm * TOP_K,), lambda i: (jnp.minimum(i + 1, nt - 1),), memory_space=pltpu.SMEM),
                  pl.BlockSpec(memory_space=pl.ANY),
                  row(d), row(TOP_K), row(p.shape[1]),
                  _const_spec(wpg.shape), _const_spec(wpp.shape), _const_spec((1, d))],
        out_specs=row(d),
        out_shape=jax.ShapeDtypeStruct((t, d), F32),
        scratch_shapes=[pltpu.VMEM((2, TOP_K, tm, d), F32), pltpu.SemaphoreType.DMA((2,))],
        compiler_params=_cparams(("arbitrary",)), name="moe_combine",
    )(dest, dest, y, r1, tg, p, wpg, wpp, g_final.reshape(1, d))


def _ffn(groups, wts):
    mixed = [_mix(x, om, on8, wts["wom"], wts["won8"], wts["g_ffn"], wts["w_router"], wts["b_router"], tm)
             for x, om, on8, _, tm in groups]
    h_all = jnp.concatenate([m[1] for m in mixed], axis=0)
    ti_all = jnp.concatenate([m[2] for m in mixed], axis=0)
    row_tok, dest, blk_e, n_used, n_blocks = _route(ti_all, TM_E)
    y = _experts(h_all, row_tok, blk_e, n_used, n_blocks, wts["wgu"], wts["bgu"], wts["wd"], wts["bd"])
    outs, start = [], 0
    for (x, _, _, p, tm), (r1, _, _, tg) in zip(groups, mixed):
        n = x.shape[0] * TOP_K
        outs.append(_combine(y, dest[start:start + n], r1, tg, p, wts["wpg"], wts["wpp"], wts["g_final"], tm))
        start += n
    return outs


KMEAN_PAGES = 8
PAGES_PER_MOBA = MOBA_BLOCK // PAGE_SIZE
SEL_PER_PAGE = PAGE_SIZE // SEL_BLOCK


def _kmean_kernel(pt_ref, *refs):
    x_refs, o_ref = refs[:KMEAN_PAGES], refs[KMEAN_PAGES]
    i = pl.program_id(1)
    nb_step = KMEAN_PAGES // PAGES_PER_MOBA

    @pl.when(i == 0)
    def _():
        o_ref[...] = jnp.zeros_like(o_ref)

    blk_lane = lax.broadcasted_iota(jnp.int32, o_ref.shape, 2)
    acc = o_ref[...]
    for blk in range(nb_step):
        s = x_refs[blk * PAGES_PER_MOBA][...]
        for k in range(1, PAGES_PER_MOBA):
            s = s + x_refs[blk * PAGES_PER_MOBA + k][...]
        col = jnp.sum(s, axis=-1, keepdims=True) * (1.0 / MOBA_BLOCK)
        acc = jnp.where(blk_lane == i * nb_step + blk, col, acc)
    o_ref[...] = acc


def _moba_kmean(cache_t, page_table):
    b, n_pages = page_table.shape
    assert n_pages % KMEAN_PAGES == 0
    _, _, h, d, ps = cache_t.shape

    def page_spec(k):
        return pl.BlockSpec((None, None, h, d, ps), lambda bi, i, pt: (0, pt[bi, i * KMEAN_PAGES + k], 0, 0, 0))

    n_blk = n_pages // PAGES_PER_MOBA
    grid_spec = pltpu.PrefetchScalarGridSpec(
        num_scalar_prefetch=1, grid=(b, n_pages // KMEAN_PAGES),
        in_specs=[page_spec(k) for k in range(KMEAN_PAGES)],
        out_specs=pl.BlockSpec((None, h, d, n_blk), lambda bi, i, pt: (bi, 0, 0, 0)))
    return pl.pallas_call(
        _kmean_kernel, grid_spec=grid_spec,
        out_shape=jax.ShapeDtypeStruct((b, h, d, n_blk), F32),
        compiler_params=_cparams(("parallel", "arbitrary")), name="moba_kmean",
    )(page_table, *([cache_t] * KMEAN_PAGES))


def _dec_select_kernel(qm_ref, kmean_ref, qn_ref, kc_ref, vc_ref, mid_ref, oc_ref, pcg_ref, *, n_cmp):
    for hd in range(8):
        gate = jnp.sum(kmean_ref[hd] * qm_ref[hd], axis=0, keepdims=True)
        blk = lax.broadcasted_iota(jnp.int32, gate.shape, 1)
        _, ids = _topk_mask(gate, blk >= 0, MOBA_TOPK, blk, 1)
        for k in range(MOBA_TOPK):
            mid_ref[k:k + 1, hd:hd + 1] = ids[k]
    lc = lax.dot_general(qn_ref[...], kc_ref[...], NT, preferred_element_type=F32)
    ok = lax.broadcasted_iota(jnp.int32, lc.shape, 1) < n_cmp
    mc = jnp.max(jnp.where(ok, lc, NEG), axis=1, keepdims=True)
    pc = jnp.where(ok, jnp.exp(lc - mc), 0.0)
    lsum = jnp.sum(pc, axis=1, keepdims=True)
    pc = pc / jnp.where(lsum > 0.0, lsum, 1.0)
    oc_ref[...] = jnp.dot(pc.astype(BF16), vc_ref[...], preferred_element_type=F32)
    for g in range(2):
        pcg_ref[g:g + 1, :] = jnp.sum(pc[4 * g:4 * g + 4], axis=0, keepdims=True)


def _dec_select(qm_col, kmean_t, qn8, kcmp, vcmp, n_cmp):
    b, _, _, n_blk = kmean_t.shape
    n_cp = kcmp.shape[1]
    per_b = lambda *s: pl.BlockSpec((None,) + s, lambda bi: (bi,) + (0,) * len(s))
    return pl.pallas_call(
        functools.partial(_dec_select_kernel, n_cmp=n_cmp), grid=(b,),
        in_specs=[per_b(8, HEAD_DIM, 1), per_b(8, HEAD_DIM, n_blk), per_b(8, LANES), per_b(n_cp, LANES),
                  per_b(n_cp, LANES)],
        out_specs=[per_b(MOBA_TOPK, 8), per_b(8, LANES), per_b(2, n_cp)],
        out_shape=[jax.ShapeDtypeStruct((b, MOBA_TOPK, 8), jnp.int32), jax.ShapeDtypeStruct((b, 8, LANES), F32),
                   jax.ShapeDtypeStruct((b, 2, n_cp), F32)],
        compiler_params=_cparams(("parallel",)), name="dec_select",
    )(qm_col, kmean_t, qn8, kcmp, vcmp)


def _dec_topn_kernel(pcg_ref, ov_ref, o_ref, *, pos, n_sel):
    imp = jnp.dot(pcg_ref[...], ov_ref[...], precision=HI, preferred_element_type=F32)
    sid = lax.broadcasted_iota(jnp.int32, imp.shape, 1)
    cur = pos // SEL_BLOCK
    forced = (sid == 0) | (sid == cur) | (sid == cur - 1)
    valid = (sid * SEL_BLOCK <= pos) & (sid < n_sel)
    n_top = min(SEL_TOPN, n_sel)
    _, ids = _topk_mask(jnp.where(forced, FORCE, imp), valid, n_top, sid, 1)
    k_id = lax.broadcasted_iota(jnp.int32, o_ref.shape, 1)
    out = jnp.full(o_ref.shape, imp.shape[1], jnp.int32)
    for k in range(n_top):
        out = jnp.where(k_id == k, ids[k], out)
    o_ref[...] = out


def _dec_topn(pcg, pos, n_sel):
    rows, n_cp = pcg.shape
    n_sp = -(-n_sel // LANES) * LANES
    ov = jnp.asarray(_overlap_matrix(n_cp, n_sp))
    return pl.pallas_call(
        functools.partial(_dec_topn_kernel, pos=pos, n_sel=n_sel),
        out_shape=jax.ShapeDtypeStruct((rows, SEL_TOPN), jnp.int32), name="dec_topn",
        compiler_params=pltpu.CompilerParams(vmem_limit_bytes=VMEM_LIMIT),
    )(pcg, ov)


def _attend_t(s_self, v_self, logit_tiles, value_tiles_t):
    m = s_self
    for s in logit_tiles:
        m = jnp.maximum(m, jnp.max(s, axis=1, keepdims=True))
    p_self = jnp.exp(s_self - m)
    l = p_self
    acc = p_self * v_self
    for s, vt in zip(logit_tiles, value_tiles_t):
        p = jnp.exp(s - m)
        l = l + jnp.sum(p, axis=1, keepdims=True)
        acc = acc + lax.dot_general(p.astype(BF16), vt, NT, preferred_element_type=F32)
    return acc / l


def _dec_attn_kernel(pt_ref, mid_ref, sid_ref, qm_ref, kmn_ref, vmn_ref, qn_ref, new_ref, gt_ref, oc_ref,
                     wk_ref, wv_ref, pbm_ref, pbn_ref, wb_ref, b0_ref,
                     mk_hbm, mv_hbm, sk_hbm, sv_hbm, om_ref, on_ref,
                     kmbuf, vmbuf, ksbuf, vsbuf, sem, *, n_blk, n_selp):
    bi = pl.program_id(0)
    n_top = ksbuf.shape[1]

    def moba_copies(hd, k):
        blk = jnp.minimum(mid_ref[bi, k * 8 + hd], n_blk - 1)
        out = []
        for half in range(PAGES_PER_MOBA):
            page = pt_ref[bi, blk * PAGES_PER_MOBA + half]
            out.append(pltpu.make_async_copy(mk_hbm.at[0, page, hd], kmbuf.at[hd, k, half], sem.at[0]))
            out.append(pltpu.make_async_copy(mv_hbm.at[0, page, hd], vmbuf.at[hd, k, half], sem.at[0]))
        return out

    def sel_copies(g, k):
        s = jnp.minimum(sid_ref[bi, g * n_top + k], n_selp - 1)
        page = pt_ref[bi, s // SEL_PER_PAGE]
        return [pltpu.make_async_copy(sk_hbm.at[0, page, g], ksbuf.at[g, k], sem.at[1]),
                pltpu.make_async_copy(sv_hbm.at[0, page, g], vsbuf.at[g, k], sem.at[1])]

    copies = [c for hd in range(8) for k in range(MOBA_TOPK) for c in moba_copies(hd, k)]
    copies += [c for g in range(2) for k in range(n_top) for c in sel_copies(g, k)]
    for c in copies:
        c.start()
    for c in copies:
        c.wait()

    b0 = b0_ref[...]

    for hd in range(8):
        qf = qm_ref[hd:hd + 1, :] * SCALE
        qh = qf.astype(BF16)
        s_self = jnp.sum(qf * kmn_ref[hd:hd + 1, :], axis=1, keepdims=True) + b0[hd:hd + 1]
        ss, vv = [], []
        for k in range(MOBA_TOPK):
            raw = mid_ref[bi, k * 8 + hd]
            blk = jnp.minimum(raw, n_blk - 1)
            for half in range(PAGES_PER_MOBA):
                s = jnp.dot(qh, kmbuf[hd, k, half].astype(BF16), preferred_element_type=F32)
                s = s + pbm_ref[blk][hd:hd + 1, half * PAGE_SIZE:(half + 1) * PAGE_SIZE]
                ss.append(jnp.where(raw < n_blk, s, NEG))
                vv.append(vmbuf[hd, k, half].astype(BF16))
        om_ref[hd:hd + 1, :] = _attend_t(s_self, vmn_ref[hd:hd + 1, :], ss, vv)

    gt = gt_ref[...]
    for g in range(2):
        lanes = slice(g * HEAD_DIM, (g + 1) * HEAD_DIM)
        hs = slice(4 * g, 4 * g + 4)
        qh = qn_ref[hs, lanes]
        qf = qh.astype(F32)
        ks_new, vs_new = new_ref[0:1, lanes], new_ref[1:2, lanes]
        kw_new, vw_new = new_ref[2:3, lanes], new_ref[3:4, lanes]

        b0g = b0[8 + 4 * g:12 + 4 * g]
        row_half = lax.broadcasted_iota(jnp.int32, (4, PAGE_SIZE), 1) // SEL_BLOCK
        ss, vv = [], []
        for k in range(n_top):
            raw = sid_ref[bi, g * n_top + k]
            sblk = jnp.minimum(raw, n_selp - 1)
            s = jnp.dot(qh, ksbuf[g, k].astype(BF16), preferred_element_type=F32)
            s = s + pbn_ref[sblk // SEL_PER_PAGE][hs, :]
            ss.append(jnp.where((row_half == sblk % SEL_PER_PAGE) & (raw < n_selp), s, NEG))
            vv.append(vsbuf[g, k].astype(BF16))
        s_self = jnp.sum(qf * ks_new, axis=1, keepdims=True) + b0g
        osel = _attend_t(s_self, vs_new, ss, vv)

        sw = jnp.dot(qh, wk_ref[g].astype(BF16), preferred_element_type=F32) + wb_ref[hs, :]
        s_self = jnp.sum(qf * kw_new, axis=1, keepdims=True) + b0g
        ow = _attend_t(s_self, vw_new, [sw], [wv_ref[g].astype(BF16)])

        oc = oc_ref[hs, lanes]
        for r in range(4):
            hd = 4 * g + r
            o = (gt[:, hd * 3:hd * 3 + 1] * oc[r:r + 1] + gt[:, hd * 3 + 1:hd * 3 + 2] * osel[r:r + 1]
                 + gt[:, hd * 3 + 2:hd * 3 + 3] * ow[r:r + 1])
            on_ref[hd:hd + 1, :] = o


def _dec_attn(page_table, mids, sids, qm, kmn, vmn, qn8, new_rows, gates, oc, wk, wv, pbm, pbn, wb, b0,
              cache_mk, cache_mv, cache_sk, cache_sv):
    b = page_table.shape[0]
    n_blk = pbm.shape[0]
    n_selp = pbn.shape[0] * SEL_PER_PAGE
    n_top = sids.shape[1] // 2
    win = wk.shape[3]
    per_b = lambda *s: pl.BlockSpec((None,) + s, lambda bi, *_: (bi,) + (0,) * len(s))
    const = lambda a: pl.BlockSpec(a.shape, lambda bi, *_: (0,) * a.ndim)
    any_spec = pl.BlockSpec(memory_space=pl.ANY)
    grid_spec = pltpu.PrefetchScalarGridSpec(
        num_scalar_prefetch=3, grid=(b,),
        in_specs=[per_b(8, HEAD_DIM), per_b(8, HEAD_DIM), per_b(8, HEAD_DIM), per_b(8, LANES), per_b(4, LANES),
                  per_b(1, LANES), per_b(8, LANES), per_b(2, HEAD_DIM, win), per_b(2, HEAD_DIM, win),
                  const(pbm), const(pbn), const(wb), const(b0), any_spec, any_spec, any_spec, any_spec],
        out_specs=[per_b(8, HEAD_DIM), per_b(8, HEAD_DIM)],
        scratch_shapes=[pltpu.VMEM((8, MOBA_TOPK, PAGES_PER_MOBA, HEAD_DIM, PAGE_SIZE), F32),
                        pltpu.VMEM((8, MOBA_TOPK, PAGES_PER_MOBA, HEAD_DIM, PAGE_SIZE), F32),
                        pltpu.VMEM((2, n_top, HEAD_DIM, PAGE_SIZE), F32),
                        pltpu.VMEM((2, n_top, HEAD_DIM, PAGE_SIZE), F32),
                        pltpu.SemaphoreType.DMA((2,))])
    return pl.pallas_call(
        functools.partial(_dec_attn_kernel, n_blk=n_blk, n_selp=n_selp), grid_spec=grid_spec,
        out_shape=[jax.ShapeDtypeStruct((b, 8, HEAD_DIM), F32), jax.ShapeDtypeStruct((b, 8, HEAD_DIM), F32)],
        compiler_params=_cparams(("arbitrary",)), name="dec_attn",
    )(page_table, mids, sids, qm, kmn, vmn, qn8, new_rows, gates, oc, wk, wv, pbm, pbn, wb, b0,
      cache_mk, cache_mv, cache_sk, cache_sv)


def kernel(x_prompt, x_sample, p_prompt, p_sample, cache_moba_k, cache_moba_v, cache_nsa_cmp_k, cache_nsa_cmp_v, cache_nsa_sel_k, cache_nsa_sel_v, state_nsa_win_k, state_nsa_win_v, page_table, rel_bias, g_mix, w_in, w_out, cmp_w1_k, cmp_b1_k, cmp_w2_k, cmp_pos_k, cmp_w1_v, cmp_b1_v, cmp_w2_v, cmp_pos_v, g_ffn, w_router, b_router, w_gate_up, b_gate_up, w_down, b_down, w_ple_proj, w_ple_gate, g_final):
    assert rel_bias.shape == (N_BUCKETS, 16) and g_mix.shape[0] == 1, "one layer, 8 MoBA + 8 NSA heads"
    b, t, d = x_prompt.shape
    bd, dec_seq, _ = x_sample.shape
    assert dec_seq == 1 and t % (2 * MOBA_BLOCK) == 0
    n_pages = page_table.shape[1]
    past = n_pages * PAGE_SIZE
    win = state_nsa_win_k.shape[2]

    bv = rel_bias[_t5_bucket(jnp.arange(MAX_DISTANCE + 1))]
    n_tbm = -(-(MAX_DISTANCE + MOBA_BLOCK - 1) // TQ_M) + 1
    tbm = _toeplitz_tiles(bv[:, :8], TQ_M, MOBA_BLOCK, n_tbm, True)
    n_tbn = -(-(MAX_DISTANCE + TK_N - 1) // TQ_N) + 1
    tbn = _toeplitz_tiles(bv[:, 8:], TQ_N, TK_N, n_tbn, False)
    n_wm = -(-(WINDOW + TK_N) // TQ_N)
    dist = np.stack([TQ_N * m + np.arange(TQ_N)[:, None] - np.arange(TK_N)[None, :] for m in range(n_wm)])
    wm = jnp.asarray(np.where(dist <= WINDOW, 0.0, NEG).astype(np.float32))
    w_proj = _proj_weight(w_in[0])
    wo = w_out[0]
    won = wo[512:].reshape(NSA_HEADS, HEAD_DIM, d)
    zn = jnp.zeros_like(won)
    grp0 = (jnp.arange(NSA_HEADS) < 4)[:, None, None]
    wd_bf = w_down[0].astype(BF16)
    wts = dict(
        wom=wo[:512].astype(BF16),
        won8=jnp.concatenate([jnp.where(grp0, won, zn), jnp.where(grp0, zn, won)], axis=1).astype(BF16),
        g_ffn=g_ffn[0], w_router=w_router[0], b_router=b_router[0],
        wgu=w_gate_up[0].astype(BF16), bgu=b_gate_up[0][:, None, :],
        wd=jnp.stack([wd_bf, jnp.zeros_like(wd_bf)], axis=2).reshape(N_EXPERTS, -1, d), bd=b_down[0][:, None, :],
        wpg=w_ple_gate[0].astype(BF16), wpp=w_ple_proj[0].astype(BF16), g_final=g_final)
    ck = _cmp_const(cmp_pos_k[0], cmp_w1_k[0], cmp_b1_k[0])
    cv = _cmp_const(cmp_pos_v[0], cmp_w1_v[0], cmp_b1_v[0])

    def pad_heads(o):
        z = jnp.zeros_like(o)
        return jnp.concatenate([jnp.where(grp0, o, z), jnp.where(grp0, z, o)], axis=-1).astype(BF16)

    xp = x_prompt.reshape(b * t, d)
    qm, km, vm, kmb, vmb, qn8, kv, kvb, gt, kmean = _project(xp, g_mix, w_proj, 2 * MOBA_BLOCK, True)
    kmean = jnp.pad(kmean.reshape(b, t // MOBA_BLOCK, 512), ((0, 0), (0, LANES - t // MOBA_BLOCK), (0, 0)))
    om = _moba_prompt(qm.reshape(b, t, 512), kmb.reshape(b, t, 512), vmb.reshape(b, t, 512), kmean, tbm)
    pt_p = jnp.arange(b * t // PAGE_SIZE, dtype=jnp.int32).reshape(b, t // PAGE_SIZE)
    kcmp = _compress(_chunk_layout(kv[0].reshape(-1, PAGE_SIZE, 2, HEAD_DIM)), pt_p, cmp_w1_k[0], ck, cmp_w2_k[0])
    vcmp = _compress(_chunk_layout(kv[1].reshape(-1, PAGE_SIZE, 2, HEAD_DIM)), pt_p, cmp_w1_v[0], cv, cmp_w2_v[0])
    seq = lambda a: a.reshape(b, t, LANES)
    on8 = _nsa_prompt(qn8, gt, kcmp, vcmp, seq(kvb[2]), seq(kvb[3]), seq(kvb[4]), seq(kvb[5]), tbn, wm,
                      (t - CMP_LEN) // CMP_STRIDE + 1)

    xs = x_sample.reshape(bd, d)
    qm_s, km_s, vm_s, _, _, qn8_s, kv_s, _, gt_s = _project(xs, g_mix, w_proj, bd, False)
    rows_minor = lambda a: jnp.transpose(a, (0, 1, 3, 4, 2))
    kmean_s = _moba_kmean(rows_minor(cache_moba_k), page_table)
    kcmp_s = _compress(_chunk_layout(cache_nsa_cmp_k[0]), page_table, cmp_w1_k[0], ck, cmp_w2_k[0])
    vcmp_s = _compress(_chunk_layout(cache_nsa_cmp_v[0]), page_table, cmp_w1_v[0], cv, cmp_w2_v[0])
    heads = lambda a: a.reshape(bd, 8, HEAD_DIM)
    qn8_sb = jnp.transpose(qn8_s, (1, 0, 2))
    mids, oc_s, pcg = _dec_select(qm_s.reshape(bd, 8, HEAD_DIM, 1), kmean_s, qn8_sb, kcmp_s, vcmp_s,
                                  (past + 1 - CMP_LEN) // CMP_STRIDE + 1)
    n_sel_s = -(-(past + 1) // SEL_BLOCK)
    sids = _dec_topn(pcg.reshape(bd * 2, -1), past, n_sel_s)
    n_blk = past // MOBA_BLOCK
    kpos_m = np.arange(n_blk * MOBA_BLOCK).reshape(n_blk, MOBA_BLOCK)
    pbm = jnp.transpose(jnp.take(bv[:, :8], jnp.asarray(np.minimum(past - kpos_m, MAX_DISTANCE)), axis=0), (0, 2, 1))
    kpos_n = np.arange(past).reshape(n_pages, PAGE_SIZE)
    pbn = jnp.transpose(jnp.take(bv[:, 8:], jnp.asarray(np.minimum(past - kpos_n, MAX_DISTANCE)), axis=0), (0, 2, 1))
    wb = jnp.take(bv[:, 8:], jnp.asarray(np.minimum(win - np.arange(win), MAX_DISTANCE)), axis=0).T
    b0 = bv[0][:, None]
    new_rows = jnp.stack([kv_s[2], kv_s[3], kv_s[4], kv_s[5]], axis=1)
    om_s, on_s = _dec_attn(page_table, mids.reshape(bd, -1), sids.reshape(bd, -1), heads(qm_s), heads(km_s),
                           heads(vm_s), qn8_sb, new_rows, gt_s.reshape(bd, 1, LANES), oc_s,
                           rows_minor(state_nsa_win_k)[0], rows_minor(state_nsa_win_v)[0],
                           pbm, pbn, wb, b0, rows_minor(cache_moba_k), rows_minor(cache_moba_v),
                           rows_minor(cache_nsa_sel_k), rows_minor(cache_nsa_sel_v))
    y_p, y_s = _ffn([(xp, om.reshape(b * t, 512), on8, p_prompt[0].reshape(b * t, -1), 256),
                     (xs, om_s.reshape(bd, 512).astype(BF16), pad_heads(jnp.transpose(on_s, (1, 0, 2))),
                      p_sample[0].reshape(bd, -1), bd)], wts)

    wp = min(WINDOW, t)
    mh = lambda a, n, h: a.reshape(1, n, -1, h, HEAD_DIM)
    kw_new = kv_s[4].reshape(bd, 1, 2, HEAD_DIM)
    vw_new = kv_s[5].reshape(bd, 1, 2, HEAD_DIM)
    win_k = jnp.concatenate([state_nsa_win_k[0], kw_new], axis=1)[:, -win:][None]
    win_v = jnp.concatenate([state_nsa_win_v[0], vw_new], axis=1)[:, -win:][None]
    return (y_p.reshape(b, t, d), y_s.reshape(bd, 1, d),
            mh(km, b, 8), mh(vm, b, 8), mh(kv[0], b, 2), mh(kv[1], b, 2), mh(kv[2], b, 2), mh(kv[3], b, 2),
            mh(kv[4], b, 2)[:, :, -wp:], mh(kv[5], b, 2)[:, :, -wp:],
            mh(km_s, bd, 8), mh(vm_s, bd, 8), mh(kv_s[0], bd, 2), mh(kv_s[1], bd, 2), mh(kv_s[2], bd, 2),
            mh(kv_s[3], bd, 2), win_k, win_v)
```

```python
import functools
import math

import numpy as np
import jax
import jax.numpy as jnp
from jax import lax
from jax.experimental import pallas as pl
from jax.experimental.pallas import tpu as pltpu

PAGE_SIZE = 128
HEAD_DIM = 64
MOBA_BLOCK = 256
MOBA_TOPK = 3
CMP_LEN = 32
CMP_STRIDE = 16
CMP_HIDDEN = 256
SEL_BLOCK = 64
SEL_TOPN = 16
WINDOW = 512
N_BUCKETS = 32
MAX_DISTANCE = 1024
N_EXPERTS = 32
TOP_K = 4
SWIGLU_LIMIT = 7.0
SWIGLU_ALPHA = 1.702
RMS_EPS = 1e-6
NEG = -1e30
FORCE = 1e30
SCALE = HEAD_DIM ** -0.5
LANES = 128
VMEM_LIMIT = 56 * 1024 * 1024

F32 = jnp.float32
BF16 = jnp.bfloat16
HI = lax.Precision.HIGHEST
NT = (((1,), (1,)), ((), ()))


def _cparams(sem):
    return pltpu.CompilerParams(dimension_semantics=sem, vmem_limit_bytes=VMEM_LIMIT)


def _const_spec(shape):
    n = len(shape)
    return pl.BlockSpec(shape, lambda *_: (0,) * n)


def _t5_bucket(dist):
    max_exact = N_BUCKETS // 2
    d = jnp.maximum(dist, 0)
    df = jnp.maximum(d, max_exact).astype(jnp.float32)
    large = max_exact + (jnp.log(df / max_exact) / math.log(MAX_DISTANCE / max_exact)
                         * (N_BUCKETS - max_exact)).astype(jnp.int32)
    return jnp.where(d < max_exact, d, jnp.minimum(large, N_BUCKETS - 1))


def _topk_mask(score, valid, k, idx_iota, axis):
    n = float(score.shape[axis])
    low = -3e38
    pos_f = idx_iota.astype(F32)
    work = jnp.where(valid, score, low)
    sel = jnp.zeros(score.shape, F32)
    ids = []
    for _ in range(k):
        mx = jnp.max(work, axis=axis, keepdims=True)
        cand = (work == mx) & (mx > low)
        idx = jnp.min(jnp.where(cand, pos_f, n), axis=axis, keepdims=True)
        hit = pos_f == idx
        sel = jnp.where(hit, 1.0, sel)
        work = jnp.where(hit, low, work)
        ids.append(idx.astype(jnp.int32))
    return sel, ids


C_QM, C_KM, C_VM, C_QN, C_KV, C_GT, C_END = 0, 512, 1024, 1536, 2560, 3328, 3456


def _proj_weight(w_in):
    d = w_in.shape[0]
    qn = w_in[:, 1536:2048].reshape(d, 8, HEAD_DIM)
    z = jnp.zeros((d, 8, HEAD_DIM), w_in.dtype)
    grp0 = (jnp.arange(8) < 4)[None, :, None]
    qn_pad = jnp.concatenate([jnp.where(grp0, qn, z), jnp.where(grp0, z, qn)], axis=-1).reshape(d, 1024)
    gates = jnp.pad(w_in[:, 2816:2840], ((0, 0), (0, LANES - 24)))
    return jnp.concatenate([w_in[:, :1536], qn_pad, w_in[:, 2048:2816], gates], axis=1).astype(BF16)


def _proj_kernel(x_ref, g_ref, w_ref, qm_ref, km_ref, vm_ref, kmb_ref, vmb_ref, qn_ref,
                 kv_ref, kvb_ref, gt_ref, *rest, with_kmean):
    x = x_ref[...]
    h = x * lax.rsqrt(jnp.mean(x * x, axis=-1, keepdims=True) + RMS_EPS) * g_ref[...]
    z = jnp.dot(h.astype(BF16), w_ref[...], preferred_element_type=F32)
    qm_ref[...] = z[:, C_QM:C_KM]
    km = z[:, C_KM:C_VM]
    vm = z[:, C_VM:C_QN]
    km_ref[...] = km
    vm_ref[...] = vm
    kmb_ref[...] = km.astype(BF16)
    vmb_ref[...] = vm.astype(BF16)
    for hd in range(8):
        qn_ref[hd] = (z[:, C_QN + hd * LANES:C_QN + (hd + 1) * LANES] * SCALE).astype(BF16)
    for j in range(6):
        blk = z[:, C_KV + j * LANES:C_KV + (j + 1) * LANES]
        kv_ref[j] = blk
        kvb_ref[j] = blk.astype(BF16)
    gt_ref[...] = jax.nn.sigmoid(z[:, C_GT:C_END])
    if with_kmean:
        (kmean_ref,) = rest
        nblk = km.shape[0] // MOBA_BLOCK
        for i in range(nblk):
            kmean_ref[0, i:i + 1, :] = jnp.mean(km[i * MOBA_BLOCK:(i + 1) * MOBA_BLOCK], axis=0, keepdims=True)


def _project(x, g, w, tm, with_kmean):
    t, d = x.shape
    nt = t // tm
    row = lambda w_: pl.BlockSpec((tm, w_), lambda i: (i, 0))
    out_shape = [
        jax.ShapeDtypeStruct((t, 512), F32), jax.ShapeDtypeStruct((t, 512), F32), jax.ShapeDtypeStruct((t, 512), F32),
        jax.ShapeDtypeStruct((t, 512), BF16), jax.ShapeDtypeStruct((t, 512), BF16),
        jax.ShapeDtypeStruct((8, t, LANES), BF16),
        jax.ShapeDtypeStruct((6, t, LANES), F32), jax.ShapeDtypeStruct((6, t, LANES), BF16),
        jax.ShapeDtypeStruct((t, LANES), F32),
    ]
    out_specs = [row(512), row(512), row(512), row(512), row(512),
                 pl.BlockSpec((8, tm, LANES), lambda i: (0, i, 0)),
                 pl.BlockSpec((6, tm, LANES), lambda i: (0, i, 0)),
                 pl.BlockSpec((6, tm, LANES), lambda i: (0, i, 0)),
                 row(LANES)]
    if with_kmean:
        nb = tm // MOBA_BLOCK
        out_shape.append(jax.ShapeDtypeStruct((nt, nb, 512), F32))
        out_specs.append(pl.BlockSpec((1, nb, 512), lambda i: (i, 0, 0)))
    return pl.pallas_call(
        functools.partial(_proj_kernel, with_kmean=with_kmean),
        grid=(nt,),
        in_specs=[row(d), _const_spec((1, d)), _const_spec(w.shape)],
        out_specs=out_specs, out_shape=out_shape,
        compiler_params=_cparams(("parallel",)), name="proj",
    )(x, g, w)


def _toeplitz_kernel(v_ref, o_ref):
    rows, cols = o_ref.shape[1:]
    x = jnp.broadcast_to(v_ref[0], (rows, v_ref.shape[2]))
    o_ref[0] = pltpu.roll(x, 0, 1, stride=1, stride_axis=0)[:, :cols]


def _toeplitz_tiles(bv, rows, cols, n_tiles, head_major):
    length = rows + cols
    assert length % LANES == 0
    k = np.arange(length)
    off = np.where(k < cols, -k, length - k)
    dist = np.stack([rows * m + off for m in range(n_tiles)])
    v = jnp.take(bv, jnp.asarray(np.clip(dist, 0, MAX_DISTANCE)), axis=0)
    v = jnp.where(jnp.asarray(dist >= 0)[..., None], v, NEG)
    v = jnp.transpose(v, (2, 0, 1) if head_major else (0, 2, 1))
    lead = v.shape[:2]
    n = lead[0] * lead[1]
    tiles = pl.pallas_call(
        _toeplitz_kernel, grid=(n,),
        in_specs=[pl.BlockSpec((1, 1, length), lambda i: (i, 0, 0))],
        out_specs=pl.BlockSpec((1, rows, cols), lambda i: (i, 0, 0)),
        out_shape=jax.ShapeDtypeStruct((n, rows, cols), F32),
        compiler_params=_cparams(("parallel",)), name="bias_tiles",
    )(v.reshape(n, 1, length))
    return tiles.reshape(lead + (rows, cols))


TQ_M = MOBA_BLOCK


def _two_pass_attention(lo, hi, logits, values, rm_ref, l_ref, acc_ref):
    rm_ref[...] = jnp.full(rm_ref.shape, -3e38, F32)
    n_pairs = (hi - lo + 2) // 2

    def pass1(jj, c):
        j0 = lo + 2 * jj
        s0 = logits(j0)
        s1 = logits(jnp.minimum(j0 + 1, hi))
        m0 = jnp.maximum(s0[:, :LANES], s0[:, LANES:])
        m1 = jnp.maximum(s1[:, :LANES], s1[:, LANES:])
        rm_ref[...] = jnp.maximum(rm_ref[...], jnp.maximum(m0, m1))
        return c

    lax.fori_loop(0, n_pairs, pass1, 0)
    rm_ref[...] = jnp.broadcast_to(jnp.max(rm_ref[...], axis=1, keepdims=True), rm_ref.shape)
    l_ref[...] = jnp.zeros(l_ref.shape, F32)
    acc_ref[...] = jnp.zeros(acc_ref.shape, F32)

    def probs(j):
        s = logits(j)
        m = rm_ref[...]
        pa = jnp.exp(s[:, :LANES] - m)
        pb = jnp.exp(s[:, LANES:] - m)
        pv = jnp.dot(jnp.concatenate([pa.astype(BF16), pb.astype(BF16)], axis=1), values(j),
                     preferred_element_type=F32)
        return pa + pb, pv

    def pass2(jj, c):
        j0 = lo + 2 * jj
        l0, pv0 = probs(j0)
        l1, pv1 = probs(jnp.minimum(j0 + 1, hi))
        w1 = (j0 + 1 <= hi).astype(F32)
        l_ref[...] += l0 + w1 * l1
        acc_ref[...] += pv0 + w1 * pv1
        return c

    lax.fori_loop(0, n_pairs, pass2, 0)
    return acc_ref[...] / jnp.sum(l_ref[...], axis=1, keepdims=True)


def _moba_kernel(q_ref, k_ref, v_ref, kmean_ref, tb_ref, o_ref, rm_ref, l_ref, acc_ref, *, n_tb):
    qi = pl.program_id(2)
    q2 = q_ref[0]
    lane = lax.broadcasted_iota(jnp.int32, q2.shape, 1)
    q_st = jnp.concatenate([jnp.where(lane < HEAD_DIM, q2, 0.0), jnp.where(lane >= HEAD_DIM, q2, 0.0)], axis=0)
    rows = 2 * TQ_M
    blk_lane = lax.broadcasted_iota(jnp.int32, (rows, LANES), 1)
    gate = lax.dot_general(q_st, kmean_ref[0], NT, precision=HI, preferred_element_type=F32)
    sel, _ = _topk_mask(gate, blk_lane < qi, MOBA_TOPK, blk_lane, 1)
    selneg = jnp.where((sel > 0.5) | (blk_lane == qi), 0.0, NEG)
    q_aug = jnp.concatenate([(q_st * SCALE).astype(BF16), selneg.astype(BF16)], axis=1)
    key_lane = lax.broadcasted_iota(jnp.int32, (MOBA_BLOCK, LANES), 1)

    def kv_rows(ref, j):
        return ref[0, pl.ds(pl.multiple_of(j * MOBA_BLOCK, MOBA_BLOCK), MOBA_BLOCK), :]

    def logits(j):
        k_aug = jnp.concatenate([kv_rows(k_ref, j), (key_lane == j).astype(BF16)], axis=1)
        s = lax.dot_general(q_aug, k_aug, NT, preferred_element_type=F32)
        m_id = jnp.minimum(qi - j, n_tb - 1)
        return s + jnp.concatenate([tb_ref[0, m_id], tb_ref[1, m_id]], axis=0)

    o = _two_pass_attention(0, qi, logits, lambda j: kv_rows(v_ref, j), rm_ref, l_ref, acc_ref)
    o_ref[0] = jnp.where(lane < HEAD_DIM, o[:TQ_M], o[TQ_M:]).astype(o_ref.dtype)


def _moba_prompt(qm, kmb, vmb, kmean, tb):
    b, t, _ = qm.shape
    nb = kmean.shape[1]
    assert t // MOBA_BLOCK <= nb == LANES
    n_tb = tb.shape[1]
    return pl.pallas_call(
        functools.partial(_moba_kernel, n_tb=n_tb),
        grid=(4, b, t // TQ_M),
        in_specs=[pl.BlockSpec((1, TQ_M, LANES), lambda p, bi, qi: (bi, qi, p)),
                  pl.BlockSpec((1, t, LANES), lambda p, bi, qi: (bi, 0, p)),
                  pl.BlockSpec((1, t, LANES), lambda p, bi, qi: (bi, 0, p)),
                  pl.BlockSpec((1, nb, LANES), lambda p, bi, qi: (bi, 0, p)),
                  pl.BlockSpec((2, n_tb, TQ_M, MOBA_BLOCK), lambda p, bi, qi: (p, 0, 0, 0))],
        out_specs=pl.BlockSpec((1, TQ_M, LANES), lambda p, bi, qi: (bi, qi, p)),
        out_shape=jax.ShapeDtypeStruct((b, t, 512), BF16),
        scratch_shapes=[pltpu.VMEM((2 * TQ_M, LANES), F32)] * 3,
        compiler_params=_cparams(("parallel", "parallel", "parallel")), name="moba_prompt",
    )(qm, kmb, vmb, kmean, tb)


CHUNKS_PER_PAGE = PAGE_SIZE // CMP_STRIDE
CMP_PAGES = 16


def _chunk_layout(pages_t):
    n = pages_t.shape[0]
    x = pages_t.reshape(n, 2, HEAD_DIM, CHUNKS_PER_PAGE, CMP_STRIDE)
    return jnp.transpose(x, (0, 1, 3, 2, 4)).reshape(n, 2 * CHUNKS_PER_PAGE, HEAD_DIM * CMP_STRIDE).astype(BF16)


def _chunk_weight(w_half):
    h = w_half.shape[1]
    return jnp.transpose(w_half.reshape(CMP_STRIDE, HEAD_DIM, h), (1, 0, 2)).reshape(CMP_STRIDE * HEAD_DIM, h)


def _cmp_const_kernel(pe_ref, w1_ref, b1_ref, o_ref):
    o_ref[...] = jnp.sum(w1_ref[...] * pe_ref[...], axis=0, keepdims=True) + b1_ref[...]


def _cmp_const(pe, w1, b1):
    return pl.pallas_call(
        _cmp_const_kernel, out_shape=jax.ShapeDtypeStruct((1, CMP_HIDDEN), F32), name="cmp_const",
    )(pe.reshape(CMP_LEN * HEAD_DIM, 1), w1, b1.reshape(1, CMP_HIDDEN))


def _gelu_tanh(x):
    return 0.5 * x * (1.0 + jnp.tanh(math.sqrt(2.0 / math.pi) * (x + 0.044715 * x * x * x)))


def _compress_kernel(pt_ref, *refs):
    x_refs = refs[:CMP_PAGES + 1]
    wab_ref, c_ref, w2_ref, o_ref = refs[CMP_PAGES + 1:]
    rows_pp = 2 * CHUNKS_PER_PAGE
    x = jnp.concatenate([r[0] for r in x_refs], axis=0)
    ab = jnp.dot(x, wab_ref[...], preferred_element_type=F32)
    a = ab[:CMP_PAGES * rows_pp, :CMP_HIDDEN]
    bm = ab[:, CMP_HIDDEN:]
    n = bm.shape[0]
    nxt = pltpu.roll(bm, n - 1, 0)
    nxt_page = pltpu.roll(bm, n - (rows_pp - CHUNKS_PER_PAGE + 1), 0)
    c_id = lax.broadcasted_iota(jnp.int32, bm.shape, 0) % CHUNKS_PER_PAGE
    b_next = jnp.where(c_id == CHUNKS_PER_PAGE - 1, nxt_page, nxt)[:CMP_PAGES * rows_pp]
    hid = _gelu_tanh(a + b_next + c_ref[...]).astype(BF16)
    hid = hid.reshape(CMP_PAGES, 2, CHUNKS_PER_PAGE, CMP_HIDDEN)
    out = None
    for g in range(2):
        hg = hid[:, g].reshape(CMP_PAGES * CHUNKS_PER_PAGE, CMP_HIDDEN)
        og = jnp.dot(hg, w2_ref[g], preferred_element_type=F32)
        out = og if out is None else out + og
    o_ref[0] = out.astype(o_ref.dtype)


def _compress(xc, page_table, w1, const, w2):
    b, n_pages = page_table.shape
    assert n_pages % CMP_PAGES == 0
    half = CMP_STRIDE * HEAD_DIM
    wab = jnp.concatenate([_chunk_weight(w1[:half]), _chunk_weight(w1[half:])], axis=1).astype(BF16)
    z = jnp.zeros_like(w2)
    w2p = jnp.stack([jnp.concatenate([w2, z], axis=1), jnp.concatenate([z, w2], axis=1)]).astype(BF16)

    def page_spec(k):
        def imap(bi, i, pt):
            return (pt[bi, jnp.minimum(i * CMP_PAGES + k, n_pages - 1)], 0, 0)
        return pl.BlockSpec((1, 2 * CHUNKS_PER_PAGE, half), imap)

    tile = CMP_PAGES * CHUNKS_PER_PAGE
    grid_spec = pltpu.PrefetchScalarGridSpec(
        num_scalar_prefetch=1, grid=(b, n_pages // CMP_PAGES),
        in_specs=[page_spec(k) for k in range(CMP_PAGES + 1)] + [
            pl.BlockSpec(wab.shape, lambda bi, i, pt: (0, 0)),
            pl.BlockSpec(const.shape, lambda bi, i, pt: (0, 0)),
            pl.BlockSpec(w2p.shape, lambda bi, i, pt: (0, 0, 0))],
        out_specs=pl.BlockSpec((1, tile, LANES), lambda bi, i, pt: (bi, i, 0)))
    return pl.pallas_call(
        _compress_kernel, grid_spec=grid_spec,
        out_shape=jax.ShapeDtypeStruct((b, n_pages * CHUNKS_PER_PAGE, LANES), BF16),
        compiler_params=_cparams(("parallel", "parallel")), name="compress",
    )(page_table, *([xc] * (CMP_PAGES + 1)), wab, const, w2p)


TQ_N = 128
TK_N = 256
NSA_HEADS = 8
assert WINDOW % TK_N == 0 and TK_N % TQ_N == 0 and TK_N == 2 * LANES == MOBA_BLOCK
WIN_TILES = WINDOW // TK_N + 1


def _overlap_matrix(n_cmp_pad, n_sel_pad):
    cstart = np.arange(n_cmp_pad)[:, None] * CMP_STRIDE
    sstart = np.arange(n_sel_pad)[None, :] * SEL_BLOCK
    return ((cstart < sstart + SEL_BLOCK) & (cstart + CMP_LEN > sstart)).astype(np.float32)


def _nsa_kernel(q_ref, gt_ref, kc_ref, vc_ref, ks_ref, vs_ref, kw_ref, vw_ref, ov_ref, tb_ref, wm_ref, o_ref,
                rm_ref, l_ref, acc_ref, *, n_cmp, n_sel, n_tb):
    qi = pl.program_id(1)
    q0 = qi * TQ_N
    rows = NSA_HEADS * TQ_N
    q8 = q_ref[...].reshape(rows, LANES)
    n_cp = kc_ref.shape[1]
    pos = q0 + lax.broadcasted_iota(jnp.int32, (1, TQ_N, 1), 1)

    lc = lax.dot_general(q8, kc_ref[0], NT, preferred_element_type=F32).reshape(NSA_HEADS, TQ_N, n_cp)
    n_id = lax.broadcasted_iota(jnp.int32, (1, 1, n_cp), 2)
    ok = (n_id * CMP_STRIDE + (CMP_LEN - 1) <= pos) & (n_id < n_cmp)
    mc = jnp.max(jnp.where(ok, lc, NEG), axis=2, keepdims=True)
    pc = jnp.where(ok, jnp.exp(lc - mc), 0.0)
    lsum = jnp.sum(pc, axis=2, keepdims=True)
    pc = pc / jnp.where(lsum > 0.0, lsum, 1.0)
    oc = jnp.dot(pc.reshape(rows, n_cp).astype(BF16), vc_ref[0], preferred_element_type=F32)

    pcg = pc.reshape(2, 4, TQ_N, n_cp)
    pcg = (pcg[:, 0] + pcg[:, 1]) + (pcg[:, 2] + pcg[:, 3])
    imp = jnp.dot(pcg.reshape(2 * TQ_N, n_cp), ov_ref[...], precision=HI, preferred_element_type=F32)
    n_sp = imp.shape[1]
    imp = imp.reshape(2, TQ_N, n_sp)
    sid = lax.broadcasted_iota(jnp.int32, (1, 1, n_sp), 2)
    cur = pos // SEL_BLOCK
    forced = (sid == 0) | (sid == cur) | (sid == cur - 1)
    valid = (sid * SEL_BLOCK <= pos) & (sid < n_sel)
    score = jnp.where(forced, FORCE, imp)
    sid_full = lax.broadcasted_iota(jnp.int32, (2, TQ_N, n_sp), 2)
    sel, _ = _topk_mask(score, valid, min(SEL_TOPN, n_sel), sid_full, 2)
    selneg = jnp.where(sel > 0.5, 0.0, NEG).astype(BF16)
    selneg8 = jnp.broadcast_to(selneg[:, None], (2, 4, TQ_N, n_sp)).reshape(rows, n_sp)
    q_aug = jnp.concatenate([q8, selneg8], axis=1)

    jmax = q0 // TK_N
    e_lane = lax.broadcasted_iota(jnp.int32, (TK_N, n_sp), 1)
    e_blk = lax.broadcasted_iota(jnp.int32, (TK_N, n_sp), 0) // SEL_BLOCK

    def bias(j):
        return tb_ref[jnp.minimum(qi - 2 * j, n_tb - 1)].reshape(rows, TK_N)

    def kslice(ref, j):
        return ref[0, pl.ds(pl.multiple_of(j * TK_N, TK_N), TK_N), :]

    def sel_logits(j):
        k_aug = jnp.concatenate([kslice(ks_ref, j), (e_lane == e_blk + j * (TK_N // SEL_BLOCK)).astype(BF16)], axis=1)
        return lax.dot_general(q_aug, k_aug, NT, preferred_element_type=F32) + bias(j)

    osel = _two_pass_attention(0, jmax, sel_logits, lambda j: kslice(vs_ref, j), rm_ref, l_ref, acc_ref)

    def win_logits(j):
        s = lax.dot_general(q8, kslice(kw_ref, j), NT, preferred_element_type=F32) + bias(j)
        wmask = wm_ref[jnp.minimum(qi - 2 * j, wm_ref.shape[0] - 1)]
        return (s.reshape(NSA_HEADS, TQ_N, TK_N) + wmask[None]).reshape(rows, TK_N)

    jlo = jnp.maximum(jmax - (WIN_TILES - 1), 0)
    ow = _two_pass_attention(jlo, jmax, win_logits, lambda j: kslice(vw_ref, j), rm_ref, l_ref, acc_ref)

    gt = gt_ref[...]
    glane = lax.broadcasted_iota(jnp.int32, gt.shape, 1)

    def gate(hd, br):
        return jnp.sum(jnp.where(glane == hd * 3 + br, gt, 0.0), axis=1, keepdims=True)

    for hd in range(NSA_HEADS):
        r0 = hd * TQ_N
        o = (gate(hd, 0) * oc[r0:r0 + TQ_N] + gate(hd, 1) * osel[r0:r0 + TQ_N] + gate(hd, 2) * ow[r0:r0 + TQ_N])
        o_ref[hd] = o.astype(o_ref.dtype)


def _nsa_prompt(qn8, gates, kcmp, vcmp, ksb, vsb, kwb, vwb, tbn, wm, n_cmp):
    b, t, _ = ksb.shape
    n_cp = kcmp.shape[1]
    n_sel = t // SEL_BLOCK
    n_sp = -(-n_sel // LANES) * LANES
    ov = jnp.asarray(_overlap_matrix(n_cp, n_sp))
    n_tb = tbn.shape[0]
    nq = t // TQ_N
    full = lambda a: pl.BlockSpec((1,) + a.shape[1:], lambda bi, qi: (bi,) + (0,) * (a.ndim - 1))
    once = lambda a: pl.BlockSpec(a.shape, lambda bi, qi: (0,) * a.ndim, pipeline_mode=pl.Buffered(1))
    return pl.pallas_call(
        functools.partial(_nsa_kernel, n_cmp=n_cmp, n_sel=n_sel, n_tb=n_tb),
        grid=(b, nq),
        in_specs=[pl.BlockSpec((NSA_HEADS, TQ_N, LANES), lambda bi, qi: (0, bi * nq + qi, 0)),
                  pl.BlockSpec((TQ_N, LANES), lambda bi, qi: (bi * nq + qi, 0)),
                  full(kcmp), full(vcmp), full(ksb), full(vsb), full(kwb), full(vwb),
                  once(ov), once(tbn), once(wm)],
        out_specs=pl.BlockSpec((NSA_HEADS, TQ_N, LANES), lambda bi, qi: (0, bi * nq + qi, 0)),
        out_shape=jax.ShapeDtypeStruct((NSA_HEADS, b * t, LANES), BF16),
        scratch_shapes=[pltpu.VMEM((NSA_HEADS * TQ_N, LANES), F32)] * 3,
        compiler_params=_cparams(("parallel", "parallel")), name="nsa_prompt",
    )(qn8, gates, kcmp, vcmp, ksb, vsb, kwb, vwb, ov, tbn, wm)


def _mix_kernel(x_ref, om_ref, on_ref, wom_ref, won_ref, g_ref, wr_ref, br_ref, c0_ref,
                r_ref, h_ref, ti_ref, tg_ref, rk_ref, cnt_ref, carry):
    @pl.when(pl.program_id(0) == 0)
    def _():
        carry[...] = c0_ref[...]

    acc = x_ref[...] + jnp.dot(om_ref[...], wom_ref[...], preferred_element_type=F32)
    for hd in range(NSA_HEADS):
        acc = acc + jnp.dot(on_ref[hd], won_ref[hd], preferred_element_type=F32)
    r_ref[...] = acc
    h = acc * lax.rsqrt(jnp.mean(acc * acc, axis=-1, keepdims=True) + RMS_EPS) * g_ref[...]
    h_ref[...] = h
    logits = jnp.dot(h, wr_ref[...], precision=HI, preferred_element_type=F32) + br_ref[...]
    e_id = lax.broadcasted_iota(jnp.int32, logits.shape, 1)
    work = logits
    vals, ids = [], []
    for _ in range(TOP_K):
        mx = jnp.max(work, axis=1, keepdims=True)
        idx = jnp.min(jnp.where(work == mx, e_id, N_EXPERTS), axis=1, keepdims=True)
        work = jnp.where(e_id == idx, -3e38, work)
        vals.append(mx)
        ids.append(idx)
    ex = [jnp.exp(v - vals[0]) for v in vals]
    tot = ex[0]
    for e in ex[1:]:
        tot = tot + e
    tm = logits.shape[0]
    onehot = [(e_id == ids[k]).astype(F32) for k in range(TOP_K)]
    oh_all = (onehot[0] + onehot[1]) + (onehot[2] + onehot[3])
    earlier = (lax.broadcasted_iota(jnp.int32, (tm, tm), 1) < lax.broadcasted_iota(jnp.int32, (tm, tm), 0))
    before = jnp.dot(earlier.astype(BF16), oh_all.astype(BF16), preferred_element_type=F32) + carry[...]
    k_id = lax.broadcasted_iota(jnp.int32, ti_ref.shape, 1)
    ti = jnp.zeros(ti_ref.shape, jnp.int32)
    rk = jnp.zeros(rk_ref.shape, jnp.int32)
    tg = jnp.zeros(tg_ref.shape, F32)
    for k in range(TOP_K):
        rank_k = jnp.sum(onehot[k] * before, axis=1, keepdims=True)
        ti = jnp.where(k_id == k, ids[k], ti)
        rk = jnp.where(k_id == k, rank_k.astype(jnp.int32), rk)
        tg = jnp.where(k_id == k, ex[k] / tot, tg)
        before = before + onehot[k]
    ti_ref[...] = ti
    rk_ref[...] = rk
    tg_ref[...] = tg
    carry[...] += jnp.sum(oh_all, axis=0, keepdims=True)
    cnt_ref[...] = carry[...]


def _mix(x, om, on8, wom, won8, g_ffn, w_router, b_router, counts0, tm):
    t, d = x.shape
    row = lambda w_: pl.BlockSpec((tm, w_), lambda i: (i, 0))
    return pl.pallas_call(
        _mix_kernel, grid=(t // tm,),
        in_specs=[row(d), row(512), pl.BlockSpec((NSA_HEADS, tm, LANES), lambda i: (0, i, 0)),
                  _const_spec(wom.shape), _const_spec(won8.shape), _const_spec((1, d)),
                  _const_spec(w_router.shape), _const_spec((1, N_EXPERTS)), _const_spec((1, N_EXPERTS))],
        out_specs=[row(d), row(d), row(TOP_K), row(TOP_K), row(TOP_K), _const_spec((1, N_EXPERTS))],
        out_shape=[jax.ShapeDtypeStruct((t, d), F32), jax.ShapeDtypeStruct((t, d), F32),
                   jax.ShapeDtypeStruct((t, TOP_K), jnp.int32), jax.ShapeDtypeStruct((t, TOP_K), F32),
                   jax.ShapeDtypeStruct((t, TOP_K), jnp.int32), jax.ShapeDtypeStruct((1, N_EXPERTS), F32)],
        scratch_shapes=[pltpu.VMEM((1, N_EXPERTS), F32)],
        compiler_params=_cparams(("arbitrary",)), name="mix_router",
    )(x, om, on8, wom, won8, g_ffn.reshape(1, d), w_router, b_router.reshape(1, N_EXPERTS), counts0)


TM_E = 256


def _route(top_i, rank, counts, tm):
    n_assign = top_i.shape[0] * TOP_K
    cnt = counts.reshape(N_EXPERTS).astype(jnp.int32)
    padded = (cnt + tm - 1) // tm * tm
    pad_end = jnp.cumsum(padded)
    dest = (jnp.take(pad_end - padded, top_i, axis=0) + rank).reshape(n_assign)
    n_blocks = -(-n_assign // tm) + N_EXPERTS
    blk_e = jnp.minimum(jnp.searchsorted(pad_end, jnp.arange(n_blocks) * tm, side='right'), N_EXPERTS - 1)
    n_used = (pad_end[-1] // tm).astype(jnp.int32).reshape(1)
    return dest.astype(jnp.int32), blk_e.astype(jnp.int32), n_used, n_blocks


DISPATCH_UNROLL = 8


def _dispatch_kernel(d_ref, h_ref, xs_in, xs_out, sem):
    del xs_in
    tm = h_ref.shape[0]
    unroll = math.gcd(tm, DISPATCH_UNROLL)

    def body(c, carry):
        for u in range(unroll):
            r = c * unroll + u
            for k in range(TOP_K):
                pltpu.make_async_copy(h_ref.at[pl.ds(r, 1), :], xs_out.at[pl.ds(d_ref[r * TOP_K + k], 1), :], sem).start()
        return carry

    lax.fori_loop(0, tm // unroll, body, 0)
    for k in range(TOP_K):
        pltpu.make_async_copy(h_ref, xs_out.at[pl.ds(0, tm), :], sem).wait()


def _dispatch(h, dest, xs, tm):
    t, d = h.shape
    assert t % tm == 0
    return pl.pallas_call(
        _dispatch_kernel, grid=(t // tm,),
        in_specs=[pl.BlockSpec((tm * TOP_K,), lambda i: (i,), memory_space=pltpu.SMEM),
                  pl.BlockSpec((tm, d), lambda i: (i, 0)),
                  pl.BlockSpec(memory_space=pl.ANY)],
        out_specs=pl.BlockSpec(memory_space=pl.ANY),
        out_shape=jax.ShapeDtypeStruct(xs.shape, xs.dtype),
        scratch_shapes=[pltpu.SemaphoreType.DMA(())],
        input_output_aliases={2: 0},
        compiler_params=_cparams(("arbitrary",)), name="moe_dispatch",
    )(dest, h, xs)


def _expert_kernel(blk_e_ref, n_used_ref, x_ref, wgu_ref, bgu_ref, wd_ref, bd_ref, y_ref):
    i = pl.program_id(0)

    @pl.when(i < n_used_ref[0])
    def _():
        gu = jnp.dot(x_ref[...].astype(BF16), wgu_ref[0], preferred_element_type=F32) + bgu_ref[0]
        glu = jnp.minimum(gu, SWIGLU_LIMIT)
        lin = jnp.clip(gu, -SWIGLU_LIMIT, SWIGLU_LIMIT) + 1.0
        lin_next = pltpu.roll(lin, lin.shape[1] - 1, 1)
        act = glu * jax.nn.sigmoid(SWIGLU_ALPHA * glu) * lin_next
        y_ref[...] = jnp.dot(act.astype(BF16), wd_ref[0], preferred_element_type=F32) + bd_ref[0]

    @pl.when(i >= n_used_ref[0])
    def _():
        y_ref[...] = jnp.zeros_like(y_ref)


def _experts(xs, blk_e, n_used, wgu, bgu, wd, bd):
    n_rows, d = xs.shape
    dff2 = wgu.shape[2]
    tm = TM_E
    e_map = lambda i, be, nu: (be[i], 0, 0)
    grid_spec = pltpu.PrefetchScalarGridSpec(
        num_scalar_prefetch=2, grid=(n_rows // tm,),
        in_specs=[pl.BlockSpec((tm, d), lambda i, be, nu: (i, 0)),
                  pl.BlockSpec((1, d, dff2), e_map), pl.BlockSpec((1, 1, dff2), e_map),
                  pl.BlockSpec((1, dff2, d), e_map), pl.BlockSpec((1, 1, d), e_map)],
        out_specs=pl.BlockSpec((tm, d), lambda i, be, nu: (i, 0)))
    return pl.pallas_call(
        _expert_kernel, grid_spec=grid_spec,
        out_shape=jax.ShapeDtypeStruct((n_rows, d), F32),
        compiler_params=_cparams(("parallel",)), name="moe_experts",
    )(blk_e, n_used, xs, wgu, bgu, wd, bd)


def _combine_kernel(d_cur_ref, d_nxt_ref, y_hbm, r_ref, tg_ref, p_ref, wpg_ref, wpp_ref, gf_ref, o_ref, ybuf, sem):
    i = pl.program_id(0)
    n = pl.num_programs(0)
    tm = ybuf.shape[2]

    def start_gather(d_ref, slot):
        for r in range(tm):
            for k in range(TOP_K):
                pltpu.make_async_copy(y_hbm.at[pl.ds(d_ref[r * TOP_K + k], 1), :],
                                      ybuf.at[slot, k, pl.ds(r, 1), :], sem.at[slot]).start()

    def wait_gather(slot):
        for k in range(TOP_K):
            pltpu.make_async_copy(y_hbm.at[pl.ds(0, tm), :], ybuf.at[slot, k], sem.at[slot]).wait()

    @pl.when(i == 0)
    def _():
        start_gather(d_cur_ref, 0)

    def step(slot):
        wait_gather(slot)
        start_gather(d_nxt_ref, 1 - slot)
        tg = tg_ref[...]
        r = r_ref[...]
        for k in range(TOP_K):
            r = r + tg[:, k:k + 1] * ybuf[slot, k]
        gate = jax.nn.sigmoid(jnp.dot(r.astype(BF16), wpg_ref[...], preferred_element_type=F32))
        r = r + gate * jnp.dot(p_ref[...].astype(BF16), wpp_ref[...], preferred_element_type=F32)
        o_ref[...] = r * lax.rsqrt(jnp.mean(r * r, axis=-1, keepdims=True) + RMS_EPS) * gf_ref[...]

        @pl.when(i == n - 1)
        def _():
            wait_gather(1 - slot)

    for parity in range(2):
        pl.when(i % 2 == parity)(functools.partial(step, parity))


def _combine(y, dest, r1, tg, p, wpg, wpp, g_final, tm):
    t, d = r1.shape
    nt = t // tm
    row = lambda w_: pl.BlockSpec((tm, w_), lambda i: (i, 0))
    return pl.pallas_call(
        _combine_kernel, grid=(nt,),
        in_specs=[pl.BlockSpec((tm * TOP_K,), lambda i: (i,), memory_space=pltpu.SMEM),
                  pl.BlockSpec((tm * TOP_K,), lambda i: (jnp.minimum(i + 1, nt - 1),), memory_space=pltpu.SMEM),
                  pl.BlockSpec(memory_space=pl.ANY),
                  row(d), row(TOP_K), row(p.shape[1]),
                  _const_spec(wpg.shape), _const_spec(wpp.shape), _const_spec((1, d))],
        out_specs=row(d),
        out_shape=jax.ShapeDtypeStruct((t, d), F32),
        scratch_shapes=[pltpu.VMEM((2, TOP_K, tm, d), F32), pltpu.SemaphoreType.DMA((2,))],
        compiler_params=_cparams(("arbitrary",)), name="moe_combine",
    )(dest, dest, y, r1, tg, p, wpg, wpp, g_final.reshape(1, d))


def _ffn(groups, wts):
    mixed, counts = [], jnp.zeros((1, N_EXPERTS), F32)
    for x, om, on8, _, tm in groups:
        *outs, counts = _mix(x, om, on8, wts["wom"], wts["won8"], wts["g_ffn"], wts["w_router"], wts["b_router"],
                             counts, tm)
        mixed.append(outs)
    dest, blk_e, n_used, n_blocks = _route(jnp.concatenate([m[2] for m in mixed], axis=0),
                                           jnp.concatenate([m[4] for m in mixed], axis=0), counts, TM_E)
    xs = jnp.zeros((n_blocks * TM_E, groups[0][0].shape[1]), F32)
    dests, start = [], 0
    for (x, _, _, _, tm), m in zip(groups, mixed):
        n = x.shape[0] * TOP_K
        dests.append(dest[start:start + n])
        xs = _dispatch(m[1], dests[-1], xs, min(2 * tm, x.shape[0]))
        start += n
    y = _experts(xs, blk_e, n_used, wts["wgu"], wts["bgu"], wts["wd"], wts["bd"])
    return [_combine(y, d_g, m[0], m[3], p, wts["wpg"], wts["wpp"], wts["g_final"], tm)
            for (_, _, _, p, tm), m, d_g in zip(groups, mixed, dests)]


KMEAN_PAGES = 8
PAGES_PER_MOBA = MOBA_BLOCK // PAGE_SIZE
SEL_PER_PAGE = PAGE_SIZE // SEL_BLOCK


def _kmean_kernel(pt_ref, *refs):
    x_refs, o_ref = refs[:KMEAN_PAGES], refs[KMEAN_PAGES]
    i = pl.program_id(1)
    nb_step = KMEAN_PAGES // PAGES_PER_MOBA

    @pl.when(i == 0)
    def _():
        o_ref[...] = jnp.zeros_like(o_ref)

    blk_lane = lax.broadcasted_iota(jnp.int32, o_ref.shape, 2)
    acc = o_ref[...]
    for blk in range(nb_step):
        s = x_refs[blk * PAGES_PER_MOBA][...]
        for k in range(1, PAGES_PER_MOBA):
            s = s + x_refs[blk * PAGES_PER_MOBA + k][...]
        col = jnp.sum(s, axis=-1, keepdims=True) * (1.0 / MOBA_BLOCK)
        acc = jnp.where(blk_lane == i * nb_step + blk, col, acc)
    o_ref[...] = acc


def _moba_kmean(cache_t, page_table):
    b, n_pages = page_table.shape
    assert n_pages % KMEAN_PAGES == 0
    _, _, h, d, ps = cache_t.shape

    def page_spec(k):
        return pl.BlockSpec((None, None, h, d, ps), lambda bi, i, pt: (0, pt[bi, i * KMEAN_PAGES + k], 0, 0, 0))

    n_blk = n_pages // PAGES_PER_MOBA
    grid_spec = pltpu.PrefetchScalarGridSpec(
        num_scalar_prefetch=1, grid=(b, n_pages // KMEAN_PAGES),
        in_specs=[page_spec(k) for k in range(KMEAN_PAGES)],
        out_specs=pl.BlockSpec((None, h, d, n_blk), lambda bi, i, pt: (bi, 0, 0, 0)))
    return pl.pallas_call(
        _kmean_kernel, grid_spec=grid_spec,
        out_shape=jax.ShapeDtypeStruct((b, h, d, n_blk), F32),
        compiler_params=_cparams(("parallel", "arbitrary")), name="moba_kmean",
    )(page_table, *([cache_t] * KMEAN_PAGES))


def _dec_select_kernel(qm_ref, kmean_ref, qn_ref, kc_ref, vc_ref, mid_ref, oc_ref, pcg_ref, *, n_cmp):
    for hd in range(8):
        gate = jnp.sum(kmean_ref[hd] * qm_ref[hd], axis=0, keepdims=True)
        blk = lax.broadcasted_iota(jnp.int32, gate.shape, 1)
        _, ids = _topk_mask(gate, blk >= 0, MOBA_TOPK, blk, 1)
        for k in range(MOBA_TOPK):
            mid_ref[k:k + 1, hd:hd + 1] = ids[k]
    lc = lax.dot_general(qn_ref[...], kc_ref[...], NT, preferred_element_type=F32)
    ok = lax.broadcasted_iota(jnp.int32, lc.shape, 1) < n_cmp
    mc = jnp.max(jnp.where(ok, lc, NEG), axis=1, keepdims=True)
    pc = jnp.where(ok, jnp.exp(lc - mc), 0.0)
    lsum = jnp.sum(pc, axis=1, keepdims=True)
    pc = pc / jnp.where(lsum > 0.0, lsum, 1.0)
    oc_ref[...] = jnp.dot(pc.astype(BF16), vc_ref[...], preferred_element_type=F32)
    for g in range(2):
        pcg_ref[g:g + 1, :] = jnp.sum(pc[4 * g:4 * g + 4], axis=0, keepdims=True)


def _dec_select(qm_col, kmean_t, qn8, kcmp, vcmp, n_cmp):
    b, _, _, n_blk = kmean_t.shape
    n_cp = kcmp.shape[1]
    per_b = lambda *s: pl.BlockSpec((None,) + s, lambda bi: (bi,) + (0,) * len(s))
    return pl.pallas_call(
        functools.partial(_dec_select_kernel, n_cmp=n_cmp), grid=(b,),
        in_specs=[per_b(8, HEAD_DIM, 1), per_b(8, HEAD_DIM, n_blk), per_b(8, LANES), per_b(n_cp, LANES),
                  per_b(n_cp, LANES)],
        out_specs=[per_b(MOBA_TOPK, 8), per_b(8, LANES), per_b(2, n_cp)],
        out_shape=[jax.ShapeDtypeStruct((b, MOBA_TOPK, 8), jnp.int32), jax.ShapeDtypeStruct((b, 8, LANES), F32),
                   jax.ShapeDtypeStruct((b, 2, n_cp), F32)],
        compiler_params=_cparams(("parallel",)), name="dec_select",
    )(qm_col, kmean_t, qn8, kcmp, vcmp)


def _dec_topn_kernel(pcg_ref, ov_ref, o_ref, *, pos, n_sel):
    imp = jnp.dot(pcg_ref[...], ov_ref[...], precision=HI, preferred_element_type=F32)
    sid = lax.broadcasted_iota(jnp.int32, imp.shape, 1)
    cur = pos // SEL_BLOCK
    forced = (sid == 0) | (sid == cur) | (sid == cur - 1)
    valid = (sid * SEL_BLOCK <= pos) & (sid < n_sel)
    n_top = min(SEL_TOPN, n_sel)
    _, ids = _topk_mask(jnp.where(forced, FORCE, imp), valid, n_top, sid, 1)
    k_id = lax.broadcasted_iota(jnp.int32, o_ref.shape, 1)
    out = jnp.full(o_ref.shape, imp.shape[1], jnp.int32)
    for k in range(n_top):
        out = jnp.where(k_id == k, ids[k], out)
    o_ref[...] = out


def _dec_topn(pcg, pos, n_sel):
    rows, n_cp = pcg.shape
    n_sp = -(-n_sel // LANES) * LANES
    ov = jnp.asarray(_overlap_matrix(n_cp, n_sp))
    return pl.pallas_call(
        functools.partial(_dec_topn_kernel, pos=pos, n_sel=n_sel),
        out_shape=jax.ShapeDtypeStruct((rows, SEL_TOPN), jnp.int32), name="dec_topn",
        compiler_params=pltpu.CompilerParams(vmem_limit_bytes=VMEM_LIMIT),
    )(pcg, ov)


def _attend_t(s_self, v_self, logit_tiles, value_tiles_t):
    m = s_self
    for s in logit_tiles:
        m = jnp.maximum(m, jnp.max(s, axis=1, keepdims=True))
    p_self = jnp.exp(s_self - m)
    l = p_self
    acc = p_self * v_self
    for s, vt in zip(logit_tiles, value_tiles_t):
        p = jnp.exp(s - m)
        l = l + jnp.sum(p, axis=1, keepdims=True)
        acc = acc + lax.dot_general(p.astype(BF16), vt, NT, preferred_element_type=F32)
    return acc / l


def _dec_attn_kernel(pt_ref, mid_ref, sid_ref, qm_ref, kmn_ref, vmn_ref, qn_ref, new_ref, gt_ref, oc_ref,
                     wk_ref, wv_ref, pbm_ref, pbn_ref, wb_ref, b0_ref,
                     mk_hbm, mv_hbm, sk_hbm, sv_hbm, om_ref, on_ref,
                     kmbuf, vmbuf, ksbuf, vsbuf, sem, *, n_blk, n_selp):
    bi = pl.program_id(0)
    n_top = ksbuf.shape[1]

    def moba_copies(hd, k):
        blk = jnp.minimum(mid_ref[bi, k * 8 + hd], n_blk - 1)
        out = []
        for half in range(PAGES_PER_MOBA):
            page = pt_ref[bi, blk * PAGES_PER_MOBA + half]
            out.append(pltpu.make_async_copy(mk_hbm.at[0, page, hd], kmbuf.at[hd, k, half], sem.at[0]))
            out.append(pltpu.make_async_copy(mv_hbm.at[0, page, hd], vmbuf.at[hd, k, half], sem.at[0]))
        return out

    def sel_copies(g, k):
        s = jnp.minimum(sid_ref[bi, g * n_top + k], n_selp - 1)
        page = pt_ref[bi, s // SEL_PER_PAGE]
        return [pltpu.make_async_copy(sk_hbm.at[0, page, g], ksbuf.at[g, k], sem.at[1]),
                pltpu.make_async_copy(sv_hbm.at[0, page, g], vsbuf.at[g, k], sem.at[1])]

    copies = [c for hd in range(8) for k in range(MOBA_TOPK) for c in moba_copies(hd, k)]
    copies += [c for g in range(2) for k in range(n_top) for c in sel_copies(g, k)]
    for c in copies:
        c.start()
    for c in copies:
        c.wait()

    b0 = b0_ref[...]

    for hd in range(8):
        qf = qm_ref[hd:hd + 1, :] * SCALE
        qh = qf.astype(BF16)
        s_self = jnp.sum(qf * kmn_ref[hd:hd + 1, :], axis=1, keepdims=True) + b0[hd:hd + 1]
        ss, vv = [], []
        for k in range(MOBA_TOPK):
            raw = mid_ref[bi, k * 8 + hd]
            blk = jnp.minimum(raw, n_blk - 1)
            for half in range(PAGES_PER_MOBA):
                s = jnp.dot(qh, kmbuf[hd, k, half].astype(BF16), preferred_element_type=F32)
                s = s + pbm_ref[blk][hd:hd + 1, half * PAGE_SIZE:(half + 1) * PAGE_SIZE]
                ss.append(jnp.where(raw < n_blk, s, NEG))
                vv.append(vmbuf[hd, k, half].astype(BF16))
        om_ref[hd:hd + 1, :] = _attend_t(s_self, vmn_ref[hd:hd + 1, :], ss, vv)

    gt = gt_ref[...]
    for g in range(2):
        lanes = slice(g * HEAD_DIM, (g + 1) * HEAD_DIM)
        hs = slice(4 * g, 4 * g + 4)
        qh = qn_ref[hs, lanes]
        qf = qh.astype(F32)
        ks_new, vs_new = new_ref[0:1, lanes], new_ref[1:2, lanes]
        kw_new, vw_new = new_ref[2:3, lanes], new_ref[3:4, lanes]

        b0g = b0[8 + 4 * g:12 + 4 * g]
        row_half = lax.broadcasted_iota(jnp.int32, (4, PAGE_SIZE), 1) // SEL_BLOCK
        ss, vv = [], []
        for k in range(n_top):
            raw = sid_ref[bi, g * n_top + k]
            sblk = jnp.minimum(raw, n_selp - 1)
            s = jnp.dot(qh, ksbuf[g, k].astype(BF16), preferred_element_type=F32)
            s = s + pbn_ref[sblk // SEL_PER_PAGE][hs, :]
            ss.append(jnp.where((row_half == sblk % SEL_PER_PAGE) & (raw < n_selp), s, NEG))
            vv.append(vsbuf[g, k].astype(BF16))
        s_self = jnp.sum(qf * ks_new, axis=1, keepdims=True) + b0g
        osel = _attend_t(s_self, vs_new, ss, vv)

        sw = jnp.dot(qh, wk_ref[g].astype(BF16), preferred_element_type=F32) + wb_ref[hs, :]
        s_self = jnp.sum(qf * kw_new, axis=1, keepdims=True) + b0g
        ow = _attend_t(s_self, vw_new, [sw], [wv_ref[g].astype(BF16)])

        oc = oc_ref[hs, lanes]
        for r in range(4):
            hd = 4 * g + r
            o = (gt[:, hd * 3:hd * 3 + 1] * oc[r:r + 1] + gt[:, hd * 3 + 1:hd * 3 + 2] * osel[r:r + 1]
                 + gt[:, hd * 3 + 2:hd * 3 + 3] * ow[r:r + 1])
            on_ref[hd:hd + 1, :] = o


def _dec_attn(page_table, mids, sids, qm, kmn, vmn, qn8, new_rows, gates, oc, wk, wv, pbm, pbn, wb, b0,
              cache_mk, cache_mv, cache_sk, cache_sv):
    b = page_table.shape[0]
    n_blk = pbm.shape[0]
    n_selp = pbn.shape[0] * SEL_PER_PAGE
    n_top = sids.shape[1] // 2
    win = wk.shape[3]
    per_b = lambda *s: pl.BlockSpec((None,) + s, lambda bi, *_: (bi,) + (0,) * len(s))
    const = lambda a: pl.BlockSpec(a.shape, lambda bi, *_: (0,) * a.ndim)
    any_spec = pl.BlockSpec(memory_space=pl.ANY)
    grid_spec = pltpu.PrefetchScalarGridSpec(
        num_scalar_prefetch=3, grid=(b,),
        in_specs=[per_b(8, HEAD_DIM), per_b(8, HEAD_DIM), per_b(8, HEAD_DIM), per_b(8, LANES), per_b(4, LANES),
                  per_b(1, LANES), per_b(8, LANES), per_b(2, HEAD_DIM, win), per_b(2, HEAD_DIM, win),
                  const(pbm), const(pbn), const(wb), const(b0), any_spec, any_spec, any_spec, any_spec],
        out_specs=[per_b(8, HEAD_DIM), per_b(8, HEAD_DIM)],
        scratch_shapes=[pltpu.VMEM((8, MOBA_TOPK, PAGES_PER_MOBA, HEAD_DIM, PAGE_SIZE), F32),
                        pltpu.VMEM((8, MOBA_TOPK, PAGES_PER_MOBA, HEAD_DIM, PAGE_SIZE), F32),
                        pltpu.VMEM((2, n_top, HEAD_DIM, PAGE_SIZE), F32),
                        pltpu.VMEM((2, n_top, HEAD_DIM, PAGE_SIZE), F32),
                        pltpu.SemaphoreType.DMA((2,))])
    return pl.pallas_call(
        functools.partial(_dec_attn_kernel, n_blk=n_blk, n_selp=n_selp), grid_spec=grid_spec,
        out_shape=[jax.ShapeDtypeStruct((b, 8, HEAD_DIM), F32), jax.ShapeDtypeStruct((b, 8, HEAD_DIM), F32)],
        compiler_params=_cparams(("arbitrary",)), name="dec_attn",
    )(page_table, mids, sids, qm, kmn, vmn, qn8, new_rows, gates, oc, wk, wv, pbm, pbn, wb, b0,
      cache_mk, cache_mv, cache_sk, cache_sv)


def kernel(x_prompt, x_sample, p_prompt, p_sample, cache_moba_k, cache_moba_v, cache_nsa_cmp_k, cache_nsa_cmp_v, cache_nsa_sel_k, cache_nsa_sel_v, state_nsa_win_k, state_nsa_win_v, page_table, rel_bias, g_mix, w_in, w_out, cmp_w1_k, cmp_b1_k, cmp_w2_k, cmp_pos_k, cmp_w1_v, cmp_b1_v, cmp_w2_v, cmp_pos_v, g_ffn, w_router, b_router, w_gate_up, b_gate_up, w_down, b_down, w_ple_proj, w_ple_gate, g_final):
    assert rel_bias.shape == (N_BUCKETS, 16) and g_mix.shape[0] == 1, "one layer, 8 MoBA + 8 NSA heads"
    b, t, d = x_prompt.shape
    bd, dec_seq, _ = x_sample.shape
    assert dec_seq == 1 and t % (2 * MOBA_BLOCK) == 0
    n_pages = page_table.shape[1]
    past = n_pages * PAGE_SIZE
    win = state_nsa_win_k.shape[2]

    bv = rel_bias[_t5_bucket(jnp.arange(MAX_DISTANCE + 1))]
    n_tbm = -(-(MAX_DISTANCE + MOBA_BLOCK - 1) // TQ_M) + 1
    tbm = _toeplitz_tiles(bv[:, :8], TQ_M, MOBA_BLOCK, n_tbm, True)
    n_tbn = -(-(MAX_DISTANCE + TK_N - 1) // TQ_N) + 1
    tbn = _toeplitz_tiles(bv[:, 8:], TQ_N, TK_N, n_tbn, False)
    n_wm = -(-(WINDOW + TK_N) // TQ_N)
    dist = np.stack([TQ_N * m + np.arange(TQ_N)[:, None] - np.arange(TK_N)[None, :] for m in range(n_wm)])
    wm = jnp.asarray(np.where(dist <= WINDOW, 0.0, NEG).astype(np.float32))
    w_proj = _proj_weight(w_in[0])
    wo = w_out[0]
    won = wo[512:].reshape(NSA_HEADS, HEAD_DIM, d)
    zn = jnp.zeros_like(won)
    grp0 = (jnp.arange(NSA_HEADS) < 4)[:, None, None]
    wd_bf = w_down[0].astype(BF16)
    wts = dict(
        wom=wo[:512].astype(BF16),
        won8=jnp.concatenate([jnp.where(grp0, won, zn), jnp.where(grp0, zn, won)], axis=1).astype(BF16),
        g_ffn=g_ffn[0], w_router=w_router[0], b_router=b_router[0],
        wgu=w_gate_up[0].astype(BF16), bgu=b_gate_up[0][:, None, :],
        wd=jnp.stack([wd_bf, jnp.zeros_like(wd_bf)], axis=2).reshape(N_EXPERTS, -1, d), bd=b_down[0][:, None, :],
        wpg=w_ple_gate[0].astype(BF16), wpp=w_ple_proj[0].astype(BF16), g_final=g_final)
    ck = _cmp_const(cmp_pos_k[0], cmp_w1_k[0], cmp_b1_k[0])
    cv = _cmp_const(cmp_pos_v[0], cmp_w1_v[0], cmp_b1_v[0])

    def pad_heads(o):
        z = jnp.zeros_like(o)
        return jnp.concatenate([jnp.where(grp0, o, z), jnp.where(grp0, z, o)], axis=-1).astype(BF16)

    xp = x_prompt.reshape(b * t, d)
    qm, km, vm, kmb, vmb, qn8, kv, kvb, gt, kmean = _project(xp, g_mix, w_proj, 2 * MOBA_BLOCK, True)
    kmean = jnp.pad(kmean.reshape(b, t // MOBA_BLOCK, 512), ((0, 0), (0, LANES - t // MOBA_BLOCK), (0, 0)))
    om = _moba_prompt(qm.reshape(b, t, 512), kmb.reshape(b, t, 512), vmb.reshape(b, t, 512), kmean, tbm)
    pt_p = jnp.arange(b * t // PAGE_SIZE, dtype=jnp.int32).reshape(b, t // PAGE_SIZE)
    pages_t = lambda a: jnp.transpose(a.reshape(-1, PAGE_SIZE, 2, HEAD_DIM), (0, 2, 3, 1))
    kcmp = _compress(_chunk_layout(pages_t(kv[0])), pt_p, cmp_w1_k[0], ck, cmp_w2_k[0])
    vcmp = _compress(_chunk_layout(pages_t(kv[1])), pt_p, cmp_w1_v[0], cv, cmp_w2_v[0])
    seq = lambda a: a.reshape(b, t, LANES)
    on8 = _nsa_prompt(qn8, gt, kcmp, vcmp, seq(kvb[2]), seq(kvb[3]), seq(kvb[4]), seq(kvb[5]), tbn, wm,
                      (t - CMP_LEN) // CMP_STRIDE + 1)

    xs = x_sample.reshape(bd, d)
    qm_s, km_s, vm_s, _, _, qn8_s, kv_s, _, gt_s = _project(xs, g_mix, w_proj, bd, False)
    rows_minor = lambda a: jnp.transpose(a, (0, 1, 3, 4, 2))
    kmean_s = _moba_kmean(rows_minor(cache_moba_k), page_table)
    kcmp_s = _compress(_chunk_layout(rows_minor(cache_nsa_cmp_k)[0]), page_table, cmp_w1_k[0], ck, cmp_w2_k[0])
    vcmp_s = _compress(_chunk_layout(rows_minor(cache_nsa_cmp_v)[0]), page_table, cmp_w1_v[0], cv, cmp_w2_v[0])
    heads = lambda a: a.reshape(bd, 8, HEAD_DIM)
    qn8_sb = jnp.transpose(qn8_s, (1, 0, 2))
    mids, oc_s, pcg = _dec_select(qm_s.reshape(bd, 8, HEAD_DIM, 1), kmean_s, qn8_sb, kcmp_s, vcmp_s,
                                  (past + 1 - CMP_LEN) // CMP_STRIDE + 1)
    n_sel_s = -(-(past + 1) // SEL_BLOCK)
    sids = _dec_topn(pcg.reshape(bd * 2, -1), past, n_sel_s)
    n_blk = past // MOBA_BLOCK
    kpos_m = np.arange(n_blk * MOBA_BLOCK).reshape(n_blk, MOBA_BLOCK)
    pbm = jnp.transpose(jnp.take(bv[:, :8], jnp.asarray(np.minimum(past - kpos_m, MAX_DISTANCE)), axis=0), (0, 2, 1))
    kpos_n = np.arange(past).reshape(n_pages, PAGE_SIZE)
    pbn = jnp.transpose(jnp.take(bv[:, 8:], jnp.asarray(np.minimum(past - kpos_n, MAX_DISTANCE)), axis=0), (0, 2, 1))
    wb = jnp.take(bv[:, 8:], jnp.asarray(np.minimum(win - np.arange(win), MAX_DISTANCE)), axis=0).T
    b0 = bv[0][:, None]
    new_rows = jnp.stack([kv_s[2], kv_s[3], kv_s[4], kv_s[5]], axis=1)
    om_s, on_s = _dec_attn(page_table, mids.reshape(bd, -1), sids.reshape(bd, -1), heads(qm_s), heads(km_s),
                           heads(vm_s), qn8_sb, new_rows, gt_s.reshape(bd, 1, LANES), oc_s,
                           rows_minor(state_nsa_win_k)[0], rows_minor(state_nsa_win_v)[0],
                           pbm, pbn, wb, b0, rows_minor(cache_moba_k), rows_minor(cache_moba_v),
                           rows_minor(cache_nsa_sel_k), rows_minor(cache_nsa_sel_v))
    y_p, y_s = _ffn([(xp, om.reshape(b * t, 512), on8, p_prompt[0].reshape(b * t, -1), 256),
                     (xs, om_s.reshape(bd, 512).astype(BF16), pad_heads(jnp.transpose(on_s, (1, 0, 2))),
                      p_sample[0].reshape(bd, -1), bd)], wts)

    wp = min(WINDOW, t)
    mh = lambda a, n, h: a.reshape(1, n, -1, h, HEAD_DIM)
    kw_new = kv_s[4].reshape(bd, 1, 2, HEAD_DIM)
    vw_new = kv_s[5].reshape(bd, 1, 2, HEAD_DIM)
    win_k = jnp.concatenate([state_nsa_win_k[0], kw_new], axis=1)[:, -win:][None]
    win_v = jnp.concatenate([state_nsa_win_v[0], vw_new], axis=1)[:, -win:][None]
    return (y_p.reshape(b, t, d), y_s.reshape(bd, 1, d),
            mh(km, b, 8), mh(vm, b, 8), mh(kv[0], b, 2), mh(kv[1], b, 2), mh(kv[2], b, 2), mh(kv[3], b, 2),
            mh(kv[4], b, 2)[:, :, -wp:], mh(kv[5], b, 2)[:, :, -wp:],
            mh(km_s, bd, 8), mh(vm_s, bd, 8), mh(kv_s[0], bd, 2), mh(kv_s[1], bd, 2), mh(kv_s[2], bd, 2),
            mh(kv_s[3], bd, 2), win_k, win_v)
```

```python
import functools
import math

import numpy as np
import jax
import jax.numpy as jnp
from jax import lax
from jax.experimental import pallas as pl
from jax.experimental.pallas import tpu as pltpu

PAGE_SIZE = 128
HEAD_DIM = 64
MOBA_BLOCK = 256
MOBA_TOPK = 3
CMP_LEN = 32
CMP_STRIDE = 16
CMP_HIDDEN = 256
SEL_BLOCK = 64
SEL_TOPN = 16
WINDOW = 512
N_BUCKETS = 32
MAX_DISTANCE = 1024
N_EXPERTS = 32
TOP_K = 4
SWIGLU_LIMIT = 7.0
SWIGLU_ALPHA = 1.702
RMS_EPS = 1e-6
NEG = -1e30
FORCE = 1e30
SCALE = HEAD_DIM ** -0.5
LANES = 128
VMEM_LIMIT = 56 * 1024 * 1024

F32 = jnp.float32
BF16 = jnp.bfloat16
HI = lax.Precision.HIGHEST
NT = (((1,), (1,)), ((), ()))


def _cparams(sem):
    return pltpu.CompilerParams(dimension_semantics=sem, vmem_limit_bytes=VMEM_LIMIT)


def _const_spec(shape):
    n = len(shape)
    return pl.BlockSpec(shape, lambda *_: (0,) * n)


def _t5_bucket(dist):
    max_exact = N_BUCKETS // 2
    d = jnp.maximum(dist, 0)
    df = jnp.maximum(d, max_exact).astype(jnp.float32)
    large = max_exact + (jnp.log(df / max_exact) / math.log(MAX_DISTANCE / max_exact)
                         * (N_BUCKETS - max_exact)).astype(jnp.int32)
    return jnp.where(d < max_exact, d, jnp.minimum(large, N_BUCKETS - 1))


def _topk_mask(score, valid, k, idx_iota, axis):
    n = float(score.shape[axis])
    low = -3e38
    pos_f = idx_iota.astype(F32)
    work = jnp.where(valid, score, low)
    sel = jnp.zeros(score.shape, F32)
    ids = []
    for _ in range(k):
        mx = jnp.max(work, axis=axis, keepdims=True)
        cand = (work == mx) & (mx > low)
        idx = jnp.min(jnp.where(cand, pos_f, n), axis=axis, keepdims=True)
        hit = pos_f == idx
        sel = jnp.where(hit, 1.0, sel)
        work = jnp.where(hit, low, work)
        ids.append(idx.astype(jnp.int32))
    return sel, ids


C_QM, C_KM, C_VM, C_QN, C_KV, C_GT, C_END = 0, 512, 1024, 1536, 2560, 3328, 3456


def _proj_weight(w_in):
    d = w_in.shape[0]
    qn = w_in[:, 1536:2048].reshape(d, 8, HEAD_DIM)
    z = jnp.zeros((d, 8, HEAD_DIM), w_in.dtype)
    grp0 = (jnp.arange(8) < 4)[None, :, None]
    qn_pad = jnp.concatenate([jnp.where(grp0, qn, z), jnp.where(grp0, z, qn)], axis=-1).reshape(d, 1024)
    gates = jnp.pad(w_in[:, 2816:2840], ((0, 0), (0, LANES - 24)))
    return jnp.concatenate([w_in[:, :1536], qn_pad, w_in[:, 2048:2816], gates], axis=1).astype(BF16)


def _proj_kernel(x_ref, g_ref, w_ref, qm_ref, km_ref, vm_ref, kmb_ref, vmb_ref, qn_ref,
                 kv_ref, kvb_ref, gt_ref, *rest, with_kmean):
    x = x_ref[...]
    h = x * lax.rsqrt(jnp.mean(x * x, axis=-1, keepdims=True) + RMS_EPS) * g_ref[...]
    z = jnp.dot(h.astype(BF16), w_ref[...], preferred_element_type=F32)
    qm_ref[...] = z[:, C_QM:C_KM]
    km = z[:, C_KM:C_VM]
    vm = z[:, C_VM:C_QN]
    km_ref[...] = km
    vm_ref[...] = vm
    kmb_ref[...] = km.astype(BF16)
    vmb_ref[...] = vm.astype(BF16)
    for hd in range(8):
        qn_ref[hd] = (z[:, C_QN + hd * LANES:C_QN + (hd + 1) * LANES] * SCALE).astype(BF16)
    for j in range(6):
        blk = z[:, C_KV + j * LANES:C_KV + (j + 1) * LANES]
        kv_ref[j] = blk
        kvb_ref[j] = blk.astype(BF16)
    gt_ref[...] = jax.nn.sigmoid(z[:, C_GT:C_END])
    if with_kmean:
        (kmean_ref,) = rest
        nblk = km.shape[0] // MOBA_BLOCK
        for i in range(nblk):
            kmean_ref[0, i:i + 1, :] = jnp.mean(km[i * MOBA_BLOCK:(i + 1) * MOBA_BLOCK], axis=0, keepdims=True)


def _project(x, g, w, tm, with_kmean):
    t, d = x.shape
    nt = t // tm
    row = lambda w_: pl.BlockSpec((tm, w_), lambda i: (i, 0))
    out_shape = [
        jax.ShapeDtypeStruct((t, 512), F32), jax.ShapeDtypeStruct((t, 512), F32), jax.ShapeDtypeStruct((t, 512), F32),
        jax.ShapeDtypeStruct((t, 512), BF16), jax.ShapeDtypeStruct((t, 512), BF16),
        jax.ShapeDtypeStruct((8, t, LANES), BF16),
        jax.ShapeDtypeStruct((6, t, LANES), F32), jax.ShapeDtypeStruct((6, t, LANES), BF16),
        jax.ShapeDtypeStruct((t, LANES), F32),
    ]
    out_specs = [row(512), row(512), row(512), row(512), row(512),
                 pl.BlockSpec((8, tm, LANES), lambda i: (0, i, 0)),
                 pl.BlockSpec((6, tm, LANES), lambda i: (0, i, 0)),
                 pl.BlockSpec((6, tm, LANES), lambda i: (0, i, 0)),
                 row(LANES)]
    if with_kmean:
        nb = tm // MOBA_BLOCK
        out_shape.append(jax.ShapeDtypeStruct((nt, nb, 512), F32))
        out_specs.append(pl.BlockSpec((1, nb, 512), lambda i: (i, 0, 0)))
    return pl.pallas_call(
        functools.partial(_proj_kernel, with_kmean=with_kmean),
        grid=(nt,),
        in_specs=[row(d), _const_spec((1, d)), _const_spec(w.shape)],
        out_specs=out_specs, out_shape=out_shape,
        compiler_params=_cparams(("parallel",)), name="proj",
    )(x, g, w)


def _toeplitz_kernel(v_ref, o_ref):
    rows, cols = o_ref.shape[1:]
    x = jnp.broadcast_to(v_ref[0], (rows, v_ref.shape[2]))
    o_ref[0] = pltpu.roll(x, 0, 1, stride=1, stride_axis=0)[:, :cols]


def _toeplitz_tiles(bv, rows, cols, n_tiles, head_major):
    length = rows + cols
    assert length % LANES == 0
    k = np.arange(length)
    off = np.where(k < cols, -k, length - k)
    dist = np.stack([rows * m + off for m in range(n_tiles)])
    v = jnp.take(bv, jnp.asarray(np.clip(dist, 0, MAX_DISTANCE)), axis=0)
    v = jnp.where(jnp.asarray(dist >= 0)[..., None], v, NEG)
    v = jnp.transpose(v, (2, 0, 1) if head_major else (0, 2, 1))
    lead = v.shape[:2]
    n = lead[0] * lead[1]
    tiles = pl.pallas_call(
        _toeplitz_kernel, grid=(n,),
        in_specs=[pl.BlockSpec((1, 1, length), lambda i: (i, 0, 0))],
        out_specs=pl.BlockSpec((1, rows, cols), lambda i: (i, 0, 0)),
        out_shape=jax.ShapeDtypeStruct((n, rows, cols), F32),
        compiler_params=_cparams(("parallel",)), name="bias_tiles",
    )(v.reshape(n, 1, length))
    return tiles.reshape(lead + (rows, cols))


TQ_M = MOBA_BLOCK


def _two_pass_attention(lo, hi, logits, values, rm_ref, l_ref, acc_ref):
    rm_ref[...] = jnp.full(rm_ref.shape, -3e38, F32)

    def pass1(jj, c):
        j0 = lo + 4 * jj
        m = rm_ref[...]
        for u in range(4):
            s = logits(jnp.minimum(j0 + u, hi))
            m = jnp.maximum(m, jnp.maximum(s[:, :LANES], s[:, LANES:]))
        rm_ref[...] = m
        return c

    lax.fori_loop(0, (hi - lo + 4) // 4, pass1, 0)
    rm_ref[...] = jnp.broadcast_to(jnp.max(rm_ref[...], axis=1, keepdims=True), rm_ref.shape)
    l_ref[...] = jnp.zeros(l_ref.shape, F32)
    acc_ref[...] = jnp.zeros(acc_ref.shape, F32)

    def probs(j):
        s = logits(j)
        m = rm_ref[...]
        pa = jnp.exp(s[:, :LANES] - m)
        pb = jnp.exp(s[:, LANES:] - m)
        pv = jnp.dot(jnp.concatenate([pa.astype(BF16), pb.astype(BF16)], axis=1), values(j),
                     preferred_element_type=F32)
        return pa + pb, pv

    def pass2(jj, c):
        j0 = lo + 4 * jj
        l_sum, pv_sum = probs(j0)
        for u in range(1, 4):
            l_u, pv_u = probs(jnp.minimum(j0 + u, hi))
            w = (j0 + u <= hi).astype(F32)
            l_sum = l_sum + w * l_u
            pv_sum = pv_sum + w * pv_u
        l_ref[...] += l_sum
        acc_ref[...] += pv_sum
        return c

    lax.fori_loop(0, (hi - lo + 4) // 4, pass2, 0)
    return acc_ref[...] / jnp.sum(l_ref[...], axis=1, keepdims=True)


def _moba_kernel(q_ref, k_ref, v_ref, kmean_ref, tb_ref, o_ref, rm_ref, l_ref, acc_ref, *, n_tb):
    qi = pl.program_id(2)
    q2 = q_ref[0]
    lane = lax.broadcasted_iota(jnp.int32, q2.shape, 1)
    q_st = jnp.concatenate([jnp.where(lane < HEAD_DIM, q2, 0.0), jnp.where(lane >= HEAD_DIM, q2, 0.0)], axis=0)
    rows = 2 * TQ_M
    gate_t = lax.dot_general(kmean_ref[0], q_st, NT, precision=HI, preferred_element_type=F32)
    blk = lax.broadcasted_iota(jnp.int32, gate_t.shape, 0)
    sel_t, _ = _topk_mask(gate_t, blk < qi, MOBA_TOPK, blk, 0)
    selneg = jnp.where((sel_t > 0.5) | (blk == qi), 0.0, NEG).T
    q_aug = jnp.concatenate([(q_st * SCALE).astype(BF16), selneg.astype(BF16)], axis=1)
    key_lane = lax.broadcasted_iota(jnp.int32, (MOBA_BLOCK, LANES), 1)

    def kv_rows(ref, j):
        return ref[0, pl.ds(pl.multiple_of(j * MOBA_BLOCK, MOBA_BLOCK), MOBA_BLOCK), :]

    def logits(j):
        k_aug = jnp.concatenate([kv_rows(k_ref, j), (key_lane == j).astype(BF16)], axis=1)
        s = lax.dot_general(q_aug, k_aug, NT, preferred_element_type=F32)
        m_id = jnp.minimum(qi - j, n_tb - 1)
        return s + jnp.concatenate([tb_ref[0, m_id], tb_ref[1, m_id]], axis=0)

    o = _two_pass_attention(0, qi, logits, lambda j: kv_rows(v_ref, j), rm_ref, l_ref, acc_ref)
    o_ref[0] = jnp.where(lane < HEAD_DIM, o[:TQ_M], o[TQ_M:]).astype(o_ref.dtype)


def _moba_prompt(qm, kmb, vmb, kmean, tb):
    b, t, _ = qm.shape
    nb = kmean.shape[1]
    assert t // MOBA_BLOCK <= nb == LANES
    n_tb = tb.shape[1]
    return pl.pallas_call(
        functools.partial(_moba_kernel, n_tb=n_tb),
        grid=(4, b, t // TQ_M),
        in_specs=[pl.BlockSpec((1, TQ_M, LANES), lambda p, bi, qi: (bi, qi, p)),
                  pl.BlockSpec((1, t, LANES), lambda p, bi, qi: (bi, 0, p)),
                  pl.BlockSpec((1, t, LANES), lambda p, bi, qi: (bi, 0, p)),
                  pl.BlockSpec((1, nb, LANES), lambda p, bi, qi: (bi, 0, p)),
                  pl.BlockSpec((2, n_tb, TQ_M, MOBA_BLOCK), lambda p, bi, qi: (p, 0, 0, 0))],
        out_specs=pl.BlockSpec((1, TQ_M, LANES), lambda p, bi, qi: (bi, qi, p)),
        out_shape=jax.ShapeDtypeStruct((b, t, 512), BF16),
        scratch_shapes=[pltpu.VMEM((2 * TQ_M, LANES), F32)] * 3,
        compiler_params=_cparams(("parallel", "parallel", "parallel")), name="moba_prompt",
    )(qm, kmb, vmb, kmean, tb)


CHUNKS_PER_PAGE = PAGE_SIZE // CMP_STRIDE
CMP_PAGES = 32


def _chunk_layout(pages_t):
    n = pages_t.shape[0]
    x = pages_t.reshape(n, 2, HEAD_DIM, CHUNKS_PER_PAGE, CMP_STRIDE)
    return jnp.transpose(x, (0, 1, 3, 2, 4)).reshape(n, 2 * CHUNKS_PER_PAGE, HEAD_DIM * CMP_STRIDE).astype(BF16)


def _chunk_weight(w_half):
    h = w_half.shape[1]
    return jnp.transpose(w_half.reshape(CMP_STRIDE, HEAD_DIM, h), (1, 0, 2)).reshape(CMP_STRIDE * HEAD_DIM, h)


def _cmp_const_kernel(pe_ref, w1_ref, b1_ref, o_ref):
    o_ref[...] = jnp.sum(w1_ref[...] * pe_ref[...], axis=0, keepdims=True) + b1_ref[...]


def _cmp_const(pe, w1, b1):
    return pl.pallas_call(
        _cmp_const_kernel, out_shape=jax.ShapeDtypeStruct((1, CMP_HIDDEN), F32), name="cmp_const",
    )(pe.reshape(CMP_LEN * HEAD_DIM, 1), w1, b1.reshape(1, CMP_HIDDEN))


def _gelu_tanh(x):
    return 0.5 * x * (1.0 + jnp.tanh(math.sqrt(2.0 / math.pi) * (x + 0.044715 * x * x * x)))


def _compress_kernel(pt_ref, *refs, n_pg):
    x_refs = refs[:n_pg + 1]
    wab_ref, c_ref, w2_ref, o_ref = refs[n_pg + 1:]
    rows_pp = 2 * CHUNKS_PER_PAGE
    x = jnp.concatenate([r[0] for r in x_refs], axis=0)
    ab = jnp.dot(x, wab_ref[...], preferred_element_type=F32)
    a = ab[:n_pg * rows_pp, :CMP_HIDDEN]
    bm = ab[:, CMP_HIDDEN:]
    n = bm.shape[0]
    nxt = pltpu.roll(bm, n - 1, 0)
    nxt_page = pltpu.roll(bm, n - (rows_pp - CHUNKS_PER_PAGE + 1), 0)
    c_id = lax.broadcasted_iota(jnp.int32, bm.shape, 0) % CHUNKS_PER_PAGE
    b_next = jnp.where(c_id == CHUNKS_PER_PAGE - 1, nxt_page, nxt)[:n_pg * rows_pp]
    hid = _gelu_tanh(a + b_next + c_ref[...]).astype(BF16)
    hid = hid.reshape(n_pg, 2, CHUNKS_PER_PAGE, CMP_HIDDEN)
    out = None
    for g in range(2):
        hg = hid[:, g].reshape(n_pg * CHUNKS_PER_PAGE, CMP_HIDDEN)
        og = jnp.dot(hg, w2_ref[g], preferred_element_type=F32)
        out = og if out is None else out + og
    o_ref[0] = out.astype(o_ref.dtype)


def _compress(xc, page_table, w1, const, w2):
    b, n_pages = page_table.shape
    n_pg = math.gcd(n_pages, CMP_PAGES)
    half = CMP_STRIDE * HEAD_DIM
    wab = jnp.concatenate([_chunk_weight(w1[:half]), _chunk_weight(w1[half:])], axis=1).astype(BF16)
    z = jnp.zeros_like(w2)
    w2p = jnp.stack([jnp.concatenate([w2, z], axis=1), jnp.concatenate([z, w2], axis=1)]).astype(BF16)

    def page_spec(k):
        def imap(bi, i, pt):
            return (pt[bi, jnp.minimum(i * n_pg + k, n_pages - 1)], 0, 0)
        return pl.BlockSpec((1, 2 * CHUNKS_PER_PAGE, half), imap)

    tile = n_pg * CHUNKS_PER_PAGE
    grid_spec = pltpu.PrefetchScalarGridSpec(
        num_scalar_prefetch=1, grid=(b, n_pages // n_pg),
        in_specs=[page_spec(k) for k in range(n_pg + 1)] + [
            pl.BlockSpec(wab.shape, lambda bi, i, pt: (0, 0)),
            pl.BlockSpec(const.shape, lambda bi, i, pt: (0, 0)),
            pl.BlockSpec(w2p.shape, lambda bi, i, pt: (0, 0, 0))],
        out_specs=pl.BlockSpec((1, tile, LANES), lambda bi, i, pt: (bi, i, 0)))
    return pl.pallas_call(
        functools.partial(_compress_kernel, n_pg=n_pg), grid_spec=grid_spec,
        out_shape=jax.ShapeDtypeStruct((b, n_pages * CHUNKS_PER_PAGE, LANES), BF16),
        compiler_params=_cparams(("parallel", "parallel")), name="compress",
    )(page_table, *([xc] * (n_pg + 1)), wab, const, w2p)


TQ_N = 128
TK_N = 256
NSA_HEADS = 8
assert WINDOW % TK_N == 0 and TK_N % TQ_N == 0 and TK_N == 2 * LANES == MOBA_BLOCK
WIN_TILES = WINDOW // TK_N + 1


def _overlap_matrix(n_cmp_pad, n_sel_pad):
    cstart = np.arange(n_cmp_pad)[:, None] * CMP_STRIDE
    sstart = np.arange(n_sel_pad)[None, :] * SEL_BLOCK
    return ((cstart < sstart + SEL_BLOCK) & (cstart + CMP_LEN > sstart)).astype(np.float32)


def _nsa_kernel(q_ref, gt_ref, kc_ref, vc_ref, ks_ref, vs_ref, kw_ref, vw_ref, ovt_ref, tb_ref, wm_ref, o_ref,
                rm_ref, l_ref, acc_ref, *, n_cmp, n_sel, n_tb):
    qi = pl.program_id(1)
    q0 = qi * TQ_N
    rows = NSA_HEADS * TQ_N
    q8 = q_ref[...].reshape(rows, LANES)
    n_cp = kc_ref.shape[1]
    pos = q0 + lax.broadcasted_iota(jnp.int32, (1, TQ_N, 1), 1)

    lc = lax.dot_general(q8, kc_ref[0], NT, preferred_element_type=F32).reshape(NSA_HEADS, TQ_N, n_cp)
    n_id = lax.broadcasted_iota(jnp.int32, (1, 1, n_cp), 2)
    ok = (n_id * CMP_STRIDE + (CMP_LEN - 1) <= pos) & (n_id < n_cmp)
    mc = jnp.max(jnp.where(ok, lc, NEG), axis=2, keepdims=True)
    pc = jnp.where(ok, jnp.exp(lc - mc), 0.0)
    lsum = jnp.sum(pc, axis=2, keepdims=True)
    pc = pc / jnp.where(lsum > 0.0, lsum, 1.0)
    oc = jnp.dot(pc.reshape(rows, n_cp).astype(BF16), vc_ref[0], preferred_element_type=F32)

    pcg = pc.reshape(2, 4, TQ_N, n_cp)
    pcg = (pcg[:, 0] + pcg[:, 1]) + (pcg[:, 2] + pcg[:, 3])
    imp_t = lax.dot_general(ovt_ref[...], pcg.reshape(2 * TQ_N, n_cp), NT, precision=HI, preferred_element_type=F32)
    n_sp = imp_t.shape[0]
    sid = lax.broadcasted_iota(jnp.int32, imp_t.shape, 0)
    pos_t = q0 + lax.broadcasted_iota(jnp.int32, imp_t.shape, 1) % TQ_N
    cur = pos_t // SEL_BLOCK
    forced = (sid == 0) | (sid == cur) | (sid == cur - 1)
    valid = (sid * SEL_BLOCK <= pos_t) & (sid < n_sel)
    sel_t, _ = _topk_mask(jnp.where(forced, FORCE, imp_t), valid, min(SEL_TOPN, n_sel), sid, 0)
    selneg = jnp.where(sel_t > 0.5, 0.0, NEG).T.astype(BF16).reshape(2, TQ_N, n_sp)
    selneg8 = jnp.broadcast_to(selneg[:, None], (2, 4, TQ_N, n_sp)).reshape(rows, n_sp)
    q_aug = jnp.concatenate([q8, selneg8], axis=1)

    jmax = q0 // TK_N
    e_lane = lax.broadcasted_iota(jnp.int32, (TK_N, n_sp), 1)
    e_blk = lax.broadcasted_iota(jnp.int32, (TK_N, n_sp), 0) // SEL_BLOCK

    def bias(j):
        return tb_ref[jnp.minimum(qi - 2 * j, n_tb - 1)].reshape(rows, TK_N)

    def kslice(ref, j):
        return ref[0, pl.ds(pl.multiple_of(j * TK_N, TK_N), TK_N), :]

    def sel_logits(j):
        k_aug = jnp.concatenate([kslice(ks_ref, j), (e_lane == e_blk + j * (TK_N // SEL_BLOCK)).astype(BF16)], axis=1)
        return lax.dot_general(q_aug, k_aug, NT, preferred_element_type=F32) + bias(j)

    osel = _two_pass_attention(0, jmax, sel_logits, lambda j: kslice(vs_ref, j), rm_ref, l_ref, acc_ref)

    def win_logits(j):
        s = lax.dot_general(q8, kslice(kw_ref, j), NT, preferred_element_type=F32) + bias(j)
        wmask = wm_ref[jnp.minimum(qi - 2 * j, wm_ref.shape[0] - 1)]
        return (s.reshape(NSA_HEADS, TQ_N, TK_N) + wmask[None]).reshape(rows, TK_N)

    jlo = jnp.maximum(jmax - (WIN_TILES - 1), 0)
    ow = _two_pass_attention(jlo, jmax, win_logits, lambda j: kslice(vw_ref, j), rm_ref, l_ref, acc_ref)

    gt = gt_ref[...]
    glane = lax.broadcasted_iota(jnp.int32, gt.shape, 1)

    def gate(hd, br):
        return jnp.sum(jnp.where(glane == hd * 3 + br, gt, 0.0), axis=1, keepdims=True)

    for hd in range(NSA_HEADS):
        r0 = hd * TQ_N
        o = (gate(hd, 0) * oc[r0:r0 + TQ_N] + gate(hd, 1) * osel[r0:r0 + TQ_N] + gate(hd, 2) * ow[r0:r0 + TQ_N])
        o_ref[hd] = o.astype(o_ref.dtype)


def _nsa_prompt(qn8, gates, kcmp, vcmp, ksb, vsb, kwb, vwb, tbn, wm, n_cmp):
    b, t, _ = ksb.shape
    n_cp = kcmp.shape[1]
    n_sel = t // SEL_BLOCK
    n_sp = -(-n_sel // LANES) * LANES
    ov = jnp.asarray(_overlap_matrix(n_cp, n_sp).T)
    n_tb = tbn.shape[0]
    nq = t // TQ_N
    full = lambda a: pl.BlockSpec((1,) + a.shape[1:], lambda bi, qi: (bi,) + (0,) * (a.ndim - 1))
    once = lambda a: pl.BlockSpec(a.shape, lambda bi, qi: (0,) * a.ndim, pipeline_mode=pl.Buffered(1))
    return pl.pallas_call(
        functools.partial(_nsa_kernel, n_cmp=n_cmp, n_sel=n_sel, n_tb=n_tb),
        grid=(b, nq),
        in_specs=[pl.BlockSpec((NSA_HEADS, TQ_N, LANES), lambda bi, qi: (0, bi * nq + qi, 0)),
                  pl.BlockSpec((TQ_N, LANES), lambda bi, qi: (bi * nq + qi, 0)),
                  full(kcmp), full(vcmp), full(ksb), full(vsb), full(kwb), full(vwb),
                  once(ov), once(tbn), once(wm)],
        out_specs=pl.BlockSpec((NSA_HEADS, TQ_N, LANES), lambda bi, qi: (0, bi * nq + qi, 0)),
        out_shape=jax.ShapeDtypeStruct((NSA_HEADS, b * t, LANES), BF16),
        scratch_shapes=[pltpu.VMEM((NSA_HEADS * TQ_N, LANES), F32)] * 3,
        compiler_params=_cparams(("parallel", "parallel")), name="nsa_prompt",
    )(qn8, gates, kcmp, vcmp, ksb, vsb, kwb, vwb, ov, tbn, wm)


def _mix_kernel(x_ref, om_ref, on_ref, wom_ref, won_ref, g_ref, wr_ref, br_ref, c0_ref,
                r_ref, h_ref, ti_ref, tg_ref, rk_ref, cnt_ref, carry):
    @pl.when(pl.program_id(0) == 0)
    def _():
        carry[...] = c0_ref[...]

    acc = x_ref[...] + jnp.dot(om_ref[...], wom_ref[...], preferred_element_type=F32)
    for hd in range(NSA_HEADS):
        acc = acc + jnp.dot(on_ref[hd], won_ref[hd], preferred_element_type=F32)
    r_ref[...] = acc
    h = acc * lax.rsqrt(jnp.mean(acc * acc, axis=-1, keepdims=True) + RMS_EPS) * g_ref[...]
    h_ref[...] = h
    logits = jnp.dot(h, wr_ref[...], precision=HI, preferred_element_type=F32) + br_ref[...]
    e_id = lax.broadcasted_iota(jnp.int32, logits.shape, 1)
    work = logits
    vals, ids = [], []
    for _ in range(TOP_K):
        mx = jnp.max(work, axis=1, keepdims=True)
        idx = jnp.min(jnp.where(work == mx, e_id, N_EXPERTS), axis=1, keepdims=True)
        work = jnp.where(e_id == idx, -3e38, work)
        vals.append(mx)
        ids.append(idx)
    ex = [jnp.exp(v - vals[0]) for v in vals]
    tot = ex[0]
    for e in ex[1:]:
        tot = tot + e
    tm = logits.shape[0]
    onehot = [(e_id == ids[k]).astype(F32) for k in range(TOP_K)]
    oh_all = (onehot[0] + onehot[1]) + (onehot[2] + onehot[3])
    earlier = (lax.broadcasted_iota(jnp.int32, (tm, tm), 1) < lax.broadcasted_iota(jnp.int32, (tm, tm), 0))
    before = jnp.dot(earlier.astype(BF16), oh_all.astype(BF16), preferred_element_type=F32) + carry[...]
    k_id = lax.broadcasted_iota(jnp.int32, ti_ref.shape, 1)
    ti = jnp.zeros(ti_ref.shape, jnp.int32)
    rk = jnp.zeros(rk_ref.shape, jnp.int32)
    tg = jnp.zeros(tg_ref.shape, F32)
    for k in range(TOP_K):
        rank_k = jnp.sum(onehot[k] * before, axis=1, keepdims=True)
        ti = jnp.where(k_id == k, ids[k], ti)
        rk = jnp.where(k_id == k, rank_k.astype(jnp.int32), rk)
        tg = jnp.where(k_id == k, ex[k] / tot, tg)
        before = before + onehot[k]
    ti_ref[...] = ti
    rk_ref[...] = rk
    tg_ref[...] = tg
    carry[...] += jnp.sum(oh_all, axis=0, keepdims=True)
    cnt_ref[...] = carry[...]


def _mix(x, om, on8, wom, won8, g_ffn, w_router, b_router, counts0, tm):
    t, d = x.shape
    row = lambda w_: pl.BlockSpec((tm, w_), lambda i: (i, 0))
    return pl.pallas_call(
        _mix_kernel, grid=(t // tm,),
        in_specs=[row(d), row(512), pl.BlockSpec((NSA_HEADS, tm, LANES), lambda i: (0, i, 0)),
                  _const_spec(wom.shape), _const_spec(won8.shape), _const_spec((1, d)),
                  _const_spec(w_router.shape), _const_spec((1, N_EXPERTS)), _const_spec((1, N_EXPERTS))],
        out_specs=[row(d), row(d), row(TOP_K), row(TOP_K), row(TOP_K), _const_spec((1, N_EXPERTS))],
        out_shape=[jax.ShapeDtypeStruct((t, d), F32), jax.ShapeDtypeStruct((t, d), F32),
                   jax.ShapeDtypeStruct((t, TOP_K), jnp.int32), jax.ShapeDtypeStruct((t, TOP_K), F32),
                   jax.ShapeDtypeStruct((t, TOP_K), jnp.int32), jax.ShapeDtypeStruct((1, N_EXPERTS), F32)],
        scratch_shapes=[pltpu.VMEM((1, N_EXPERTS), F32)],
        compiler_params=_cparams(("arbitrary",)), name="mix_router",
    )(x, om, on8, wom, won8, g_ffn.reshape(1, d), w_router, b_router.reshape(1, N_EXPERTS), counts0)


TM_E = 256


def _route(top_i, rank, counts, tm):
    n_assign = top_i.shape[0] * TOP_K
    cnt = counts.reshape(N_EXPERTS).astype(jnp.int32)
    padded = (cnt + tm - 1) // tm * tm
    pad_end = jnp.cumsum(padded)
    dest = (jnp.take(pad_end - padded, top_i, axis=0) + rank).reshape(n_assign)
    n_blocks = -(-n_assign // tm) + N_EXPERTS
    first_row = jnp.arange(n_blocks, dtype=jnp.int32)[:, None] * tm
    blk_e = jnp.minimum(jnp.sum((pad_end[None, :] <= first_row).astype(jnp.int32), axis=1), N_EXPERTS - 1)
    n_used = (pad_end[-1] // tm).astype(jnp.int32).reshape(1)
    return dest.astype(jnp.int32), blk_e.astype(jnp.int32), n_used, n_blocks


DISPATCH_UNROLL = 8


def _dispatch_kernel(d_ref, h_ref, xs_in, xs_out, sem):
    del xs_in
    tm = h_ref.shape[0]
    unroll = math.gcd(tm, DISPATCH_UNROLL)

    def body(c, carry):
        for u in range(unroll):
            r = c * unroll + u
            for k in range(TOP_K):
                pltpu.make_async_copy(h_ref.at[pl.ds(r, 1), :], xs_out.at[pl.ds(d_ref[r * TOP_K + k], 1), :], sem).start()
        return carry

    lax.fori_loop(0, tm // unroll, body, 0)
    for k in range(TOP_K):
        pltpu.make_async_copy(h_ref, xs_out.at[pl.ds(0, tm), :], sem).wait()


def _dispatch(h, dest, xs, tm):
    t, d = h.shape
    assert t % tm == 0
    return pl.pallas_call(
        _dispatch_kernel, grid=(t // tm,),
        in_specs=[pl.BlockSpec((tm * TOP_K,), lambda i: (i,), memory_space=pltpu.SMEM),
                  pl.BlockSpec((tm, d), lambda i: (i, 0)),
                  pl.BlockSpec(memory_space=pl.ANY)],
        out_specs=pl.BlockSpec(memory_space=pl.ANY),
        out_shape=jax.ShapeDtypeStruct(xs.shape, xs.dtype),
        scratch_shapes=[pltpu.SemaphoreType.DMA(())],
        input_output_aliases={2: 0},
        compiler_params=_cparams(("arbitrary",)), name="moe_dispatch",
    )(dest, h, xs)


def _expert_kernel(blk_e_ref, n_used_ref, x_ref, wgu_ref, bgu_ref, wd_ref, bd_ref, y_ref):
    i = pl.program_id(0)

    @pl.when(i < n_used_ref[0])
    def _():
        gu = jnp.dot(x_ref[...].astype(BF16), wgu_ref[0], preferred_element_type=F32) + bgu_ref[0]
        glu = jnp.minimum(gu, SWIGLU_LIMIT)
        lin = jnp.clip(gu, -SWIGLU_LIMIT, SWIGLU_LIMIT) + 1.0
        lin_next = pltpu.roll(lin, lin.shape[1] - 1, 1)
        act = glu * jax.nn.sigmoid(SWIGLU_ALPHA * glu) * lin_next
        y_ref[...] = jnp.dot(act.astype(BF16), wd_ref[0], preferred_element_type=F32) + bd_ref[0]

    @pl.when(i >= n_used_ref[0])
    def _():
        y_ref[...] = jnp.zeros_like(y_ref)


def _experts(xs, blk_e, n_used, wgu, bgu, wd, bd):
    n_rows, d = xs.shape
    dff2 = wgu.shape[2]
    tm = TM_E
    e_map = lambda i, be, nu: (be[i], 0, 0)
    grid_spec = pltpu.PrefetchScalarGridSpec(
        num_scalar_prefetch=2, grid=(n_rows // tm,),
        in_specs=[pl.BlockSpec((tm, d), lambda i, be, nu: (i, 0)),
                  pl.BlockSpec((1, d, dff2), e_map), pl.BlockSpec((1, 1, dff2), e_map),
                  pl.BlockSpec((1, dff2, d), e_map), pl.BlockSpec((1, 1, d), e_map)],
        out_specs=pl.BlockSpec((tm, d), lambda i, be, nu: (i, 0)))
    return pl.pallas_call(
        _expert_kernel, grid_spec=grid_spec,
        out_shape=jax.ShapeDtypeStruct((n_rows, d), F32),
        compiler_params=_cparams(("parallel",)), name="moe_experts",
    )(blk_e, n_used, xs, wgu, bgu, wd, bd)


def _combine_kernel(d_cur_ref, d_nxt_ref, y_hbm, r_ref, tg_ref, p_ref, wpg_ref, wpp_ref, gf_ref, o_ref, ybuf, sem):
    i = pl.program_id(0)
    n = pl.num_programs(0)
    tm = ybuf.shape[2]

    def start_gather(d_ref, slot):
        for r in range(tm):
            for k in range(TOP_K):
                pltpu.make_async_copy(y_hbm.at[pl.ds(d_ref[r * TOP_K + k], 1), :],
                                      ybuf.at[slot, k, pl.ds(r, 1), :], sem.at[slot]).start()

    def wait_gather(slot):
        for k in range(TOP_K):
            pltpu.make_async_copy(y_hbm.at[pl.ds(0, tm), :], ybuf.at[slot, k], sem.at[slot]).wait()

    @pl.when(i == 0)
    def _():
        start_gather(d_cur_ref, 0)

    def step(slot):
        wait_gather(slot)
        start_gather(d_nxt_ref, 1 - slot)
        tg = tg_ref[...]
        r = r_ref[...]
        for k in range(TOP_K):
            r = r + tg[:, k:k + 1] * ybuf[slot, k]
        gate = jax.nn.sigmoid(jnp.dot(r.astype(BF16), wpg_ref[...], preferred_element_type=F32))
        r = r + gate * jnp.dot(p_ref[...].astype(BF16), wpp_ref[...], preferred_element_type=F32)
        o_ref[...] = r * lax.rsqrt(jnp.mean(r * r, axis=-1, keepdims=True) + RMS_EPS) * gf_ref[...]

        @pl.when(i == n - 1)
        def _():
            wait_gather(1 - slot)

    for parity in range(2):
        pl.when(i % 2 == parity)(functools.partial(step, parity))


def _combine(y, dest, r1, tg, p, wpg, wpp, g_final, tm):
    t, d = r1.shape
    nt = t // tm
    row = lambda w_: pl.BlockSpec((tm, w_), lambda i: (i, 0))
    return pl.pallas_call(
        _combine_kernel, grid=(nt,),
        in_specs=[pl.BlockSpec((tm * TOP_K,), lambda i: (i,), memory_space=pltpu.SMEM),
                  pl.BlockSpec((tm * TOP_K,), lambda i: (jnp.minimum(i + 1, nt - 1),), memory_space=pltpu.SMEM),
                  pl.BlockSpec(memory_space=pl.ANY),
                  row(d), row(TOP_K), row(p.shape[1]),
                  _const_spec(wpg.shape), _const_spec(wpp.shape), _const_spec((1, d))],
        out_specs=row(d),
        out_shape=jax.ShapeDtypeStruct((t, d), F32),
        scratch_shapes=[pltpu.VMEM((2, TOP_K, tm, d), F32), pltpu.SemaphoreType.DMA((2,))],
        compiler_params=_cparams(("arbitrary",)), name="moe_combine",
    )(dest, dest, y, r1, tg, p, wpg, wpp, g_final.reshape(1, d))


def _ffn(groups, wts):
    mixed, counts = [], jnp.zeros((1, N_EXPERTS), F32)
    for x, om, on8, _, tm in groups:
        *outs, counts = _mix(x, om, on8, wts["wom"], wts["won8"], wts["g_ffn"], wts["w_router"], wts["b_router"],
                             counts, tm)
        mixed.append(outs)
    dest, blk_e, n_used, n_blocks = _route(jnp.concatenate([m[2] for m in mixed], axis=0),
                                           jnp.concatenate([m[4] for m in mixed], axis=0), counts, TM_E)
    xs = jnp.zeros((n_blocks * TM_E, groups[0][0].shape[1]), F32)
    dests, start = [], 0
    for (x, _, _, _, tm), m in zip(groups, mixed):
        n = x.shape[0] * TOP_K
        dests.append(dest[start:start + n])
        xs = _dispatch(m[1], dests[-1], xs, min(2 * tm, x.shape[0]))
        start += n
    y = _experts(xs, blk_e, n_used, wts["wgu"], wts["bgu"], wts["wd"], wts["bd"])
    return [_combine(y, d_g, m[0], m[3], p, wts["wpg"], wts["wpp"], wts["g_final"], tm)
            for (_, _, _, p, tm), m, d_g in zip(groups, mixed, dests)]


KMEAN_PAGES = 16
PAGES_PER_MOBA = MOBA_BLOCK // PAGE_SIZE
SEL_PER_PAGE = PAGE_SIZE // SEL_BLOCK


def _kmean_kernel(pt_ref, *refs):
    x_refs, o_ref = refs[:KMEAN_PAGES], refs[KMEAN_PAGES]
    i = pl.program_id(1)
    nb_step = KMEAN_PAGES // PAGES_PER_MOBA

    @pl.when(i == 0)
    def _():
        o_ref[...] = jnp.zeros_like(o_ref)

    blk_lane = lax.broadcasted_iota(jnp.int32, o_ref.shape, 2)
    acc = o_ref[...]
    for blk in range(nb_step):
        s = x_refs[blk * PAGES_PER_MOBA][...]
        for k in range(1, PAGES_PER_MOBA):
            s = s + x_refs[blk * PAGES_PER_MOBA + k][...]
        col = jnp.sum(s, axis=-1, keepdims=True) * (1.0 / MOBA_BLOCK)
        acc = jnp.where(blk_lane == i * nb_step + blk, col, acc)
    o_ref[...] = acc


def _moba_kmean(cache_t, page_table):
    b, n_pages = page_table.shape
    assert n_pages % KMEAN_PAGES == 0
    _, _, h, d, ps = cache_t.shape

    def page_spec(k):
        return pl.BlockSpec((None, None, h, d, ps), lambda bi, i, pt: (0, pt[bi, i * KMEAN_PAGES + k], 0, 0, 0))

    n_blk = n_pages // PAGES_PER_MOBA
    grid_spec = pltpu.PrefetchScalarGridSpec(
        num_scalar_prefetch=1, grid=(b, n_pages // KMEAN_PAGES),
        in_specs=[page_spec(k) for k in range(KMEAN_PAGES)],
        out_specs=pl.BlockSpec((None, h, d, n_blk), lambda bi, i, pt: (bi, 0, 0, 0)))
    return pl.pallas_call(
        _kmean_kernel, grid_spec=grid_spec,
        out_shape=jax.ShapeDtypeStruct((b, h, d, n_blk), F32),
        compiler_params=_cparams(("parallel", "arbitrary")), name="moba_kmean",
    )(page_table, *([cache_t] * KMEAN_PAGES))


def _dec_select_kernel(qm_ref, kmean_ref, qn_ref, kc_ref, vc_ref, mid_ref, oc_ref, pcg_ref, *, n_cmp):
    for hd in range(8):
        gate = jnp.sum(kmean_ref[hd] * qm_ref[hd], axis=0, keepdims=True)
        blk = lax.broadcasted_iota(jnp.int32, gate.shape, 1)
        _, ids = _topk_mask(gate, blk >= 0, MOBA_TOPK, blk, 1)
        for k in range(MOBA_TOPK):
            mid_ref[k:k + 1, hd:hd + 1] = ids[k]
    lc = lax.dot_general(qn_ref[...], kc_ref[...], NT, preferred_element_type=F32)
    ok = lax.broadcasted_iota(jnp.int32, lc.shape, 1) < n_cmp
    mc = jnp.max(jnp.where(ok, lc, NEG), axis=1, keepdims=True)
    pc = jnp.where(ok, jnp.exp(lc - mc), 0.0)
    lsum = jnp.sum(pc, axis=1, keepdims=True)
    pc = pc / jnp.where(lsum > 0.0, lsum, 1.0)
    oc_ref[...] = jnp.dot(pc.astype(BF16), vc_ref[...], preferred_element_type=F32)
    for g in range(2):
        pcg_ref[g:g + 1, :] = jnp.sum(pc[4 * g:4 * g + 4], axis=0, keepdims=True)


def _dec_select(qm_col, kmean_t, qn8, kcmp, vcmp, n_cmp):
    b, _, _, n_blk = kmean_t.shape
    n_cp = kcmp.shape[1]
    per_b = lambda *s: pl.BlockSpec((None,) + s, lambda bi: (bi,) + (0,) * len(s))
    return pl.pallas_call(
        functools.partial(_dec_select_kernel, n_cmp=n_cmp), grid=(b,),
        in_specs=[per_b(8, HEAD_DIM, 1), per_b(8, HEAD_DIM, n_blk), per_b(8, LANES), per_b(n_cp, LANES),
                  per_b(n_cp, LANES)],
        out_specs=[per_b(MOBA_TOPK, 8), per_b(8, LANES), per_b(2, n_cp)],
        out_shape=[jax.ShapeDtypeStruct((b, MOBA_TOPK, 8), jnp.int32), jax.ShapeDtypeStruct((b, 8, LANES), F32),
                   jax.ShapeDtypeStruct((b, 2, n_cp), F32)],
        compiler_params=_cparams(("parallel",)), name="dec_select",
    )(qm_col, kmean_t, qn8, kcmp, vcmp)


def _dec_topn_kernel(pcg_ref, ov_ref, o_ref, *, pos, n_sel):
    imp = jnp.dot(pcg_ref[...], ov_ref[...], precision=HI, preferred_element_type=F32)
    sid = lax.broadcasted_iota(jnp.int32, imp.shape, 1)
    cur = pos // SEL_BLOCK
    forced = (sid == 0) | (sid == cur) | (sid == cur - 1)
    valid = (sid * SEL_BLOCK <= pos) & (sid < n_sel)
    n_top = min(SEL_TOPN, n_sel)
    _, ids = _topk_mask(jnp.where(forced, FORCE, imp), valid, n_top, sid, 1)
    k_id = lax.broadcasted_iota(jnp.int32, o_ref.shape, 1)
    out = jnp.full(o_ref.shape, imp.shape[1], jnp.int32)
    for k in range(n_top):
        out = jnp.where(k_id == k, ids[k], out)
    o_ref[...] = out


def _dec_topn(pcg, pos, n_sel):
    rows, n_cp = pcg.shape
    n_sp = -(-n_sel // LANES) * LANES
    ov = jnp.asarray(_overlap_matrix(n_cp, n_sp))
    return pl.pallas_call(
        functools.partial(_dec_topn_kernel, pos=pos, n_sel=n_sel),
        out_shape=jax.ShapeDtypeStruct((rows, SEL_TOPN), jnp.int32), name="dec_topn",
        compiler_params=pltpu.CompilerParams(vmem_limit_bytes=VMEM_LIMIT),
    )(pcg, ov)


def _attend_t(s_self, v_self, logit_tiles, value_tiles_t):
    m = s_self
    for s in logit_tiles:
        m = jnp.maximum(m, jnp.max(s, axis=1, keepdims=True))
    p_self = jnp.exp(s_self - m)
    l = p_self
    acc = p_self * v_self
    for s, vt in zip(logit_tiles, value_tiles_t):
        p = jnp.exp(s - m)
        l = l + jnp.sum(p, axis=1, keepdims=True)
        acc = acc + lax.dot_general(p.astype(BF16), vt, NT, preferred_element_type=F32)
    return acc / l


def _dec_attn_kernel(pt_ref, mid_ref, sid_ref, qm_ref, kmn_ref, vmn_ref, qn_ref, new_ref, gt_ref, oc_ref,
                     wk_ref, wv_ref, pbm_ref, pbn_ref, wb_ref, b0_ref,
                     mk_hbm, mv_hbm, sk_hbm, sv_hbm, om_ref, on_ref,
                     kmbuf, vmbuf, ksbuf, vsbuf, sem, *, n_blk, n_selp):
    bi = pl.program_id(0)
    n_top = ksbuf.shape[1]

    def moba_copies(hd, k):
        blk = jnp.minimum(mid_ref[bi, k * 8 + hd], n_blk - 1)
        out = []
        for half in range(PAGES_PER_MOBA):
            page = pt_ref[bi, blk * PAGES_PER_MOBA + half]
            out.append(pltpu.make_async_copy(mk_hbm.at[0, page, hd], kmbuf.at[hd, k, half], sem.at[0]))
            out.append(pltpu.make_async_copy(mv_hbm.at[0, page, hd], vmbuf.at[hd, k, half], sem.at[0]))
        return out

    def sel_copies(g, k):
        s = jnp.minimum(sid_ref[bi, g * n_top + k], n_selp - 1)
        page = pt_ref[bi, s // SEL_PER_PAGE]
        return [pltpu.make_async_copy(sk_hbm.at[0, page, g], ksbuf.at[g, k], sem.at[1]),
                pltpu.make_async_copy(sv_hbm.at[0, page, g], vsbuf.at[g, k], sem.at[1])]

    copies = [c for hd in range(8) for k in range(MOBA_TOPK) for c in moba_copies(hd, k)]
    copies += [c for g in range(2) for k in range(n_top) for c in sel_copies(g, k)]
    for c in copies:
        c.start()
    for c in copies:
        c.wait()

    b0 = b0_ref[...]

    for hd in range(8):
        qf = qm_ref[hd:hd + 1, :] * SCALE
        qh = qf.astype(BF16)
        s_self = jnp.sum(qf * kmn_ref[hd:hd + 1, :], axis=1, keepdims=True) + b0[hd:hd + 1]
        ss, vv = [], []
        for k in range(MOBA_TOPK):
            raw = mid_ref[bi, k * 8 + hd]
            blk = jnp.minimum(raw, n_blk - 1)
            for half in range(PAGES_PER_MOBA):
                s = jnp.dot(qh, kmbuf[hd, k, half].astype(BF16), preferred_element_type=F32)
                s = s + pbm_ref[blk][hd:hd + 1, half * PAGE_SIZE:(half + 1) * PAGE_SIZE]
                ss.append(jnp.where(raw < n_blk, s, NEG))
                vv.append(vmbuf[hd, k, half].astype(BF16))
        om_ref[hd:hd + 1, :] = _attend_t(s_self, vmn_ref[hd:hd + 1, :], ss, vv)

    gt = gt_ref[...]
    for g in range(2):
        lanes = slice(g * HEAD_DIM, (g + 1) * HEAD_DIM)
        hs = slice(4 * g, 4 * g + 4)
        qh = qn_ref[hs, lanes]
        qf = qh.astype(F32)
        ks_new, vs_new = new_ref[0:1, lanes], new_ref[1:2, lanes]
        kw_new, vw_new = new_ref[2:3, lanes], new_ref[3:4, lanes]

        b0g = b0[8 + 4 * g:12 + 4 * g]
        row_half = lax.broadcasted_iota(jnp.int32, (4, PAGE_SIZE), 1) // SEL_BLOCK
        ss, vv = [], []
        for k in range(n_top):
            raw = sid_ref[bi, g * n_top + k]
            sblk = jnp.minimum(raw, n_selp - 1)
            s = jnp.dot(qh, ksbuf[g, k].astype(BF16), preferred_element_type=F32)
            s = s + pbn_ref[sblk // SEL_PER_PAGE][hs, :]
            ss.append(jnp.where((row_half == sblk % SEL_PER_PAGE) & (raw < n_selp), s, NEG))
            vv.append(vsbuf[g, k].astype(BF16))
        s_self = jnp.sum(qf * ks_new, axis=1, keepdims=True) + b0g
        osel = _attend_t(s_self, vs_new, ss, vv)

        sw = jnp.dot(qh, wk_ref[g].astype(BF16), preferred_element_type=F32) + wb_ref[hs, :]
        s_self = jnp.sum(qf * kw_new, axis=1, keepdims=True) + b0g
        ow = _attend_t(s_self, vw_new, [sw], [wv_ref[g].astype(BF16)])

        oc = oc_ref[hs, lanes]
        for r in range(4):
            hd = 4 * g + r
            o = (gt[:, hd * 3:hd * 3 + 1] * oc[r:r + 1] + gt[:, hd * 3 + 1:hd * 3 + 2] * osel[r:r + 1]
                 + gt[:, hd * 3 + 2:hd * 3 + 3] * ow[r:r + 1])
            on_ref[hd:hd + 1, :] = o


def _dec_attn(page_table, mids, sids, qm, kmn, vmn, qn8, new_rows, gates, oc, wk, wv, pbm, pbn, wb, b0,
              cache_mk, cache_mv, cache_sk, cache_sv):
    b = page_table.shape[0]
    n_blk = pbm.shape[0]
    n_selp = pbn.shape[0] * SEL_PER_PAGE
    n_top = sids.shape[1] // 2
    win = wk.shape[3]
    per_b = lambda *s: pl.BlockSpec((None,) + s, lambda bi, *_: (bi,) + (0,) * len(s))
    const = lambda a: pl.BlockSpec(a.shape, lambda bi, *_: (0,) * a.ndim)
    any_spec = pl.BlockSpec(memory_space=pl.ANY)
    grid_spec = pltpu.PrefetchScalarGridSpec(
        num_scalar_prefetch=3, grid=(b,),
        in_specs=[per_b(8, HEAD_DIM), per_b(8, HEAD_DIM), per_b(8, HEAD_DIM), per_b(8, LANES), per_b(4, LANES),
                  per_b(1, LANES), per_b(8, LANES), per_b(2, HEAD_DIM, win), per_b(2, HEAD_DIM, win),
                  const(pbm), const(pbn), const(wb), const(b0), any_spec, any_spec, any_spec, any_spec],
        out_specs=[per_b(8, HEAD_DIM), per_b(8, HEAD_DIM)],
        scratch_shapes=[pltpu.VMEM((8, MOBA_TOPK, PAGES_PER_MOBA, HEAD_DIM, PAGE_SIZE), F32),
                        pltpu.VMEM((8, MOBA_TOPK, PAGES_PER_MOBA, HEAD_DIM, PAGE_SIZE), F32),
                        pltpu.VMEM((2, n_top, HEAD_DIM, PAGE_SIZE), F32),
                        pltpu.VMEM((2, n_top, HEAD_DIM, PAGE_SIZE), F32),
                        pltpu.SemaphoreType.DMA((2,))])
    return pl.pallas_call(
        functools.partial(_dec_attn_kernel, n_blk=n_blk, n_selp=n_selp), grid_spec=grid_spec,
        out_shape=[jax.ShapeDtypeStruct((b, 8, HEAD_DIM), F32), jax.ShapeDtypeStruct((b, 8, HEAD_DIM), F32)],
        compiler_params=_cparams(("arbitrary",)), name="dec_attn",
    )(page_table, mids, sids, qm, kmn, vmn, qn8, new_rows, gates, oc, wk, wv, pbm, pbn, wb, b0,
      cache_mk, cache_mv, cache_sk, cache_sv)


def kernel(x_prompt, x_sample, p_prompt, p_sample, cache_moba_k, cache_moba_v, cache_nsa_cmp_k, cache_nsa_cmp_v, cache_nsa_sel_k, cache_nsa_sel_v, state_nsa_win_k, state_nsa_win_v, page_table, rel_bias, g_mix, w_in, w_out, cmp_w1_k, cmp_b1_k, cmp_w2_k, cmp_pos_k, cmp_w1_v, cmp_b1_v, cmp_w2_v, cmp_pos_v, g_ffn, w_router, b_router, w_gate_up, b_gate_up, w_down, b_down, w_ple_proj, w_ple_gate, g_final):
    assert rel_bias.shape == (N_BUCKETS, 16) and g_mix.shape[0] == 1, "one layer, 8 MoBA + 8 NSA heads"
    b, t, d = x_prompt.shape
    bd, dec_seq, _ = x_sample.shape
    assert dec_seq == 1 and t % (2 * MOBA_BLOCK) == 0
    n_pages = page_table.shape[1]
    past = n_pages * PAGE_SIZE
    win = state_nsa_win_k.shape[2]

    bv = rel_bias[_t5_bucket(jnp.arange(MAX_DISTANCE + 1))]
    n_tbm = -(-(MAX_DISTANCE + MOBA_BLOCK - 1) // TQ_M) + 1
    tbm = _toeplitz_tiles(bv[:, :8], TQ_M, MOBA_BLOCK, n_tbm, True)
    n_tbn = -(-(MAX_DISTANCE + TK_N - 1) // TQ_N) + 1
    tbn = _toeplitz_tiles(bv[:, 8:], TQ_N, TK_N, n_tbn, False)
    n_wm = -(-(WINDOW + TK_N) // TQ_N)
    dist = np.stack([TQ_N * m + np.arange(TQ_N)[:, None] - np.arange(TK_N)[None, :] for m in range(n_wm)])
    wm = jnp.asarray(np.where(dist <= WINDOW, 0.0, NEG).astype(np.float32))
    w_proj = _proj_weight(w_in[0])
    wo = w_out[0]
    won = wo[512:].reshape(NSA_HEADS, HEAD_DIM, d)
    zn = jnp.zeros_like(won)
    grp0 = (jnp.arange(NSA_HEADS) < 4)[:, None, None]
    wd_rows = jnp.stack([w_down[0], jnp.zeros_like(w_down[0])], axis=2).reshape(N_EXPERTS, -1, d)
    wts = dict(
        wom=wo[:512].astype(BF16),
        won8=jnp.concatenate([jnp.where(grp0, won, zn), jnp.where(grp0, zn, won)], axis=1).astype(BF16),
        g_ffn=g_ffn[0], w_router=w_router[0], b_router=b_router[0],
        wgu=w_gate_up[0].astype(BF16), bgu=b_gate_up[0][:, None, :],
        wd=wd_rows.astype(BF16), bd=b_down[0][:, None, :],
        wpg=w_ple_gate[0].astype(BF16), wpp=w_ple_proj[0].astype(BF16), g_final=g_final)
    ck = _cmp_const(cmp_pos_k[0], cmp_w1_k[0], cmp_b1_k[0])
    cv = _cmp_const(cmp_pos_v[0], cmp_w1_v[0], cmp_b1_v[0])

    def pad_heads(o):
        z = jnp.zeros_like(o)
        return jnp.concatenate([jnp.where(grp0, o, z), jnp.where(grp0, z, o)], axis=-1).astype(BF16)

    xp = x_prompt.reshape(b * t, d)
    qm, km, vm, kmb, vmb, qn8, kv, kvb, gt, kmean = _project(xp, g_mix, w_proj, 2 * MOBA_BLOCK, True)
    kmean = jnp.pad(kmean.reshape(b, t // MOBA_BLOCK, 512), ((0, 0), (0, LANES - t // MOBA_BLOCK), (0, 0)))
    om = _moba_prompt(qm.reshape(b, t, 512), kmb.reshape(b, t, 512), vmb.reshape(b, t, 512), kmean, tbm)
    pt_p = jnp.arange(b * t // PAGE_SIZE, dtype=jnp.int32).reshape(b, t // PAGE_SIZE)
    pages_t = lambda a: jnp.transpose(a.reshape(-1, PAGE_SIZE, 2, HEAD_DIM), (0, 2, 3, 1))
    kcmp = _compress(_chunk_layout(pages_t(kv[0])), pt_p, cmp_w1_k[0], ck, cmp_w2_k[0])
    vcmp = _compress(_chunk_layout(pages_t(kv[1])), pt_p, cmp_w1_v[0], cv, cmp_w2_v[0])
    seq = lambda a: a.reshape(b, t, LANES)
    on8 = _nsa_prompt(qn8, gt, kcmp, vcmp, seq(kvb[2]), seq(kvb[3]), seq(kvb[4]), seq(kvb[5]), tbn, wm,
                      (t - CMP_LEN) // CMP_STRIDE + 1)

    xs = x_sample.reshape(bd, d)
    qm_s, km_s, vm_s, _, _, qn8_s, kv_s, _, gt_s = _project(xs, g_mix, w_proj, bd, False)
    rows_minor = lambda a: jnp.transpose(a, (0, 1, 3, 4, 2))
    kmean_s = _moba_kmean(rows_minor(cache_moba_k), page_table)
    kcmp_s = _compress(_chunk_layout(rows_minor(cache_nsa_cmp_k)[0]), page_table, cmp_w1_k[0], ck, cmp_w2_k[0])
    vcmp_s = _compress(_chunk_layout(rows_minor(cache_nsa_cmp_v)[0]), page_table, cmp_w1_v[0], cv, cmp_w2_v[0])
    heads = lambda a: a.reshape(bd, 8, HEAD_DIM)
    qn8_sb = jnp.transpose(qn8_s, (1, 0, 2))
    mids, oc_s, pcg = _dec_select(qm_s.reshape(bd, 8, HEAD_DIM, 1), kmean_s, qn8_sb, kcmp_s, vcmp_s,
                                  (past + 1 - CMP_LEN) // CMP_STRIDE + 1)
    n_sel_s = -(-(past + 1) // SEL_BLOCK)
    sids = _dec_topn(pcg.reshape(bd * 2, -1), past, n_sel_s)
    n_blk = past // MOBA_BLOCK
    kpos_m = np.arange(n_blk * MOBA_BLOCK).reshape(n_blk, MOBA_BLOCK)
    pbm = jnp.transpose(jnp.take(bv[:, :8], jnp.asarray(np.minimum(past - kpos_m, MAX_DISTANCE)), axis=0), (0, 2, 1))
    kpos_n = np.arange(past).reshape(n_pages, PAGE_SIZE)
    pbn = jnp.transpose(jnp.take(bv[:, 8:], jnp.asarray(np.minimum(past - kpos_n, MAX_DISTANCE)), axis=0), (0, 2, 1))
    wb = jnp.take(bv[:, 8:], jnp.asarray(np.minimum(win - np.arange(win), MAX_DISTANCE)), axis=0).T
    b0 = bv[0][:, None]
    new_rows = jnp.stack([kv_s[2], kv_s[3], kv_s[4], kv_s[5]], axis=1)
    om_s, on_s = _dec_attn(page_table, mids.reshape(bd, -1), sids.reshape(bd, -1), heads(qm_s), heads(km_s),
                           heads(vm_s), qn8_sb, new_rows, gt_s.reshape(bd, 1, LANES), oc_s,
                           rows_minor(state_nsa_win_k)[0], rows_minor(state_nsa_win_v)[0],
                           pbm, pbn, wb, b0, rows_minor(cache_moba_k), rows_minor(cache_moba_v),
                           rows_minor(cache_nsa_sel_k), rows_minor(cache_nsa_sel_v))
    y_p, y_s = _ffn([(xp, om.reshape(b * t, 512), on8, p_prompt[0].reshape(b * t, -1), 256),
                     (xs, om_s.reshape(bd, 512).astype(BF16), pad_heads(jnp.transpose(on_s, (1, 0, 2))),
                      p_sample[0].reshape(bd, -1), bd)], wts)

    wp = min(WINDOW, t)
    mh = lambda a, n, h: a.reshape(1, n, -1, h, HEAD_DIM)
    kw_new = kv_s[4].reshape(bd, 1, 2, HEAD_DIM)
    vw_new = kv_s[5].reshape(bd, 1, 2, HEAD_DIM)
    win_k = jnp.concatenate([state_nsa_win_k[0], kw_new], axis=1)[:, -win:][None]
    win_v = jnp.concatenate([state_nsa_win_v[0], vw_new], axis=1)[:, -win:][None]
    return (y_p.reshape(b, t, d), y_s.reshape(bd, 1, d),
            mh(km, b, 8), mh(vm, b, 8), mh(kv[0], b, 2), mh(kv[1], b, 2), mh(kv[2], b, 2), mh(kv[3], b, 2),
            mh(kv[4], b, 2)[:, :, -wp:], mh(kv[5], b, 2)[:, :, -wp:],
            mh(km_s, bd, 8), mh(vm_s, bd, 8), mh(kv_s[0], bd, 2), mh(kv_s[1], bd, 2), mh(kv_s[2], bd, 2),
            mh(kv_s[3], bd, 2), win_k, win_v)
```

```python
import functools
import math

import numpy as np
import jax
import jax.numpy as jnp
from jax import lax
from jax.experimental import pallas as pl
from jax.experimental.pallas import tpu as pltpu

PAGE_SIZE = 128
HEAD_DIM = 64
MOBA_BLOCK = 256
MOBA_TOPK = 3
CMP_LEN = 32
CMP_STRIDE = 16
CMP_HIDDEN = 256
SEL_BLOCK = 64
SEL_TOPN = 16
WINDOW = 512
N_BUCKETS = 32
MAX_DISTANCE = 1024
N_EXPERTS = 32
TOP_K = 4
SWIGLU_LIMIT = 7.0
SWIGLU_ALPHA = 1.702
RMS_EPS = 1e-6
NEG = -1e30
FORCE = 1e30
SCALE = HEAD_DIM ** -0.5
LANES = 128
VMEM_LIMIT = 56 * 1024 * 1024

F32 = jnp.float32
BF16 = jnp.bfloat16
HI = lax.Precision.HIGHEST
NT = (((1,), (1,)), ((), ()))


def _cparams(sem):
    return pltpu.CompilerParams(dimension_semantics=sem, vmem_limit_bytes=VMEM_LIMIT)


def _const_spec(shape):
    n = len(shape)
    return pl.BlockSpec(shape, lambda *_: (0,) * n)


def _t5_bucket(dist):
    max_exact = N_BUCKETS // 2
    d = jnp.maximum(dist, 0)
    df = jnp.maximum(d, max_exact).astype(jnp.float32)
    large = max_exact + (jnp.log(df / max_exact) / math.log(MAX_DISTANCE / max_exact)
                         * (N_BUCKETS - max_exact)).astype(jnp.int32)
    return jnp.where(d < max_exact, d, jnp.minimum(large, N_BUCKETS - 1))


def _topk_mask(score, valid, k, idx_iota, axis):
    n = float(score.shape[axis])
    low = -3e38
    pos_f = idx_iota.astype(F32)
    work = jnp.where(valid, score, low)
    sel = jnp.zeros(score.shape, F32)
    ids = []
    for _ in range(k):
        mx = jnp.max(work, axis=axis, keepdims=True)
        cand = (work == mx) & (mx > low)
        idx = jnp.min(jnp.where(cand, pos_f, n), axis=axis, keepdims=True)
        hit = pos_f == idx
        sel = jnp.where(hit, 1.0, sel)
        work = jnp.where(hit, low, work)
        ids.append(idx.astype(jnp.int32))
    return sel, ids


C_QM, C_KM, C_VM, C_QN, C_KV, C_GT, C_END = 0, 512, 1024, 1536, 2560, 3328, 3456


def _proj_weight(w_in):
    d = w_in.shape[0]
    qn = w_in[:, 1536:2048].reshape(d, 8, HEAD_DIM)
    z = jnp.zeros((d, 8, HEAD_DIM), w_in.dtype)
    grp0 = (jnp.arange(8) < 4)[None, :, None]
    qn_pad = jnp.concatenate([jnp.where(grp0, qn, z), jnp.where(grp0, z, qn)], axis=-1).reshape(d, 1024)
    gates = jnp.pad(w_in[:, 2816:2840], ((0, 0), (0, LANES - 24)))
    return jnp.concatenate([w_in[:, :1536], qn_pad, w_in[:, 2048:2816], gates], axis=1).astype(BF16)


def _proj_kernel(x_ref, g_ref, w_ref, qm_ref, km_ref, vm_ref, kmb_ref, vmb_ref, qn_ref,
                 kv_ref, kvb_ref, gt_ref, *rest, with_kmean):
    x = x_ref[...]
    h = x * lax.rsqrt(jnp.mean(x * x, axis=-1, keepdims=True) + RMS_EPS) * g_ref[...]
    z = jnp.dot(h.astype(BF16), w_ref[...], preferred_element_type=F32)
    qm_ref[...] = z[:, C_QM:C_KM]
    km = z[:, C_KM:C_VM]
    vm = z[:, C_VM:C_QN]
    km_ref[...] = km
    vm_ref[...] = vm
    kmb_ref[...] = km.astype(BF16)
    vmb_ref[...] = vm.astype(BF16)
    for hd in range(8):
        qn_ref[hd] = (z[:, C_QN + hd * LANES:C_QN + (hd + 1) * LANES] * SCALE).astype(BF16)
    for j in range(6):
        blk = z[:, C_KV + j * LANES:C_KV + (j + 1) * LANES]
        kv_ref[j] = blk
        kvb_ref[j] = blk.astype(BF16)
    gt_ref[...] = jax.nn.sigmoid(z[:, C_GT:C_END])
    if with_kmean:
        (kmean_ref,) = rest
        nblk = km.shape[0] // MOBA_BLOCK
        for i in range(nblk):
            kmean_ref[0, i:i + 1, :] = jnp.mean(km[i * MOBA_BLOCK:(i + 1) * MOBA_BLOCK], axis=0, keepdims=True)


def _project(x, g, w, tm, with_kmean):
    t, d = x.shape
    nt = t // tm
    row = lambda w_: pl.BlockSpec((tm, w_), lambda i: (i, 0))
    out_shape = [
        jax.ShapeDtypeStruct((t, 512), F32), jax.ShapeDtypeStruct((t, 512), F32), jax.ShapeDtypeStruct((t, 512), F32),
        jax.ShapeDtypeStruct((t, 512), BF16), jax.ShapeDtypeStruct((t, 512), BF16),
        jax.ShapeDtypeStruct((8, t, LANES), BF16),
        jax.ShapeDtypeStruct((6, t, LANES), F32), jax.ShapeDtypeStruct((6, t, LANES), BF16),
        jax.ShapeDtypeStruct((t, LANES), F32),
    ]
    out_specs = [row(512), row(512), row(512), row(512), row(512),
                 pl.BlockSpec((8, tm, LANES), lambda i: (0, i, 0)),
                 pl.BlockSpec((6, tm, LANES), lambda i: (0, i, 0)),
                 pl.BlockSpec((6, tm, LANES), lambda i: (0, i, 0)),
                 row(LANES)]
    if with_kmean:
        nb = tm // MOBA_BLOCK
        out_shape.append(jax.ShapeDtypeStruct((nt, nb, 512), F32))
        out_specs.append(pl.BlockSpec((1, nb, 512), lambda i: (i, 0, 0)))
    return pl.pallas_call(
        functools.partial(_proj_kernel, with_kmean=with_kmean),
        grid=(nt,),
        in_specs=[row(d), _const_spec((1, d)), _const_spec(w.shape)],
        out_specs=out_specs, out_shape=out_shape,
        compiler_params=_cparams(("parallel",)), name="proj",
    )(x, g, w)


def _toeplitz_kernel(v_ref, o_ref):
    rows, cols = o_ref.shape[1:]
    x = jnp.broadcast_to(v_ref[0], (rows, v_ref.shape[2]))
    o_ref[0] = pltpu.roll(x, 0, 1, stride=1, stride_axis=0)[:, :cols]


def _toeplitz_tiles(bv, rows, cols, n_tiles, head_major):
    length = rows + cols
    assert length % LANES == 0
    k = np.arange(length)
    off = np.where(k < cols, -k, length - k)
    dist = np.stack([rows * m + off for m in range(n_tiles)])
    v = jnp.take(bv, jnp.asarray(np.clip(dist, 0, MAX_DISTANCE)), axis=0)
    v = jnp.where(jnp.asarray(dist >= 0)[..., None], v, NEG)
    v = jnp.transpose(v, (2, 0, 1) if head_major else (0, 2, 1))
    lead = v.shape[:2]
    n = lead[0] * lead[1]
    tiles = pl.pallas_call(
        _toeplitz_kernel, grid=(n,),
        in_specs=[pl.BlockSpec((1, 1, length), lambda i: (i, 0, 0))],
        out_specs=pl.BlockSpec((1, rows, cols), lambda i: (i, 0, 0)),
        out_shape=jax.ShapeDtypeStruct((n, rows, cols), F32),
        compiler_params=_cparams(("parallel",)), name="bias_tiles",
    )(v.reshape(n, 1, length))
    return tiles.reshape(lead + (rows, cols))


TQ_M = MOBA_BLOCK


def _online_attention(lo, hi, logits, values, m_ref, l_ref, acc_ref):
    m_ref[...] = jnp.full(m_ref.shape, -3e38, F32)
    l_ref[...] = jnp.zeros(l_ref.shape, F32)
    acc_ref[...] = jnp.zeros(acc_ref.shape, F32)

    def trip(jj, c):
        j0 = lo + 4 * jj
        tiles = [jnp.minimum(j0 + u, hi) for u in range(4)]
        ss = [logits(j) for j in tiles]
        tmax = None
        for s in ss:
            half = jnp.maximum(s[:, :LANES], s[:, LANES:])
            tmax = half if tmax is None else jnp.maximum(tmax, half)
        m_old = m_ref[...]
        m_new = jnp.maximum(m_old, jnp.max(tmax, axis=1, keepdims=True))
        alpha = jnp.exp(m_old - m_new)
        l_sum = alpha * l_ref[...]
        pv_sum = alpha * acc_ref[...]
        for u, (j, s) in enumerate(zip(tiles, ss)):
            pa = jnp.exp(s[:, :LANES] - m_new)
            pb = jnp.exp(s[:, LANES:] - m_new)
            pv = jnp.dot(jnp.concatenate([pa.astype(BF16), pb.astype(BF16)], axis=1), values(j),
                         preferred_element_type=F32)
            if u == 0:
                l_sum, pv_sum = l_sum + (pa + pb), pv_sum + pv
            else:
                w = (j0 + u <= hi).astype(F32)
                l_sum, pv_sum = l_sum + w * (pa + pb), pv_sum + w * pv
        m_ref[...] = m_new
        l_ref[...] = l_sum
        acc_ref[...] = pv_sum
        return c

    lax.fori_loop(0, (hi - lo + 4) // 4, trip, 0)
    return acc_ref[...] / jnp.sum(l_ref[...], axis=1, keepdims=True)


def _moba_kernel(q_ref, k_ref, v_ref, kmean_ref, tb_ref, o_ref, rm_ref, l_ref, acc_ref, *, n_tb):
    qi = pl.program_id(2)
    q2 = q_ref[0]
    lane = lax.broadcasted_iota(jnp.int32, q2.shape, 1)
    q_st = jnp.concatenate([jnp.where(lane < HEAD_DIM, q2, 0.0), jnp.where(lane >= HEAD_DIM, q2, 0.0)], axis=0)
    rows = 2 * TQ_M
    gate_t = lax.dot_general(kmean_ref[0], q_st, NT, precision=HI, preferred_element_type=F32)
    blk = lax.broadcasted_iota(jnp.int32, gate_t.shape, 0)
    sel_t, _ = _topk_mask(gate_t, blk < qi, MOBA_TOPK, blk, 0)
    selneg = jnp.where((sel_t > 0.5) | (blk == qi), 0.0, NEG).T
    q_aug = jnp.concatenate([(q_st * SCALE).astype(BF16), selneg.astype(BF16)], axis=1)
    key_lane = lax.broadcasted_iota(jnp.int32, (MOBA_BLOCK, LANES), 1)

    def kv_rows(ref, j):
        return ref[0, pl.ds(pl.multiple_of(j * MOBA_BLOCK, MOBA_BLOCK), MOBA_BLOCK), :]

    def logits(j):
        k_aug = jnp.concatenate([kv_rows(k_ref, j), (key_lane == j).astype(BF16)], axis=1)
        s = lax.dot_general(q_aug, k_aug, NT, preferred_element_type=F32)
        m_id = jnp.minimum(qi - j, n_tb - 1)
        return s + jnp.concatenate([tb_ref[0, m_id], tb_ref[1, m_id]], axis=0)

    o = _online_attention(0, qi, logits, lambda j: kv_rows(v_ref, j), rm_ref, l_ref, acc_ref)
    o_ref[0] = jnp.where(lane < HEAD_DIM, o[:TQ_M], o[TQ_M:]).astype(o_ref.dtype)


def _moba_prompt(qm, kmb, vmb, kmean, tb):
    b, t, _ = qm.shape
    nb = kmean.shape[1]
    assert t // MOBA_BLOCK <= nb == LANES
    n_tb = tb.shape[1]
    return pl.pallas_call(
        functools.partial(_moba_kernel, n_tb=n_tb),
        grid=(4, b, t // TQ_M),
        in_specs=[pl.BlockSpec((1, TQ_M, LANES), lambda p, bi, qi: (bi, qi, p)),
                  pl.BlockSpec((1, t, LANES), lambda p, bi, qi: (bi, 0, p)),
                  pl.BlockSpec((1, t, LANES), lambda p, bi, qi: (bi, 0, p)),
                  pl.BlockSpec((1, nb, LANES), lambda p, bi, qi: (bi, 0, p)),
                  pl.BlockSpec((2, n_tb, TQ_M, MOBA_BLOCK), lambda p, bi, qi: (p, 0, 0, 0))],
        out_specs=pl.BlockSpec((1, TQ_M, LANES), lambda p, bi, qi: (bi, qi, p)),
        out_shape=jax.ShapeDtypeStruct((b, t, 512), BF16),
        scratch_shapes=[pltpu.VMEM((2 * TQ_M, LANES), F32)] * 3,
        compiler_params=_cparams(("parallel", "parallel", "parallel")), name="moba_prompt",
    )(qm, kmb, vmb, kmean, tb)


CHUNKS_PER_PAGE = PAGE_SIZE // CMP_STRIDE
CMP_PAGES = 32


def _chunk_layout(pages_t):
    n = pages_t.shape[0]
    x = pages_t.reshape(n, 2, HEAD_DIM, CHUNKS_PER_PAGE, CMP_STRIDE)
    return jnp.transpose(x, (0, 1, 3, 2, 4)).reshape(n, 2 * CHUNKS_PER_PAGE, HEAD_DIM * CMP_STRIDE).astype(BF16)


def _chunk_weight(w_half):
    h = w_half.shape[1]
    return jnp.transpose(w_half.reshape(CMP_STRIDE, HEAD_DIM, h), (1, 0, 2)).reshape(CMP_STRIDE * HEAD_DIM, h)


def _cmp_const_kernel(pe_ref, w1_ref, b1_ref, o_ref):
    o_ref[...] = jnp.sum(w1_ref[...] * pe_ref[...], axis=0, keepdims=True) + b1_ref[...]


def _cmp_const(pe, w1, b1):
    return pl.pallas_call(
        _cmp_const_kernel, out_shape=jax.ShapeDtypeStruct((1, CMP_HIDDEN), F32), name="cmp_const",
    )(pe.reshape(CMP_LEN * HEAD_DIM, 1), w1, b1.reshape(1, CMP_HIDDEN))


def _gelu_tanh(x):
    return 0.5 * x * (1.0 + jnp.tanh(math.sqrt(2.0 / math.pi) * (x + 0.044715 * x * x * x)))


def _compress_kernel(pt_ref, *refs, n_pg):
    x_refs = refs[:n_pg + 1]
    wab_ref, c_ref, w2_ref, o_ref = refs[n_pg + 1:]
    rows_pp = 2 * CHUNKS_PER_PAGE
    x = jnp.concatenate([r[0] for r in x_refs], axis=0)
    ab = jnp.dot(x, wab_ref[...], preferred_element_type=F32)
    a = ab[:n_pg * rows_pp, :CMP_HIDDEN]
    bm = ab[:, CMP_HIDDEN:]
    n = bm.shape[0]
    nxt = pltpu.roll(bm, n - 1, 0)
    nxt_page = pltpu.roll(bm, n - (rows_pp - CHUNKS_PER_PAGE + 1), 0)
    c_id = lax.broadcasted_iota(jnp.int32, bm.shape, 0) % CHUNKS_PER_PAGE
    b_next = jnp.where(c_id == CHUNKS_PER_PAGE - 1, nxt_page, nxt)[:n_pg * rows_pp]
    hid = _gelu_tanh(a + b_next + c_ref[...]).astype(BF16)
    hid = hid.reshape(n_pg, 2, CHUNKS_PER_PAGE, CMP_HIDDEN)
    out = None
    for g in range(2):
        hg = hid[:, g].reshape(n_pg * CHUNKS_PER_PAGE, CMP_HIDDEN)
        og = jnp.dot(hg, w2_ref[g], preferred_element_type=F32)
        out = og if out is None else out + og
    o_ref[0] = out.astype(o_ref.dtype)


def _compress(xc, page_table, w1, const, w2):
    b, n_pages = page_table.shape
    n_pg = math.gcd(n_pages, CMP_PAGES)
    half = CMP_STRIDE * HEAD_DIM
    wab = jnp.concatenate([_chunk_weight(w1[:half]), _chunk_weight(w1[half:])], axis=1).astype(BF16)
    z = jnp.zeros_like(w2)
    w2p = jnp.stack([jnp.concatenate([w2, z], axis=1), jnp.concatenate([z, w2], axis=1)]).astype(BF16)

    def page_spec(k):
        def imap(bi, i, pt):
            return (pt[bi, jnp.minimum(i * n_pg + k, n_pages - 1)], 0, 0)
        return pl.BlockSpec((1, 2 * CHUNKS_PER_PAGE, half), imap)

    tile = n_pg * CHUNKS_PER_PAGE
    grid_spec = pltpu.PrefetchScalarGridSpec(
        num_scalar_prefetch=1, grid=(b, n_pages // n_pg),
        in_specs=[page_spec(k) for k in range(n_pg + 1)] + [
            pl.BlockSpec(wab.shape, lambda bi, i, pt: (0, 0)),
            pl.BlockSpec(const.shape, lambda bi, i, pt: (0, 0)),
            pl.BlockSpec(w2p.shape, lambda bi, i, pt: (0, 0, 0))],
        out_specs=pl.BlockSpec((1, tile, LANES), lambda bi, i, pt: (bi, i, 0)))
    return pl.pallas_call(
        functools.partial(_compress_kernel, n_pg=n_pg), grid_spec=grid_spec,
        out_shape=jax.ShapeDtypeStruct((b, n_pages * CHUNKS_PER_PAGE, LANES), BF16),
        compiler_params=_cparams(("parallel", "parallel")), name="compress",
    )(page_table, *([xc] * (n_pg + 1)), wab, const, w2p)


TQ_N = 128
TK_N = 256
NSA_HEADS = 8
assert WINDOW % TK_N == 0 and TK_N % TQ_N == 0 and TK_N == 2 * LANES == MOBA_BLOCK
WIN_TILES = WINDOW // TK_N + 1


def _overlap_matrix(n_cmp_pad, n_sel_pad):
    cstart = np.arange(n_cmp_pad)[:, None] * CMP_STRIDE
    sstart = np.arange(n_sel_pad)[None, :] * SEL_BLOCK
    return ((cstart < sstart + SEL_BLOCK) & (cstart + CMP_LEN > sstart)).astype(np.float32)


def _nsa_kernel(q_ref, gt_ref, kc_ref, vc_ref, ks_ref, vs_ref, kw_ref, vw_ref, ovt_ref, tb_ref, wm_ref, o_ref,
                rm_ref, l_ref, acc_ref, *, n_cmp, n_sel, n_tb):
    qi = pl.program_id(1)
    q0 = qi * TQ_N
    rows = NSA_HEADS * TQ_N
    q8 = q_ref[...].reshape(rows, LANES)
    n_cp = kc_ref.shape[1]
    pos = q0 + lax.broadcasted_iota(jnp.int32, (1, TQ_N, 1), 1)

    lc = lax.dot_general(q8, kc_ref[0], NT, preferred_element_type=F32).reshape(NSA_HEADS, TQ_N, n_cp)
    n_id = lax.broadcasted_iota(jnp.int32, (1, 1, n_cp), 2)
    ok = (n_id * CMP_STRIDE + (CMP_LEN - 1) <= pos) & (n_id < n_cmp)
    mc = jnp.max(jnp.where(ok, lc, NEG), axis=2, keepdims=True)
    pc = jnp.where(ok, jnp.exp(lc - mc), 0.0)
    lsum = jnp.sum(pc, axis=2, keepdims=True)
    pc = pc / jnp.where(lsum > 0.0, lsum, 1.0)
    oc = jnp.dot(pc.reshape(rows, n_cp).astype(BF16), vc_ref[0], preferred_element_type=F32)

    pcg = pc.reshape(2, 4, TQ_N, n_cp)
    pcg = (pcg[:, 0] + pcg[:, 1]) + (pcg[:, 2] + pcg[:, 3])
    imp_t = lax.dot_general(ovt_ref[...], pcg.reshape(2 * TQ_N, n_cp), NT, precision=HI, preferred_element_type=F32)
    n_sp = imp_t.shape[0]
    sid = lax.broadcasted_iota(jnp.int32, imp_t.shape, 0)
    pos_t = q0 + lax.broadcasted_iota(jnp.int32, imp_t.shape, 1) % TQ_N
    cur = pos_t // SEL_BLOCK
    forced = (sid == 0) | (sid == cur) | (sid == cur - 1)
    valid = (sid * SEL_BLOCK <= pos_t) & (sid < n_sel)
    sel_t, _ = _topk_mask(jnp.where(forced, FORCE, imp_t), valid, min(SEL_TOPN, n_sel), sid, 0)
    selneg = jnp.where(sel_t > 0.5, 0.0, NEG).T.astype(BF16).reshape(2, TQ_N, n_sp)
    selneg8 = jnp.broadcast_to(selneg[:, None], (2, 4, TQ_N, n_sp)).reshape(rows, n_sp)
    q_aug = jnp.concatenate([q8, selneg8], axis=1)

    jmax = q0 // TK_N
    e_lane = lax.broadcasted_iota(jnp.int32, (TK_N, n_sp), 1)
    e_blk = lax.broadcasted_iota(jnp.int32, (TK_N, n_sp), 0) // SEL_BLOCK

    def bias(j):
        return tb_ref[jnp.minimum(qi - 2 * j, n_tb - 1)].reshape(rows, TK_N)

    def kslice(ref, j):
        return ref[0, pl.ds(pl.multiple_of(j * TK_N, TK_N), TK_N), :]

    def sel_logits(j):
        k_aug = jnp.concatenate([kslice(ks_ref, j), (e_lane == e_blk + j * (TK_N // SEL_BLOCK)).astype(BF16)], axis=1)
        return lax.dot_general(q_aug, k_aug, NT, preferred_element_type=F32) + bias(j)

    osel = _online_attention(0, jmax, sel_logits, lambda j: kslice(vs_ref, j), rm_ref, l_ref, acc_ref)

    def win_logits(j):
        s = lax.dot_general(q8, kslice(kw_ref, j), NT, preferred_element_type=F32) + bias(j)
        wmask = wm_ref[jnp.minimum(qi - 2 * j, wm_ref.shape[0] - 1)]
        return (s.reshape(NSA_HEADS, TQ_N, TK_N) + wmask[None]).reshape(rows, TK_N)

    jlo = jnp.maximum(jmax - (WIN_TILES - 1), 0)
    ow = _online_attention(jlo, jmax, win_logits, lambda j: kslice(vw_ref, j), rm_ref, l_ref, acc_ref)

    gt = gt_ref[...]
    glane = lax.broadcasted_iota(jnp.int32, gt.shape, 1)

    def gate(hd, br):
        return jnp.sum(jnp.where(glane == hd * 3 + br, gt, 0.0), axis=1, keepdims=True)

    for hd in range(NSA_HEADS):
        r0 = hd * TQ_N
        o = (gate(hd, 0) * oc[r0:r0 + TQ_N] + gate(hd, 1) * osel[r0:r0 + TQ_N] + gate(hd, 2) * ow[r0:r0 + TQ_N])
        o_ref[hd] = o.astype(o_ref.dtype)


def _nsa_prompt(qn8, gates, kcmp, vcmp, ksb, vsb, kwb, vwb, tbn, wm, n_cmp):
    b, t, _ = ksb.shape
    n_cp = kcmp.shape[1]
    n_sel = t // SEL_BLOCK
    n_sp = -(-n_sel // LANES) * LANES
    ov = jnp.asarray(_overlap_matrix(n_cp, n_sp).T)
    n_tb = tbn.shape[0]
    nq = t // TQ_N
    full = lambda a: pl.BlockSpec((1,) + a.shape[1:], lambda bi, qi: (bi,) + (0,) * (a.ndim - 1))
    once = lambda a: pl.BlockSpec(a.shape, lambda bi, qi: (0,) * a.ndim, pipeline_mode=pl.Buffered(1))
    return pl.pallas_call(
        functools.partial(_nsa_kernel, n_cmp=n_cmp, n_sel=n_sel, n_tb=n_tb),
        grid=(b, nq),
        in_specs=[pl.BlockSpec((NSA_HEADS, TQ_N, LANES), lambda bi, qi: (0, bi * nq + qi, 0)),
                  pl.BlockSpec((TQ_N, LANES), lambda bi, qi: (bi * nq + qi, 0)),
                  full(kcmp), full(vcmp), full(ksb), full(vsb), full(kwb), full(vwb),
                  once(ov), once(tbn), once(wm)],
        out_specs=pl.BlockSpec((NSA_HEADS, TQ_N, LANES), lambda bi, qi: (0, bi * nq + qi, 0)),
        out_shape=jax.ShapeDtypeStruct((NSA_HEADS, b * t, LANES), BF16),
        scratch_shapes=[pltpu.VMEM((NSA_HEADS * TQ_N, LANES), F32)] * 3,
        compiler_params=_cparams(("parallel", "parallel")), name="nsa_prompt",
    )(qn8, gates, kcmp, vcmp, ksb, vsb, kwb, vwb, ov, tbn, wm)


def _mix_kernel(x_ref, om_ref, on_ref, wom_ref, won_ref, g_ref, wr_ref, br_ref, c0_ref,
                r_ref, h_ref, ti_ref, tg_ref, rk_ref, cnt_ref, carry):
    @pl.when(pl.program_id(0) == 0)
    def _():
        carry[...] = c0_ref[...]

    acc = x_ref[...] + jnp.dot(om_ref[...], wom_ref[...], preferred_element_type=F32)
    for hd in range(NSA_HEADS):
        acc = acc + jnp.dot(on_ref[hd], won_ref[hd], preferred_element_type=F32)
    r_ref[...] = acc
    h = acc * lax.rsqrt(jnp.mean(acc * acc, axis=-1, keepdims=True) + RMS_EPS) * g_ref[...]
    h_ref[...] = h
    logits = jnp.dot(h, wr_ref[...], precision=HI, preferred_element_type=F32) + br_ref[...]
    e_id = lax.broadcasted_iota(jnp.int32, logits.shape, 1)
    work = logits
    vals, ids = [], []
    for _ in range(TOP_K):
        mx = jnp.max(work, axis=1, keepdims=True)
        idx = jnp.min(jnp.where(work == mx, e_id, N_EXPERTS), axis=1, keepdims=True)
        work = jnp.where(e_id == idx, -3e38, work)
        vals.append(mx)
        ids.append(idx)
    ex = [jnp.exp(v - vals[0]) for v in vals]
    tot = ex[0]
    for e in ex[1:]:
        tot = tot + e
    tm = logits.shape[0]
    onehot = [(e_id == ids[k]).astype(F32) for k in range(TOP_K)]
    oh_all = (onehot[0] + onehot[1]) + (onehot[2] + onehot[3])
    earlier = (lax.broadcasted_iota(jnp.int32, (tm, tm), 1) < lax.broadcasted_iota(jnp.int32, (tm, tm), 0))
    before = jnp.dot(earlier.astype(BF16), oh_all.astype(BF16), preferred_element_type=F32) + carry[...]
    k_id = lax.broadcasted_iota(jnp.int32, ti_ref.shape, 1)
    ti = jnp.zeros(ti_ref.shape, jnp.int32)
    rk = jnp.zeros(rk_ref.shape, jnp.int32)
    tg = jnp.zeros(tg_ref.shape, F32)
    for k in range(TOP_K):
        rank_k = jnp.sum(onehot[k] * before, axis=1, keepdims=True)
        ti = jnp.where(k_id == k, ids[k], ti)
        rk = jnp.where(k_id == k, rank_k.astype(jnp.int32), rk)
        tg = jnp.where(k_id == k, ex[k] / tot, tg)
        before = before + onehot[k]
    ti_ref[...] = ti
    rk_ref[...] = rk
    tg_ref[...] = tg
    carry[...] += jnp.sum(oh_all, axis=0, keepdims=True)
    cnt_ref[...] = carry[...]


def _mix(x, om, on8, wom, won8, g_ffn, w_router, b_router, counts0, tm):
    t, d = x.shape
    row = lambda w_: pl.BlockSpec((tm, w_), lambda i: (i, 0))
    return pl.pallas_call(
        _mix_kernel, grid=(t // tm,),
        in_specs=[row(d), row(512), pl.BlockSpec((NSA_HEADS, tm, LANES), lambda i: (0, i, 0)),
                  _const_spec(wom.shape), _const_spec(won8.shape), _const_spec((1, d)),
                  _const_spec(w_router.shape), _const_spec((1, N_EXPERTS)), _const_spec((1, N_EXPERTS))],
        out_specs=[row(d), row(d), row(TOP_K), row(TOP_K), row(TOP_K), _const_spec((1, N_EXPERTS))],
        out_shape=[jax.ShapeDtypeStruct((t, d), F32), jax.ShapeDtypeStruct((t, d), F32),
                   jax.ShapeDtypeStruct((t, TOP_K), jnp.int32), jax.ShapeDtypeStruct((t, TOP_K), F32),
                   jax.ShapeDtypeStruct((t, TOP_K), jnp.int32), jax.ShapeDtypeStruct((1, N_EXPERTS), F32)],
        scratch_shapes=[pltpu.VMEM((1, N_EXPERTS), F32)],
        compiler_params=_cparams(("arbitrary",)), name="mix_router",
    )(x, om, on8, wom, won8, g_ffn.reshape(1, d), w_router, b_router.reshape(1, N_EXPERTS), counts0)


TM_E = 256


def _route(top_i, rank, counts, tm):
    n_assign = top_i.shape[0] * TOP_K
    cnt = counts.reshape(N_EXPERTS).astype(jnp.int32)
    padded = (cnt + tm - 1) // tm * tm
    pad_end = jnp.cumsum(padded)
    dest = (jnp.take(pad_end - padded, top_i, axis=0) + rank).reshape(n_assign)
    n_blocks = -(-n_assign // tm) + N_EXPERTS
    first_row = jnp.arange(n_blocks, dtype=jnp.int32)[:, None] * tm
    blk_e = jnp.minimum(jnp.sum((pad_end[None, :] <= first_row).astype(jnp.int32), axis=1), N_EXPERTS - 1)
    n_used = (pad_end[-1] // tm).astype(jnp.int32).reshape(1)
    return dest.astype(jnp.int32), blk_e.astype(jnp.int32), n_used, n_blocks


DISPATCH_UNROLL = 8


def _dispatch_kernel(d_ref, h_ref, xs_in, xs_out, sem):
    del xs_in
    tm = h_ref.shape[0]
    unroll = math.gcd(tm, DISPATCH_UNROLL)

    def body(c, carry):
        for u in range(unroll):
            r = c * unroll + u
            for k in range(TOP_K):
                pltpu.make_async_copy(h_ref.at[pl.ds(r, 1), :], xs_out.at[pl.ds(d_ref[r * TOP_K + k], 1), :], sem).start()
        return carry

    lax.fori_loop(0, tm // unroll, body, 0)
    for k in range(TOP_K):
        pltpu.make_async_copy(h_ref, xs_out.at[pl.ds(0, tm), :], sem).wait()


def _dispatch(h, dest, xs, tm):
    t, d = h.shape
    assert t % tm == 0
    return pl.pallas_call(
        _dispatch_kernel, grid=(t // tm,),
        in_specs=[pl.BlockSpec((tm * TOP_K,), lambda i: (i,), memory_space=pltpu.SMEM),
                  pl.BlockSpec((tm, d), lambda i: (i, 0)),
                  pl.BlockSpec(memory_space=pl.ANY)],
        out_specs=pl.BlockSpec(memory_space=pl.ANY),
        out_shape=jax.ShapeDtypeStruct(xs.shape, xs.dtype),
        scratch_shapes=[pltpu.SemaphoreType.DMA(())],
        input_output_aliases={2: 0},
        compiler_params=_cparams(("arbitrary",)), name="moe_dispatch",
    )(dest, h, xs)


def _expert_kernel(blk_e_ref, n_used_ref, x_ref, wgu_ref, bgu_ref, wd_ref, bd_ref, y_ref, wgu_s, wd_s):
    i = pl.program_id(0)

    @pl.when((i == 0) | (blk_e_ref[i] != blk_e_ref[jnp.maximum(i - 1, 0)]))
    def _():
        wgu_s[...] = wgu_ref[0].astype(BF16)
        hi = lax.bitcast_convert_type(wd_ref[0].astype(BF16).astype(F32), jnp.uint32)
        wd_s[...] = pltpu.bitcast(hi | (hi >> 16), BF16)

    @pl.when(i < n_used_ref[0])
    def _():
        gu = jnp.dot(x_ref[...].astype(BF16), wgu_s[...], preferred_element_type=F32) + bgu_ref[0]
        glu = jnp.minimum(gu, SWIGLU_LIMIT)
        lin = jnp.clip(gu, -SWIGLU_LIMIT, SWIGLU_LIMIT) + 1.0
        lin_next = pltpu.roll(lin, lin.shape[1] - 1, 1)
        even = lax.broadcasted_iota(jnp.int32, gu.shape, 1) % 2 == 0
        act = jnp.where(even, glu * jax.nn.sigmoid(SWIGLU_ALPHA * glu) * lin_next, 0.0)
        y_ref[...] = jnp.dot(act.astype(BF16), wd_s[...], preferred_element_type=F32) + bd_ref[0]

    @pl.when(i >= n_used_ref[0])
    def _():
        y_ref[...] = jnp.zeros_like(y_ref)


def _experts(xs, blk_e, n_used, wgu, bgu, wd, bd):
    n_rows, d = xs.shape
    dff2 = wgu.shape[2]
    tm = TM_E
    e_map = lambda i, be, nu: (be[i], 0, 0)
    grid_spec = pltpu.PrefetchScalarGridSpec(
        num_scalar_prefetch=2, grid=(n_rows // tm,),
        in_specs=[pl.BlockSpec((tm, d), lambda i, be, nu: (i, 0)),
                  pl.BlockSpec((1, d, dff2), e_map), pl.BlockSpec((1, 1, dff2), e_map),
                  pl.BlockSpec((1, dff2 // 2, d), e_map), pl.BlockSpec((1, 1, d), e_map)],
        out_specs=pl.BlockSpec((tm, d), lambda i, be, nu: (i, 0)),
        scratch_shapes=[pltpu.VMEM((d, dff2), BF16), pltpu.VMEM((dff2, d), BF16)])
    return pl.pallas_call(
        _expert_kernel, grid_spec=grid_spec,
        out_shape=jax.ShapeDtypeStruct((n_rows, d), F32),
        compiler_params=_cparams(("arbitrary",)), name="moe_experts",
    )(blk_e, n_used, xs, wgu, bgu, wd, bd)


def _combine_kernel(d_cur_ref, d_nxt_ref, y_hbm, r_ref, tg_ref, p_ref, wpg_ref, wpp_ref, gf_ref, o_ref, ybuf, sem):
    i = pl.program_id(0)
    n = pl.num_programs(0)
    tm = ybuf.shape[2]

    def start_gather(d_ref, slot):
        for r in range(tm):
            for k in range(TOP_K):
                pltpu.make_async_copy(y_hbm.at[pl.ds(d_ref[r * TOP_K + k], 1), :],
                                      ybuf.at[slot, k, pl.ds(r, 1), :], sem.at[slot]).start()

    def wait_gather(slot):
        for k in range(TOP_K):
            pltpu.make_async_copy(y_hbm.at[pl.ds(0, tm), :], ybuf.at[slot, k], sem.at[slot]).wait()

    @pl.when(i == 0)
    def _():
        start_gather(d_cur_ref, 0)

    def step(slot):
        wait_gather(slot)
        start_gather(d_nxt_ref, 1 - slot)
        tg = tg_ref[...]
        r = r_ref[...]
        for k in range(TOP_K):
            r = r + tg[:, k:k + 1] * ybuf[slot, k]
        gate = jax.nn.sigmoid(jnp.dot(r.astype(BF16), wpg_ref[...], preferred_element_type=F32))
        r = r + gate * jnp.dot(p_ref[...].astype(BF16), wpp_ref[...], preferred_element_type=F32)
        o_ref[...] = r * lax.rsqrt(jnp.mean(r * r, axis=-1, keepdims=True) + RMS_EPS) * gf_ref[...]

        @pl.when(i == n - 1)
        def _():
            wait_gather(1 - slot)

    for parity in range(2):
        pl.when(i % 2 == parity)(functools.partial(step, parity))


def _combine(y, dest, r1, tg, p, wpg, wpp, g_final, tm):
    t, d = r1.shape
    nt = t // tm
    row = lambda w_: pl.BlockSpec((tm, w_), lambda i: (i, 0))
    return pl.pallas_call(
        _combine_kernel, grid=(nt,),
        in_specs=[pl.BlockSpec((tm * TOP_K,), lambda i: (i,), memory_space=pltpu.SMEM),
                  pl.BlockSpec((tm * TOP_K,), lambda i: (jnp.minimum(i + 1, nt - 1),), memory_space=pltpu.SMEM),
                  pl.BlockSpec(memory_space=pl.ANY),
                  row(d), row(TOP_K), row(p.shape[1]),
                  _const_spec(wpg.shape), _const_spec(wpp.shape), _const_spec((1, d))],
        out_specs=row(d),
        out_shape=jax.ShapeDtypeStruct((t, d), F32),
        scratch_shapes=[pltpu.VMEM((2, TOP_K, tm, d), F32), pltpu.SemaphoreType.DMA((2,))],
        compiler_params=_cparams(("arbitrary",)), name="moe_combine",
    )(dest, dest, y, r1, tg, p, wpg, wpp, g_final.reshape(1, d))


def _ffn(groups, wts):
    mixed, counts = [], jnp.zeros((1, N_EXPERTS), F32)
    for x, om, on8, _, tm in groups:
        *outs, counts = _mix(x, om, on8, wts["wom"], wts["won8"], wts["g_ffn"], wts["w_router"], wts["b_router"],
                             counts, tm)
        mixed.append(outs)
    dest, blk_e, n_used, n_blocks = _route(jnp.concatenate([m[2] for m in mixed], axis=0),
                                           jnp.concatenate([m[4] for m in mixed], axis=0), counts, TM_E)
    xs = jnp.zeros((n_blocks * TM_E, groups[0][0].shape[1]), F32)
    dests, start = [], 0
    for (x, _, _, _, tm), m in zip(groups, mixed):
        n = x.shape[0] * TOP_K
        dests.append(dest[start:start + n])
        xs = _dispatch(m[1], dests[-1], xs, min(2 * tm, x.shape[0]))
        start += n
    y = _experts(xs, blk_e, n_used, wts["wgu"], wts["bgu"], wts["wd"], wts["bd"])
    return [_combine(y, d_g, m[0], m[3], p, wts["wpg"], wts["wpp"], wts["g_final"], tm)
            for (_, _, _, p, tm), m, d_g in zip(groups, mixed, dests)]


KMEAN_PAGES = 16
PAGES_PER_MOBA = MOBA_BLOCK // PAGE_SIZE
SEL_PER_PAGE = PAGE_SIZE // SEL_BLOCK


def _kmean_kernel(pt_ref, *refs):
    x_refs, o_ref = refs[:KMEAN_PAGES], refs[KMEAN_PAGES]
    i = pl.program_id(1)
    nb_step = KMEAN_PAGES // PAGES_PER_MOBA

    @pl.when(i == 0)
    def _():
        o_ref[...] = jnp.zeros_like(o_ref)

    blk_lane = lax.broadcasted_iota(jnp.int32, o_ref.shape, 2)
    acc = o_ref[...]
    for blk in range(nb_step):
        s = x_refs[blk * PAGES_PER_MOBA][...]
        for k in range(1, PAGES_PER_MOBA):
            s = s + x_refs[blk * PAGES_PER_MOBA + k][...]
        col = jnp.sum(s, axis=-1, keepdims=True) * (1.0 / MOBA_BLOCK)
        acc = jnp.where(blk_lane == i * nb_step + blk, col, acc)
    o_ref[...] = acc


def _moba_kmean(cache_t, page_table):
    b, n_pages = page_table.shape
    assert n_pages % KMEAN_PAGES == 0
    _, _, h, d, ps = cache_t.shape

    def page_spec(k):
        return pl.BlockSpec((None, None, h, d, ps), lambda bi, i, pt: (0, pt[bi, i * KMEAN_PAGES + k], 0, 0, 0))

    n_blk = n_pages // PAGES_PER_MOBA
    grid_spec = pltpu.PrefetchScalarGridSpec(
        num_scalar_prefetch=1, grid=(b, n_pages // KMEAN_PAGES),
        in_specs=[page_spec(k) for k in range(KMEAN_PAGES)],
        out_specs=pl.BlockSpec((None, h, d, n_blk), lambda bi, i, pt: (bi, 0, 0, 0)))
    return pl.pallas_call(
        _kmean_kernel, grid_spec=grid_spec,
        out_shape=jax.ShapeDtypeStruct((b, h, d, n_blk), F32),
        compiler_params=_cparams(("parallel", "arbitrary")), name="moba_kmean",
    )(page_table, *([cache_t] * KMEAN_PAGES))


def _dec_select_kernel(qm_ref, kmean_ref, qn_ref, kc_ref, vc_ref, mid_ref, oc_ref, pcg_ref, *, n_cmp):
    for hd in range(8):
        gate = jnp.sum(kmean_ref[hd] * qm_ref[hd], axis=0, keepdims=True)
        blk = lax.broadcasted_iota(jnp.int32, gate.shape, 1)
        _, ids = _topk_mask(gate, blk >= 0, MOBA_TOPK, blk, 1)
        for k in range(MOBA_TOPK):
            mid_ref[k:k + 1, hd:hd + 1] = ids[k]
    lc = lax.dot_general(qn_ref[...], kc_ref[...], NT, preferred_element_type=F32)
    ok = lax.broadcasted_iota(jnp.int32, lc.shape, 1) < n_cmp
    mc = jnp.max(jnp.where(ok, lc, NEG), axis=1, keepdims=True)
    pc = jnp.where(ok, jnp.exp(lc - mc), 0.0)
    lsum = jnp.sum(pc, axis=1, keepdims=True)
    pc = pc / jnp.where(lsum > 0.0, lsum, 1.0)
    oc_ref[...] = jnp.dot(pc.astype(BF16), vc_ref[...], preferred_element_type=F32)
    for g in range(2):
        pcg_ref[g:g + 1, :] = jnp.sum(pc[4 * g:4 * g + 4], axis=0, keepdims=True)


def _dec_select(qm_col, kmean_t, qn8, kcmp, vcmp, n_cmp):
    b, _, _, n_blk = kmean_t.shape
    n_cp = kcmp.shape[1]
    per_b = lambda *s: pl.BlockSpec((None,) + s, lambda bi: (bi,) + (0,) * len(s))
    return pl.pallas_call(
        functools.partial(_dec_select_kernel, n_cmp=n_cmp), grid=(b,),
        in_specs=[per_b(8, HEAD_DIM, 1), per_b(8, HEAD_DIM, n_blk), per_b(8, LANES), per_b(n_cp, LANES),
                  per_b(n_cp, LANES)],
        out_specs=[per_b(MOBA_TOPK, 8), per_b(8, LANES), per_b(2, n_cp)],
        out_shape=[jax.ShapeDtypeStruct((b, MOBA_TOPK, 8), jnp.int32), jax.ShapeDtypeStruct((b, 8, LANES), F32),
                   jax.ShapeDtypeStruct((b, 2, n_cp), F32)],
        compiler_params=_cparams(("parallel",)), name="dec_select",
    )(qm_col, kmean_t, qn8, kcmp, vcmp)


def _dec_topn_kernel(pcg_ref, ov_ref, o_ref, *, pos, n_sel):
    imp = jnp.dot(pcg_ref[...], ov_ref[...], precision=HI, preferred_element_type=F32)
    sid = lax.broadcasted_iota(jnp.int32, imp.shape, 1)
    cur = pos // SEL_BLOCK
    forced = (sid == 0) | (sid == cur) | (sid == cur - 1)
    valid = (sid * SEL_BLOCK <= pos) & (sid < n_sel)
    n_top = min(SEL_TOPN, n_sel)
    _, ids = _topk_mask(jnp.where(forced, FORCE, imp), valid, n_top, sid, 1)
    k_id = lax.broadcasted_iota(jnp.int32, o_ref.shape, 1)
    out = jnp.full(o_ref.shape, imp.shape[1], jnp.int32)
    for k in range(n_top):
        out = jnp.where(k_id == k, ids[k], out)
    o_ref[...] = out


def _dec_topn(pcg, pos, n_sel):
    rows, n_cp = pcg.shape
    n_sp = -(-n_sel // LANES) * LANES
    ov = jnp.asarray(_overlap_matrix(n_cp, n_sp))
    return pl.pallas_call(
        functools.partial(_dec_topn_kernel, pos=pos, n_sel=n_sel),
        out_shape=jax.ShapeDtypeStruct((rows, SEL_TOPN), jnp.int32), name="dec_topn",
        compiler_params=pltpu.CompilerParams(vmem_limit_bytes=VMEM_LIMIT),
    )(pcg, ov)


def _attend_t(s_self, v_self, logit_tiles, value_tiles_t):
    m = s_self
    for s in logit_tiles:
        m = jnp.maximum(m, jnp.max(s, axis=1, keepdims=True))
    p_self = jnp.exp(s_self - m)
    l = p_self
    acc = p_self * v_self
    for s, vt in zip(logit_tiles, value_tiles_t):
        p = jnp.exp(s - m)
        l = l + jnp.sum(p, axis=1, keepdims=True)
        acc = acc + lax.dot_general(p.astype(BF16), vt, NT, preferred_element_type=F32)
    return acc / l


def _dec_attn_kernel(pt_ref, mid_ref, sid_ref, qm_ref, kmn_ref, vmn_ref, qn_ref, new_ref, gt_ref, oc_ref,
                     wk_ref, wv_ref, pbm_ref, pbn_ref, wb_ref, b0_ref,
                     mk_hbm, mv_hbm, sk_hbm, sv_hbm, om_ref, on_ref,
                     kmbuf, vmbuf, ksbuf, vsbuf, sem, *, n_blk, n_selp):
    bi = pl.program_id(0)
    n_top = ksbuf.shape[1]

    def moba_copies(hd, k):
        blk = jnp.minimum(mid_ref[bi, k * 8 + hd], n_blk - 1)
        out = []
        for half in range(PAGES_PER_MOBA):
            page = pt_ref[bi, blk * PAGES_PER_MOBA + half]
            out.append(pltpu.make_async_copy(mk_hbm.at[0, page, hd], kmbuf.at[hd, k, half], sem.at[0]))
            out.append(pltpu.make_async_copy(mv_hbm.at[0, page, hd], vmbuf.at[hd, k, half], sem.at[0]))
        return out

    def sel_copies(g, k):
        s = jnp.minimum(sid_ref[bi, g * n_top + k], n_selp - 1)
        page = pt_ref[bi, s // SEL_PER_PAGE]
        return [pltpu.make_async_copy(sk_hbm.at[0, page, g], ksbuf.at[g, k], sem.at[1]),
                pltpu.make_async_copy(sv_hbm.at[0, page, g], vsbuf.at[g, k], sem.at[1])]

    copies = [c for hd in range(8) for k in range(MOBA_TOPK) for c in moba_copies(hd, k)]
    copies += [c for g in range(2) for k in range(n_top) for c in sel_copies(g, k)]
    for c in copies:
        c.start()
    for c in copies:
        c.wait()

    b0 = b0_ref[...]

    for hd in range(8):
        qf = qm_ref[hd:hd + 1, :] * SCALE
        qh = qf.astype(BF16)
        s_self = jnp.sum(qf * kmn_ref[hd:hd + 1, :], axis=1, keepdims=True) + b0[hd:hd + 1]
        ss, vv = [], []
        for k in range(MOBA_TOPK):
            raw = mid_ref[bi, k * 8 + hd]
            blk = jnp.minimum(raw, n_blk - 1)
            for half in range(PAGES_PER_MOBA):
                s = jnp.dot(qh, kmbuf[hd, k, half].astype(BF16), preferred_element_type=F32)
                s = s + pbm_ref[blk][hd:hd + 1, half * PAGE_SIZE:(half + 1) * PAGE_SIZE]
                ss.append(jnp.where(raw < n_blk, s, NEG))
                vv.append(vmbuf[hd, k, half].astype(BF16))
        om_ref[hd:hd + 1, :] = _attend_t(s_self, vmn_ref[hd:hd + 1, :], ss, vv)

    gt = gt_ref[...]
    for g in range(2):
        lanes = slice(g * HEAD_DIM, (g + 1) * HEAD_DIM)
        hs = slice(4 * g, 4 * g + 4)
        qh = qn_ref[hs, lanes]
        qf = qh.astype(F32)
        ks_new, vs_new = new_ref[0:1, lanes], new_ref[1:2, lanes]
        kw_new, vw_new = new_ref[2:3, lanes], new_ref[3:4, lanes]

        b0g = b0[8 + 4 * g:12 + 4 * g]
        row_half = lax.broadcasted_iota(jnp.int32, (4, PAGE_SIZE), 1) // SEL_BLOCK
        ss, vv = [], []
        for k in range(n_top):
            raw = sid_ref[bi, g * n_top + k]
            sblk = jnp.minimum(raw, n_selp - 1)
            s = jnp.dot(qh, ksbuf[g, k].astype(BF16), preferred_element_type=F32)
            s = s + pbn_ref[sblk // SEL_PER_PAGE][hs, :]
            ss.append(jnp.where((row_half == sblk % SEL_PER_PAGE) & (raw < n_selp), s, NEG))
            vv.append(vsbuf[g, k].astype(BF16))
        s_self = jnp.sum(qf * ks_new, axis=1, keepdims=True) + b0g
        osel = _attend_t(s_self, vs_new, ss, vv)

        sw = jnp.dot(qh, wk_ref[g].astype(BF16), preferred_element_type=F32) + wb_ref[hs, :]
        s_self = jnp.sum(qf * kw_new, axis=1, keepdims=True) + b0g
        ow = _attend_t(s_self, vw_new, [sw], [wv_ref[g].astype(BF16)])

        oc = oc_ref[hs, lanes]
        for r in range(4):
            hd = 4 * g + r
            o = (gt[:, hd * 3:hd * 3 + 1] * oc[r:r + 1] + gt[:, hd * 3 + 1:hd * 3 + 2] * osel[r:r + 1]
                 + gt[:, hd * 3 + 2:hd * 3 + 3] * ow[r:r + 1])
            on_ref[hd:hd + 1, :] = o


def _dec_attn(page_table, mids, sids, qm, kmn, vmn, qn8, new_rows, gates, oc, wk, wv, pbm, pbn, wb, b0,
              cache_mk, cache_mv, cache_sk, cache_sv):
    b = page_table.shape[0]
    n_blk = pbm.shape[0]
    n_selp = pbn.shape[0] * SEL_PER_PAGE
    n_top = sids.shape[1] // 2
    win = wk.shape[3]
    per_b = lambda *s: pl.BlockSpec((None,) + s, lambda bi, *_: (bi,) + (0,) * len(s))
    const = lambda a: pl.BlockSpec(a.shape, lambda bi, *_: (0,) * a.ndim)
    any_spec = pl.BlockSpec(memory_space=pl.ANY)
    grid_spec = pltpu.PrefetchScalarGridSpec(
        num_scalar_prefetch=3, grid=(b,),
        in_specs=[per_b(8, HEAD_DIM), per_b(8, HEAD_DIM), per_b(8, HEAD_DIM), per_b(8, LANES), per_b(4, LANES),
                  per_b(1, LANES), per_b(8, LANES), per_b(2, HEAD_DIM, win), per_b(2, HEAD_DIM, win),
                  const(pbm), const(pbn), const(wb), const(b0), any_spec, any_spec, any_spec, any_spec],
        out_specs=[per_b(8, HEAD_DIM), per_b(8, HEAD_DIM)],
        scratch_shapes=[pltpu.VMEM((8, MOBA_TOPK, PAGES_PER_MOBA, HEAD_DIM, PAGE_SIZE), F32),
                        pltpu.VMEM((8, MOBA_TOPK, PAGES_PER_MOBA, HEAD_DIM, PAGE_SIZE), F32),
                        pltpu.VMEM((2, n_top, HEAD_DIM, PAGE_SIZE), F32),
                        pltpu.VMEM((2, n_top, HEAD_DIM, PAGE_SIZE), F32),
                        pltpu.SemaphoreType.DMA((2,))])
    return pl.pallas_call(
        functools.partial(_dec_attn_kernel, n_blk=n_blk, n_selp=n_selp), grid_spec=grid_spec,
        out_shape=[jax.ShapeDtypeStruct((b, 8, HEAD_DIM), F32), jax.ShapeDtypeStruct((b, 8, HEAD_DIM), F32)],
        compiler_params=_cparams(("arbitrary",)), name="dec_attn",
    )(page_table, mids, sids, qm, kmn, vmn, qn8, new_rows, gates, oc, wk, wv, pbm, pbn, wb, b0,
      cache_mk, cache_mv, cache_sk, cache_sv)


def kernel(x_prompt, x_sample, p_prompt, p_sample, cache_moba_k, cache_moba_v, cache_nsa_cmp_k, cache_nsa_cmp_v, cache_nsa_sel_k, cache_nsa_sel_v, state_nsa_win_k, state_nsa_win_v, page_table, rel_bias, g_mix, w_in, w_out, cmp_w1_k, cmp_b1_k, cmp_w2_k, cmp_pos_k, cmp_w1_v, cmp_b1_v, cmp_w2_v, cmp_pos_v, g_ffn, w_router, b_router, w_gate_up, b_gate_up, w_down, b_down, w_ple_proj, w_ple_gate, g_final):
    assert rel_bias.shape == (N_BUCKETS, 16) and g_mix.shape[0] == 1, "one layer, 8 MoBA + 8 NSA heads"
    b, t, d = x_prompt.shape
    bd, dec_seq, _ = x_sample.shape
    assert dec_seq == 1 and t % (2 * MOBA_BLOCK) == 0
    n_pages = page_table.shape[1]
    past = n_pages * PAGE_SIZE
    win = state_nsa_win_k.shape[2]

    bv = rel_bias[_t5_bucket(jnp.arange(MAX_DISTANCE + 1))]
    n_tbm = -(-(MAX_DISTANCE + MOBA_BLOCK - 1) // TQ_M) + 1
    tbm = _toeplitz_tiles(bv[:, :8], TQ_M, MOBA_BLOCK, n_tbm, True)
    n_tbn = -(-(MAX_DISTANCE + TK_N - 1) // TQ_N) + 1
    tbn = _toeplitz_tiles(bv[:, 8:], TQ_N, TK_N, n_tbn, False)
    n_wm = -(-(WINDOW + TK_N) // TQ_N)
    dist = np.stack([TQ_N * m + np.arange(TQ_N)[:, None] - np.arange(TK_N)[None, :] for m in range(n_wm)])
    wm = jnp.asarray(np.where(dist <= WINDOW, 0.0, NEG).astype(np.float32))
    w_proj = _proj_weight(w_in[0])
    wo = w_out[0]
    won = wo[512:].reshape(NSA_HEADS, HEAD_DIM, d)
    zn = jnp.zeros_like(won)
    grp0 = (jnp.arange(NSA_HEADS) < 4)[:, None, None]
    wts = dict(
        wom=wo[:512].astype(BF16),
        won8=jnp.concatenate([jnp.where(grp0, won, zn), jnp.where(grp0, zn, won)], axis=1).astype(BF16),
        g_ffn=g_ffn[0], w_router=w_router[0], b_router=b_router[0],
        wgu=w_gate_up[0], bgu=b_gate_up[0][:, None, :], wd=w_down[0], bd=b_down[0][:, None, :],
        wpg=w_ple_gate[0].astype(BF16), wpp=w_ple_proj[0].astype(BF16), g_final=g_final)
    ck = _cmp_const(cmp_pos_k[0], cmp_w1_k[0], cmp_b1_k[0])
    cv = _cmp_const(cmp_pos_v[0], cmp_w1_v[0], cmp_b1_v[0])

    def pad_heads(o):
        z = jnp.zeros_like(o)
        return jnp.concatenate([jnp.where(grp0, o, z), jnp.where(grp0, z, o)], axis=-1).astype(BF16)

    xp = x_prompt.reshape(b * t, d)
    qm, km, vm, kmb, vmb, qn8, kv, kvb, gt, kmean = _project(xp, g_mix, w_proj, 2 * MOBA_BLOCK, True)
    kmean = jnp.pad(kmean.reshape(b, t // MOBA_BLOCK, 512), ((0, 0), (0, LANES - t // MOBA_BLOCK), (0, 0)))
    om = _moba_prompt(qm.reshape(b, t, 512), kmb.reshape(b, t, 512), vmb.reshape(b, t, 512), kmean, tbm)
    pt_p = jnp.arange(b * t // PAGE_SIZE, dtype=jnp.int32).reshape(b, t // PAGE_SIZE)
    pages_t = lambda a: jnp.transpose(a.reshape(-1, PAGE_SIZE, 2, HEAD_DIM), (0, 2, 3, 1))
    kcmp = _compress(_chunk_layout(pages_t(kv[0])), pt_p, cmp_w1_k[0], ck, cmp_w2_k[0])
    vcmp = _compress(_chunk_layout(pages_t(kv[1])), pt_p, cmp_w1_v[0], cv, cmp_w2_v[0])
    seq = lambda a: a.reshape(b, t, LANES)
    on8 = _nsa_prompt(qn8, gt, kcmp, vcmp, seq(kvb[2]), seq(kvb[3]), seq(kvb[4]), seq(kvb[5]), tbn, wm,
                      (t - CMP_LEN) // CMP_STRIDE + 1)

    xs = x_sample.reshape(bd, d)
    qm_s, km_s, vm_s, _, _, qn8_s, kv_s, _, gt_s = _project(xs, g_mix, w_proj, bd, False)
    rows_minor = lambda a: jnp.transpose(a, (0, 1, 3, 4, 2))
    kmean_s = _moba_kmean(rows_minor(cache_moba_k), page_table)
    kcmp_s = _compress(_chunk_layout(rows_minor(cache_nsa_cmp_k)[0]), page_table, cmp_w1_k[0], ck, cmp_w2_k[0])
    vcmp_s = _compress(_chunk_layout(rows_minor(cache_nsa_cmp_v)[0]), page_table, cmp_w1_v[0], cv, cmp_w2_v[0])
    heads = lambda a: a.reshape(bd, 8, HEAD_DIM)
    qn8_sb = jnp.transpose(qn8_s, (1, 0, 2))
    mids, oc_s, pcg = _dec_select(qm_s.reshape(bd, 8, HEAD_DIM, 1), kmean_s, qn8_sb, kcmp_s, vcmp_s,
                                  (past + 1 - CMP_LEN) // CMP_STRIDE + 1)
    n_sel_s = -(-(past + 1) // SEL_BLOCK)
    sids = _dec_topn(pcg.reshape(bd * 2, -1), past, n_sel_s)
    n_blk = past // MOBA_BLOCK
    kpos_m = np.arange(n_blk * MOBA_BLOCK).reshape(n_blk, MOBA_BLOCK)
    pbm = jnp.transpose(jnp.take(bv[:, :8], jnp.asarray(np.minimum(past - kpos_m, MAX_DISTANCE)), axis=0), (0, 2, 1))
    kpos_n = np.arange(past).reshape(n_pages, PAGE_SIZE)
    pbn = jnp.transpose(jnp.take(bv[:, 8:], jnp.asarray(np.minimum(past - kpos_n, MAX_DISTANCE)), axis=0), (0, 2, 1))
    wb = jnp.take(bv[:, 8:], jnp.asarray(np.minimum(win - np.arange(win), MAX_DISTANCE)), axis=0).T
    b0 = bv[0][:, None]
    new_rows = jnp.stack([kv_s[2], kv_s[3], kv_s[4], kv_s[5]], axis=1)
    om_s, on_s = _dec_attn(page_table, mids.reshape(bd, -1), sids.reshape(bd, -1), heads(qm_s), heads(km_s),
                           heads(vm_s), qn8_sb, new_rows, gt_s.reshape(bd, 1, LANES), oc_s,
                           rows_minor(state_nsa_win_k)[0], rows_minor(state_nsa_win_v)[0],
                           pbm, pbn, wb, b0, rows_minor(cache_moba_k), rows_minor(cache_moba_v),
                           rows_minor(cache_nsa_sel_k), rows_minor(cache_nsa_sel_v))
    y_p, y_s = _ffn([(xp, om.reshape(b * t, 512), on8, p_prompt[0].reshape(b * t, -1), 256),
                     (xs, om_s.reshape(bd, 512).astype(BF16), pad_heads(jnp.transpose(on_s, (1, 0, 2))),
                      p_sample[0].reshape(bd, -1), bd)], wts)

    wp = min(WINDOW, t)
    mh = lambda a, n, h: a.reshape(1, n, -1, h, HEAD_DIM)
    kw_new = kv_s[4].reshape(bd, 1, 2, HEAD_DIM)
    vw_new = kv_s[5].reshape(bd, 1, 2, HEAD_DIM)
    win_k = jnp.concatenate([state_nsa_win_k[0], kw_new], axis=1)[:, -win:][None]
    win_v = jnp.concatenate([state_nsa_win_v[0], vw_new], axis=1)[:, -win:][None]
    return (y_p.reshape(b, t, d), y_s.reshape(bd, 1, d),
            mh(km, b, 8), mh(vm, b, 8), mh(kv[0], b, 2), mh(kv[1], b, 2), mh(kv[2], b, 2), mh(kv[3], b, 2),
            mh(kv[4], b, 2)[:, :, -wp:], mh(kv[5], b, 2)[:, :, -wp:],
            mh(km_s, bd, 8), mh(vm_s, bd, 8), mh(kv_s[0], bd, 2), mh(kv_s[1], bd, 2), mh(kv_s[2], bd, 2),
            mh(kv_s[3], bd, 2), win_k, win_v)
```

```python
import functools
import math

import numpy as np
import jax
import jax.numpy as jnp
from jax import lax
from jax.experimental import pallas as pl
from jax.experimental.pallas import tpu as pltpu

PAGE_SIZE = 128
HEAD_DIM = 64
MOBA_BLOCK = 256
MOBA_TOPK = 3
CMP_LEN = 32
CMP_STRIDE = 16
CMP_HIDDEN = 256
SEL_BLOCK = 64
SEL_TOPN = 16
WINDOW = 512
N_BUCKETS = 32
MAX_DISTANCE = 1024
N_EXPERTS = 32
TOP_K = 4
SWIGLU_LIMIT = 7.0
SWIGLU_ALPHA = 1.702
RMS_EPS = 1e-6
NEG = -1e30
FORCE = 1e30
SCALE = HEAD_DIM ** -0.5
LANES = 128
VMEM_LIMIT = 56 * 1024 * 1024

F32 = jnp.float32
BF16 = jnp.bfloat16
HI = lax.Precision.HIGHEST
NT = (((1,), (1,)), ((), ()))


def _cparams(sem):
    return pltpu.CompilerParams(dimension_semantics=sem, vmem_limit_bytes=VMEM_LIMIT)


def _const_spec(shape):
    n = len(shape)
    return pl.BlockSpec(shape, lambda *_: (0,) * n)


def _t5_bucket(dist):
    max_exact = N_BUCKETS // 2
    d = jnp.maximum(dist, 0)
    df = jnp.maximum(d, max_exact).astype(jnp.float32)
    large = max_exact + (jnp.log(df / max_exact) / math.log(MAX_DISTANCE / max_exact)
                         * (N_BUCKETS - max_exact)).astype(jnp.int32)
    return jnp.where(d < max_exact, d, jnp.minimum(large, N_BUCKETS - 1))


def _topk_mask(score, valid, k, idx_iota, axis):
    n = float(score.shape[axis])
    low = -3e38
    pos_f = idx_iota.astype(F32)
    work = jnp.where(valid, score, low)
    sel = jnp.zeros(score.shape, F32)
    ids = []
    for _ in range(k):
        mx = jnp.max(work, axis=axis, keepdims=True)
        cand = (work == mx) & (mx > low)
        idx = jnp.min(jnp.where(cand, pos_f, n), axis=axis, keepdims=True)
        hit = pos_f == idx
        sel = jnp.where(hit, 1.0, sel)
        work = jnp.where(hit, low, work)
        ids.append(idx.astype(jnp.int32))
    return sel, ids


C_QM, C_KM, C_VM, C_QN, C_KV, C_GT, C_END = 0, 512, 1024, 1536, 2560, 3328, 3456


def _proj_weight(w_in):
    d = w_in.shape[0]
    qn = w_in[:, 1536:2048].reshape(d, 8, HEAD_DIM)
    z = jnp.zeros((d, 8, HEAD_DIM), w_in.dtype)
    grp0 = (jnp.arange(8) < 4)[None, :, None]
    qn_pad = jnp.concatenate([jnp.where(grp0, qn, z), jnp.where(grp0, z, qn)], axis=-1).reshape(d, 1024)
    gates = jnp.pad(w_in[:, 2816:2840], ((0, 0), (0, LANES - 24)))
    return jnp.concatenate([w_in[:, :1536], qn_pad, w_in[:, 2048:2816], gates], axis=1).astype(BF16)


def _proj_kernel(x_ref, g_ref, w_ref, qm_ref, km_ref, vm_ref, kmb_ref, vmb_ref, qn_ref,
                 kv_ref, kvb_ref, gt_ref, *rest, with_kmean):
    x = x_ref[...]
    h = x * lax.rsqrt(jnp.mean(x * x, axis=-1, keepdims=True) + RMS_EPS) * g_ref[...]
    z = jnp.dot(h.astype(BF16), w_ref[...], preferred_element_type=F32)
    qm_ref[...] = z[:, C_QM:C_KM]
    km = z[:, C_KM:C_VM]
    vm = z[:, C_VM:C_QN]
    km_ref[...] = km
    vm_ref[...] = vm
    kmb_ref[...] = km.astype(BF16)
    vmb_ref[...] = vm.astype(BF16)
    for hd in range(8):
        qn_ref[hd] = (z[:, C_QN + hd * LANES:C_QN + (hd + 1) * LANES] * SCALE).astype(BF16)
    for j in range(6):
        blk = z[:, C_KV + j * LANES:C_KV + (j + 1) * LANES]
        kv_ref[j] = blk
        kvb_ref[j] = blk.astype(BF16)
    gt_ref[...] = jax.nn.sigmoid(z[:, C_GT:C_END])
    if with_kmean:
        (kmean_ref,) = rest
        nblk = km.shape[0] // MOBA_BLOCK
        for i in range(nblk):
            kmean_ref[0, i:i + 1, :] = jnp.mean(km[i * MOBA_BLOCK:(i + 1) * MOBA_BLOCK], axis=0, keepdims=True)


def _project(x, g, w, tm, with_kmean):
    t, d = x.shape
    nt = t // tm
    row = lambda w_: pl.BlockSpec((tm, w_), lambda i: (i, 0))
    out_shape = [
        jax.ShapeDtypeStruct((t, 512), F32), jax.ShapeDtypeStruct((t, 512), F32), jax.ShapeDtypeStruct((t, 512), F32),
        jax.ShapeDtypeStruct((t, 512), BF16), jax.ShapeDtypeStruct((t, 512), BF16),
        jax.ShapeDtypeStruct((8, t, LANES), BF16),
        jax.ShapeDtypeStruct((6, t, LANES), F32), jax.ShapeDtypeStruct((6, t, LANES), BF16),
        jax.ShapeDtypeStruct((t, LANES), F32),
    ]
    out_specs = [row(512), row(512), row(512), row(512), row(512),
                 pl.BlockSpec((8, tm, LANES), lambda i: (0, i, 0)),
                 pl.BlockSpec((6, tm, LANES), lambda i: (0, i, 0)),
                 pl.BlockSpec((6, tm, LANES), lambda i: (0, i, 0)),
                 row(LANES)]
    if with_kmean:
        nb = tm // MOBA_BLOCK
        out_shape.append(jax.ShapeDtypeStruct((nt, nb, 512), F32))
        out_specs.append(pl.BlockSpec((1, nb, 512), lambda i: (i, 0, 0)))
    return pl.pallas_call(
        functools.partial(_proj_kernel, with_kmean=with_kmean),
        grid=(nt,),
        in_specs=[row(d), _const_spec((1, d)), _const_spec(w.shape)],
        out_specs=out_specs, out_shape=out_shape,
        compiler_params=_cparams(("parallel",)), name="proj",
    )(x, g, w)


def _toeplitz_kernel(v_ref, o_ref):
    rows, cols = o_ref.shape[1:]
    x = jnp.broadcast_to(v_ref[0], (rows, v_ref.shape[2]))
    o_ref[0] = pltpu.roll(x, 0, 1, stride=1, stride_axis=0)[:, :cols]


def _toeplitz_tiles(bv, rows, cols, n_tiles, head_major):
    length = rows + cols
    assert length % LANES == 0
    k = np.arange(length)
    off = np.where(k < cols, -k, length - k)
    dist = np.stack([rows * m + off for m in range(n_tiles)])
    v = jnp.take(bv, jnp.asarray(np.clip(dist, 0, MAX_DISTANCE)), axis=0)
    v = jnp.where(jnp.asarray(dist >= 0)[..., None], v, NEG)
    v = jnp.transpose(v, (2, 0, 1) if head_major else (0, 2, 1))
    lead = v.shape[:2]
    n = lead[0] * lead[1]
    tiles = pl.pallas_call(
        _toeplitz_kernel, grid=(n,),
        in_specs=[pl.BlockSpec((1, 1, length), lambda i: (i, 0, 0))],
        out_specs=pl.BlockSpec((1, rows, cols), lambda i: (i, 0, 0)),
        out_shape=jax.ShapeDtypeStruct((n, rows, cols), F32),
        compiler_params=_cparams(("parallel",)), name="bias_tiles",
    )(v.reshape(n, 1, length))
    return tiles.reshape(lead + (rows, cols))


TQ_M = MOBA_BLOCK


def _online_attention(lo, hi, logits, values, m_ref, l_ref, acc_ref, group=4):
    m_ref[...] = jnp.full(m_ref.shape, -3e38, F32)
    l_ref[...] = jnp.zeros(l_ref.shape, F32)
    acc_ref[...] = jnp.zeros(acc_ref.shape, F32)

    def trip(jj, c):
        j0 = lo + group * jj
        tiles = [jnp.minimum(j0 + u, hi) for u in range(group)]
        ss = [logits(j) for j in tiles]
        tmax = None
        for s in ss:
            half = jnp.maximum(s[:, :LANES], s[:, LANES:])
            tmax = half if tmax is None else jnp.maximum(tmax, half)
        m_old = m_ref[...]
        m_new = jnp.maximum(m_old, jnp.max(tmax, axis=1, keepdims=True))
        alpha = jnp.exp(m_old - m_new)
        l_sum = alpha * l_ref[...]
        pv_sum = alpha * acc_ref[...]
        for u, (j, s) in enumerate(zip(tiles, ss)):
            pa = jnp.exp(s[:, :LANES] - m_new)
            pb = jnp.exp(s[:, LANES:] - m_new)
            pv = jnp.dot(jnp.concatenate([pa.astype(BF16), pb.astype(BF16)], axis=1), values(j),
                         preferred_element_type=F32)
            if u == 0:
                l_sum, pv_sum = l_sum + (pa + pb), pv_sum + pv
            else:
                w = (j0 + u <= hi).astype(F32)
                l_sum, pv_sum = l_sum + w * (pa + pb), pv_sum + w * pv
        m_ref[...] = m_new
        l_ref[...] = l_sum
        acc_ref[...] = pv_sum
        return c

    lax.fori_loop(0, (hi - lo + group) // group, trip, 0)
    return acc_ref[...] / jnp.sum(l_ref[...], axis=1, keepdims=True)


def _moba_kernel(q_ref, k_ref, v_ref, kmean_ref, tb_ref, o_ref, rm_ref, l_ref, acc_ref, *, n_tb):
    qi = pl.program_id(2)
    q2 = q_ref[0]
    lane = lax.broadcasted_iota(jnp.int32, q2.shape, 1)
    q_st = jnp.concatenate([jnp.where(lane < HEAD_DIM, q2, 0.0), jnp.where(lane >= HEAD_DIM, q2, 0.0)], axis=0)
    rows = 2 * TQ_M
    gate_t = lax.dot_general(kmean_ref[0], q_st, NT, precision=HI, preferred_element_type=F32)
    blk = lax.broadcasted_iota(jnp.int32, gate_t.shape, 0)
    sel_t, _ = _topk_mask(gate_t, blk < qi, MOBA_TOPK, blk, 0)
    selneg = jnp.where((sel_t > 0.5) | (blk == qi), 0.0, NEG).T
    q_aug = jnp.concatenate([(q_st * SCALE).astype(BF16), selneg.astype(BF16)], axis=1)
    key_lane = lax.broadcasted_iota(jnp.int32, (MOBA_BLOCK, LANES), 1)

    def kv_rows(ref, j):
        return ref[0, pl.ds(pl.multiple_of(j * MOBA_BLOCK, MOBA_BLOCK), MOBA_BLOCK), :]

    def logits(j):
        k_aug = jnp.concatenate([kv_rows(k_ref, j), (key_lane == j).astype(BF16)], axis=1)
        s = lax.dot_general(q_aug, k_aug, NT, preferred_element_type=F32)
        m_id = jnp.minimum(qi - j, n_tb - 1)
        return s + jnp.concatenate([tb_ref[0, m_id], tb_ref[1, m_id]], axis=0)

    o = _online_attention(0, qi, logits, lambda j: kv_rows(v_ref, j), rm_ref, l_ref, acc_ref)
    o_ref[0] = jnp.where(lane < HEAD_DIM, o[:TQ_M], o[TQ_M:]).astype(o_ref.dtype)


def _moba_prompt(qm, kmb, vmb, kmean, tb):
    b, t, _ = qm.shape
    nb = kmean.shape[1]
    assert t // MOBA_BLOCK <= nb == LANES
    n_tb = tb.shape[1]
    return pl.pallas_call(
        functools.partial(_moba_kernel, n_tb=n_tb),
        grid=(4, b, t // TQ_M),
        in_specs=[pl.BlockSpec((1, TQ_M, LANES), lambda p, bi, qi: (bi, qi, p)),
                  pl.BlockSpec((1, t, LANES), lambda p, bi, qi: (bi, 0, p)),
                  pl.BlockSpec((1, t, LANES), lambda p, bi, qi: (bi, 0, p)),
                  pl.BlockSpec((1, nb, LANES), lambda p, bi, qi: (bi, 0, p)),
                  pl.BlockSpec((2, n_tb, TQ_M, MOBA_BLOCK), lambda p, bi, qi: (p, 0, 0, 0))],
        out_specs=pl.BlockSpec((1, TQ_M, LANES), lambda p, bi, qi: (bi, qi, p)),
        out_shape=jax.ShapeDtypeStruct((b, t, 512), BF16),
        scratch_shapes=[pltpu.VMEM((2 * TQ_M, LANES), F32)] * 3,
        compiler_params=_cparams(("parallel", "parallel", "parallel")), name="moba_prompt",
    )(qm, kmb, vmb, kmean, tb)


CHUNKS_PER_PAGE = PAGE_SIZE // CMP_STRIDE
CMP_PAGES = 32


def _chunk_layout(pages_t):
    n = pages_t.shape[0]
    x = pages_t.reshape(n, 2, HEAD_DIM, CHUNKS_PER_PAGE, CMP_STRIDE)
    return jnp.transpose(x, (0, 1, 3, 2, 4)).reshape(n, 2 * CHUNKS_PER_PAGE, HEAD_DIM * CMP_STRIDE).astype(BF16)


def _chunk_weight(w_half):
    h = w_half.shape[1]
    return jnp.transpose(w_half.reshape(CMP_STRIDE, HEAD_DIM, h), (1, 0, 2)).reshape(CMP_STRIDE * HEAD_DIM, h)


def _cmp_const_kernel(pe_ref, w1_ref, b1_ref, o_ref):
    o_ref[...] = jnp.sum(w1_ref[...] * pe_ref[...], axis=0, keepdims=True) + b1_ref[...]


def _cmp_const(pe, w1, b1):
    return pl.pallas_call(
        _cmp_const_kernel, out_shape=jax.ShapeDtypeStruct((1, CMP_HIDDEN), F32), name="cmp_const",
    )(pe.reshape(CMP_LEN * HEAD_DIM, 1), w1, b1.reshape(1, CMP_HIDDEN))


def _gelu_tanh(x):
    return 0.5 * x * (1.0 + jnp.tanh(math.sqrt(2.0 / math.pi) * (x + 0.044715 * x * x * x)))


def _compress_kernel(pt_ref, *refs, n_pg):
    x_refs = refs[:n_pg + 1]
    wab_ref, c_ref, w2_ref, o_ref = refs[n_pg + 1:]
    rows_pp = 2 * CHUNKS_PER_PAGE
    x = jnp.concatenate([r[0] for r in x_refs], axis=0)
    ab = jnp.dot(x, wab_ref[...], preferred_element_type=F32)
    a = ab[:n_pg * rows_pp, :CMP_HIDDEN]
    bm = ab[:, CMP_HIDDEN:]
    n = bm.shape[0]
    nxt = pltpu.roll(bm, n - 1, 0)
    nxt_page = pltpu.roll(bm, n - (rows_pp - CHUNKS_PER_PAGE + 1), 0)
    c_id = lax.broadcasted_iota(jnp.int32, bm.shape, 0) % CHUNKS_PER_PAGE
    b_next = jnp.where(c_id == CHUNKS_PER_PAGE - 1, nxt_page, nxt)[:n_pg * rows_pp]
    hid = _gelu_tanh(a + b_next + c_ref[...]).astype(BF16)
    hid = hid.reshape(n_pg, 2, CHUNKS_PER_PAGE, CMP_HIDDEN)
    out = None
    for g in range(2):
        hg = hid[:, g].reshape(n_pg * CHUNKS_PER_PAGE, CMP_HIDDEN)
        og = jnp.dot(hg, w2_ref[g], preferred_element_type=F32)
        out = og if out is None else out + og
    o_ref[0] = out.astype(o_ref.dtype)


def _compress(xc, page_table, w1, const, w2):
    b, n_pages = page_table.shape
    n_pg = math.gcd(n_pages, CMP_PAGES)
    half = CMP_STRIDE * HEAD_DIM
    wab = jnp.concatenate([_chunk_weight(w1[:half]), _chunk_weight(w1[half:])], axis=1).astype(BF16)
    z = jnp.zeros_like(w2)
    w2p = jnp.stack([jnp.concatenate([w2, z], axis=1), jnp.concatenate([z, w2], axis=1)]).astype(BF16)

    def page_spec(k):
        def imap(bi, i, pt):
            return (pt[bi, jnp.minimum(i * n_pg + k, n_pages - 1)], 0, 0)
        return pl.BlockSpec((1, 2 * CHUNKS_PER_PAGE, half), imap)

    tile = n_pg * CHUNKS_PER_PAGE
    grid_spec = pltpu.PrefetchScalarGridSpec(
        num_scalar_prefetch=1, grid=(b, n_pages // n_pg),
        in_specs=[page_spec(k) for k in range(n_pg + 1)] + [
            pl.BlockSpec(wab.shape, lambda bi, i, pt: (0, 0)),
            pl.BlockSpec(const.shape, lambda bi, i, pt: (0, 0)),
            pl.BlockSpec(w2p.shape, lambda bi, i, pt: (0, 0, 0))],
        out_specs=pl.BlockSpec((1, tile, LANES), lambda bi, i, pt: (bi, i, 0)))
    return pl.pallas_call(
        functools.partial(_compress_kernel, n_pg=n_pg), grid_spec=grid_spec,
        out_shape=jax.ShapeDtypeStruct((b, n_pages * CHUNKS_PER_PAGE, LANES), BF16),
        compiler_params=_cparams(("parallel", "parallel")), name="compress",
    )(page_table, *([xc] * (n_pg + 1)), wab, const, w2p)


TQ_N = 128
TK_N = 256
NSA_HEADS = 8
assert WINDOW % TK_N == 0 and TK_N % TQ_N == 0 and TK_N == 2 * LANES == MOBA_BLOCK
WIN_TILES = WINDOW // TK_N + 1


def _overlap_matrix(n_cmp_pad, n_sel_pad):
    cstart = np.arange(n_cmp_pad)[:, None] * CMP_STRIDE
    sstart = np.arange(n_sel_pad)[None, :] * SEL_BLOCK
    return ((cstart < sstart + SEL_BLOCK) & (cstart + CMP_LEN > sstart)).astype(np.float32)


def _nsa_kernel(q_ref, gt_ref, kc_ref, vc_ref, ks_ref, vs_ref, kw_ref, vw_ref, ovt_ref, tb_ref, wm_ref, o_ref,
                rm_ref, l_ref, acc_ref, *, n_cmp, n_sel, n_tb):
    qi = pl.program_id(1)
    q0 = qi * TQ_N
    rows = NSA_HEADS * TQ_N
    q8 = q_ref[...].reshape(rows, LANES)
    n_cp = kc_ref.shape[1]
    pos = q0 + lax.broadcasted_iota(jnp.int32, (1, TQ_N, 1), 1)

    lc = lax.dot_general(q8, kc_ref[0], NT, preferred_element_type=F32).reshape(NSA_HEADS, TQ_N, n_cp)
    n_id = lax.broadcasted_iota(jnp.int32, (1, 1, n_cp), 2)
    ok = (n_id * CMP_STRIDE + (CMP_LEN - 1) <= pos) & (n_id < n_cmp)
    mc = jnp.max(jnp.where(ok, lc, NEG), axis=2, keepdims=True)
    pc = jnp.where(ok, jnp.exp(lc - mc), 0.0)
    lsum = jnp.sum(pc, axis=2, keepdims=True)
    pc = pc / jnp.where(lsum > 0.0, lsum, 1.0)
    oc = jnp.dot(pc.reshape(rows, n_cp).astype(BF16), vc_ref[0], preferred_element_type=F32)

    pcg = pc.reshape(2, 4, TQ_N, n_cp)
    pcg = (pcg[:, 0] + pcg[:, 1]) + (pcg[:, 2] + pcg[:, 3])
    imp_t = lax.dot_general(ovt_ref[...], pcg.reshape(2 * TQ_N, n_cp), NT, precision=HI, preferred_element_type=F32)
    n_sp = imp_t.shape[0]
    sid = lax.broadcasted_iota(jnp.int32, imp_t.shape, 0)
    pos_t = q0 + lax.broadcasted_iota(jnp.int32, imp_t.shape, 1) % TQ_N
    cur = pos_t // SEL_BLOCK
    forced = (sid == 0) | (sid == cur) | (sid == cur - 1)
    valid = (sid * SEL_BLOCK <= pos_t) & (sid < n_sel)
    sel_t, _ = _topk_mask(jnp.where(forced, FORCE, imp_t), valid, min(SEL_TOPN, n_sel), sid, 0)
    selneg = jnp.where(sel_t > 0.5, 0.0, NEG).T.astype(BF16).reshape(2, TQ_N, n_sp)
    selneg8 = jnp.broadcast_to(selneg[:, None], (2, 4, TQ_N, n_sp)).reshape(rows, n_sp)
    q_aug = jnp.concatenate([q8, selneg8], axis=1)

    jmax = q0 // TK_N
    e_lane = lax.broadcasted_iota(jnp.int32, (TK_N, n_sp), 1)
    e_blk = lax.broadcasted_iota(jnp.int32, (TK_N, n_sp), 0) // SEL_BLOCK

    def bias(j):
        return tb_ref[jnp.minimum(qi - 2 * j, n_tb - 1)].reshape(rows, TK_N)

    def kslice(ref, j):
        return ref[0, pl.ds(pl.multiple_of(j * TK_N, TK_N), TK_N), :]

    def sel_logits(j):
        k_aug = jnp.concatenate([kslice(ks_ref, j), (e_lane == e_blk + j * (TK_N // SEL_BLOCK)).astype(BF16)], axis=1)
        return lax.dot_general(q_aug, k_aug, NT, preferred_element_type=F32) + bias(j)

    osel = _online_attention(0, jmax, sel_logits, lambda j: kslice(vs_ref, j), rm_ref, l_ref, acc_ref)

    def win_logits(j):
        s = lax.dot_general(q8, kslice(kw_ref, j), NT, preferred_element_type=F32) + bias(j)
        wmask = wm_ref[jnp.minimum(qi - 2 * j, wm_ref.shape[0] - 1)]
        return (s.reshape(NSA_HEADS, TQ_N, TK_N) + wmask[None]).reshape(rows, TK_N)

    jlo = jnp.maximum(jmax - (WIN_TILES - 1), 0)
    ow = _online_attention(jlo, jmax, win_logits, lambda j: kslice(vw_ref, j), rm_ref, l_ref, acc_ref, WIN_TILES)

    gt = gt_ref[...]
    glane = lax.broadcasted_iota(jnp.int32, gt.shape, 1)

    def gate(hd, br):
        return jnp.sum(jnp.where(glane == hd * 3 + br, gt, 0.0), axis=1, keepdims=True)

    for hd in range(NSA_HEADS):
        r0 = hd * TQ_N
        o = (gate(hd, 0) * oc[r0:r0 + TQ_N] + gate(hd, 1) * osel[r0:r0 + TQ_N] + gate(hd, 2) * ow[r0:r0 + TQ_N])
        o_ref[hd] = o.astype(o_ref.dtype)


def _nsa_prompt(qn8, gates, kcmp, vcmp, ksb, vsb, kwb, vwb, tbn, wm, n_cmp):
    b, t, _ = ksb.shape
    n_cp = kcmp.shape[1]
    n_sel = t // SEL_BLOCK
    n_sp = -(-n_sel // LANES) * LANES
    ov = jnp.asarray(_overlap_matrix(n_cp, n_sp).T)
    n_tb = tbn.shape[0]
    nq = t // TQ_N
    full = lambda a: pl.BlockSpec((1,) + a.shape[1:], lambda bi, qi: (bi,) + (0,) * (a.ndim - 1))
    once = lambda a: pl.BlockSpec(a.shape, lambda bi, qi: (0,) * a.ndim, pipeline_mode=pl.Buffered(1))
    return pl.pallas_call(
        functools.partial(_nsa_kernel, n_cmp=n_cmp, n_sel=n_sel, n_tb=n_tb),
        grid=(b, nq),
        in_specs=[pl.BlockSpec((NSA_HEADS, TQ_N, LANES), lambda bi, qi: (0, bi * nq + qi, 0)),
                  pl.BlockSpec((TQ_N, LANES), lambda bi, qi: (bi * nq + qi, 0)),
                  full(kcmp), full(vcmp), full(ksb), full(vsb), full(kwb), full(vwb),
                  once(ov), once(tbn), once(wm)],
        out_specs=pl.BlockSpec((NSA_HEADS, TQ_N, LANES), lambda bi, qi: (0, bi * nq + qi, 0)),
        out_shape=jax.ShapeDtypeStruct((NSA_HEADS, b * t, LANES), BF16),
        scratch_shapes=[pltpu.VMEM((NSA_HEADS * TQ_N, LANES), F32)] * 3,
        compiler_params=_cparams(("parallel", "parallel")), name="nsa_prompt",
    )(qn8, gates, kcmp, vcmp, ksb, vsb, kwb, vwb, ov, tbn, wm)


def _mix_kernel(x_ref, om_ref, on_ref, wom_ref, won_ref, g_ref, wr_ref, br_ref, c0_ref,
                r_ref, h_ref, ti_ref, tg_ref, rk_ref, cnt_ref, carry):
    @pl.when(pl.program_id(0) == 0)
    def _():
        carry[...] = c0_ref[...]

    acc = x_ref[...] + jnp.dot(om_ref[...], wom_ref[...], preferred_element_type=F32)
    for hd in range(NSA_HEADS):
        acc = acc + jnp.dot(on_ref[hd], won_ref[hd], preferred_element_type=F32)
    r_ref[...] = acc
    h = acc * lax.rsqrt(jnp.mean(acc * acc, axis=-1, keepdims=True) + RMS_EPS) * g_ref[...]
    h_ref[...] = h
    h_hi = h.astype(BF16)
    h_lo = (h - h_hi.astype(F32)).astype(BF16)
    logits = jnp.dot(jnp.concatenate([h_hi, h_hi, h_lo], axis=1), wr_ref[...],
                     preferred_element_type=F32) + br_ref[...]
    e_id = lax.broadcasted_iota(jnp.int32, logits.shape, 1)
    work = logits
    vals, ids = [], []
    for _ in range(TOP_K):
        mx = jnp.max(work, axis=1, keepdims=True)
        idx = jnp.min(jnp.where(work == mx, e_id, N_EXPERTS), axis=1, keepdims=True)
        work = jnp.where(e_id == idx, -3e38, work)
        vals.append(mx)
        ids.append(idx)
    ex = [jnp.exp(v - vals[0]) for v in vals]
    tot = ex[0]
    for e in ex[1:]:
        tot = tot + e
    tm = logits.shape[0]
    onehot = [(e_id == ids[k]).astype(F32) for k in range(TOP_K)]
    oh_all = (onehot[0] + onehot[1]) + (onehot[2] + onehot[3])
    earlier = (lax.broadcasted_iota(jnp.int32, (tm, tm), 1) < lax.broadcasted_iota(jnp.int32, (tm, tm), 0))
    before = jnp.dot(earlier.astype(BF16), oh_all.astype(BF16), preferred_element_type=F32) + carry[...]
    k_id = lax.broadcasted_iota(jnp.int32, ti_ref.shape, 1)
    ti = jnp.zeros(ti_ref.shape, jnp.int32)
    rk = jnp.zeros(rk_ref.shape, jnp.int32)
    tg = jnp.zeros(tg_ref.shape, F32)
    for k in range(TOP_K):
        rank_k = jnp.sum(onehot[k] * before, axis=1, keepdims=True)
        ti = jnp.where(k_id == k, ids[k], ti)
        rk = jnp.where(k_id == k, rank_k.astype(jnp.int32), rk)
        tg = jnp.where(k_id == k, ex[k] / tot, tg)
        before = before + onehot[k]
    ti_ref[...] = ti
    rk_ref[...] = rk
    tg_ref[...] = tg
    carry[...] += jnp.sum(oh_all, axis=0, keepdims=True)
    cnt_ref[...] = carry[...]


def _mix(x, om, on8, wom, won8, g_ffn, w_router, b_router, counts0, tm):
    t, d = x.shape
    row = lambda w_: pl.BlockSpec((tm, w_), lambda i: (i, 0))
    w_hi = w_router.astype(BF16)
    w_lo = (w_router - w_hi.astype(F32)).astype(BF16)
    w_router = jnp.concatenate([w_hi, w_lo, w_hi], axis=0)
    return pl.pallas_call(
        _mix_kernel, grid=(t // tm,),
        in_specs=[row(d), row(512), pl.BlockSpec((NSA_HEADS, tm, LANES), lambda i: (0, i, 0)),
                  _const_spec(wom.shape), _const_spec(won8.shape), _const_spec((1, d)),
                  _const_spec(w_router.shape), _const_spec((1, N_EXPERTS)), _const_spec((1, N_EXPERTS))],
        out_specs=[row(d), row(d), row(TOP_K), row(TOP_K), row(TOP_K), _const_spec((1, N_EXPERTS))],
        out_shape=[jax.ShapeDtypeStruct((t, d), F32), jax.ShapeDtypeStruct((t, d), F32),
                   jax.ShapeDtypeStruct((t, TOP_K), jnp.int32), jax.ShapeDtypeStruct((t, TOP_K), F32),
                   jax.ShapeDtypeStruct((t, TOP_K), jnp.int32), jax.ShapeDtypeStruct((1, N_EXPERTS), F32)],
        scratch_shapes=[pltpu.VMEM((1, N_EXPERTS), F32)],
        compiler_params=_cparams(("arbitrary",)), name="mix_router",
    )(x, om, on8, wom, won8, g_ffn.reshape(1, d), w_router, b_router.reshape(1, N_EXPERTS), counts0)


TM_E = 256


def _route(top_i, rank, counts, tm):
    n_assign = top_i.shape[0] * TOP_K
    cnt = counts.reshape(N_EXPERTS).astype(jnp.int32)
    padded = (cnt + tm - 1) // tm * tm
    pad_end = jnp.cumsum(padded)
    dest = (jnp.take(pad_end - padded, top_i, axis=0) + rank).reshape(n_assign)
    n_blocks = -(-n_assign // tm) + N_EXPERTS
    first_row = jnp.arange(n_blocks, dtype=jnp.int32)[:, None] * tm
    blk_e = jnp.minimum(jnp.sum((pad_end[None, :] <= first_row).astype(jnp.int32), axis=1), N_EXPERTS - 1)
    n_used = (pad_end[-1] // tm).astype(jnp.int32).reshape(1)
    return dest.astype(jnp.int32), blk_e.astype(jnp.int32), n_used, n_blocks


DISPATCH_UNROLL = 8


def _dispatch_kernel(d_ref, h_ref, xs_in, xs_out, sem):
    del xs_in
    tm = h_ref.shape[0]
    unroll = math.gcd(tm, DISPATCH_UNROLL)

    def body(c, carry):
        for u in range(unroll):
            r = c * unroll + u
            for k in range(TOP_K):
                pltpu.make_async_copy(h_ref.at[pl.ds(r, 1), :], xs_out.at[pl.ds(d_ref[r * TOP_K + k], 1), :], sem).start()
        return carry

    lax.fori_loop(0, tm // unroll, body, 0)
    for k in range(TOP_K):
        pltpu.make_async_copy(h_ref, xs_out.at[pl.ds(0, tm), :], sem).wait()


def _dispatch(h, dest, xs, tm):
    t, d = h.shape
    assert t % tm == 0
    return pl.pallas_call(
        _dispatch_kernel, grid=(t // tm,),
        in_specs=[pl.BlockSpec((tm * TOP_K,), lambda i: (i,), memory_space=pltpu.SMEM),
                  pl.BlockSpec((tm, d), lambda i: (i, 0)),
                  pl.BlockSpec(memory_space=pl.ANY)],
        out_specs=pl.BlockSpec(memory_space=pl.ANY),
        out_shape=jax.ShapeDtypeStruct(xs.shape, xs.dtype),
        scratch_shapes=[pltpu.SemaphoreType.DMA(())],
        input_output_aliases={2: 0},
        compiler_params=_cparams(("arbitrary",)), name="moe_dispatch",
    )(dest, h, xs)


def _expert_kernel(blk_e_ref, n_used_ref, x_ref, wgu_ref, bgu_ref, wd_ref, bd_ref, y_ref, wgu_s, wd_s):
    i = pl.program_id(0)

    @pl.when((i == 0) | (blk_e_ref[i] != blk_e_ref[jnp.maximum(i - 1, 0)]))
    def _():
        wgu_s[...] = wgu_ref[0].astype(BF16)
        hi = lax.bitcast_convert_type(wd_ref[0].astype(BF16).astype(F32), jnp.uint32)
        wd_s[...] = pltpu.bitcast(hi | (hi >> 16), BF16)

    @pl.when(i < n_used_ref[0])
    def _():
        gu = jnp.dot(x_ref[...].astype(BF16), wgu_s[...], preferred_element_type=F32) + bgu_ref[0]
        glu = jnp.minimum(gu, SWIGLU_LIMIT)
        lin = jnp.clip(gu, -SWIGLU_LIMIT, SWIGLU_LIMIT) + 1.0
        lin_next = pltpu.roll(lin, lin.shape[1] - 1, 1)
        even = lax.broadcasted_iota(jnp.int32, gu.shape, 1) % 2 == 0
        act = jnp.where(even, glu * jax.nn.sigmoid(SWIGLU_ALPHA * glu) * lin_next, 0.0)
        y_ref[...] = jnp.dot(act.astype(BF16), wd_s[...], preferred_element_type=F32) + bd_ref[0]

    @pl.when(i >= n_used_ref[0])
    def _():
        y_ref[...] = jnp.zeros_like(y_ref)


def _experts(xs, blk_e, n_used, wgu, bgu, wd, bd):
    n_rows, d = xs.shape
    dff2 = wgu.shape[2]
    tm = TM_E
    e_map = lambda i, be, nu: (be[i], 0, 0)
    grid_spec = pltpu.PrefetchScalarGridSpec(
        num_scalar_prefetch=2, grid=(n_rows // tm,),
        in_specs=[pl.BlockSpec((tm, d), lambda i, be, nu: (i, 0)),
                  pl.BlockSpec((1, d, dff2), e_map), pl.BlockSpec((1, 1, dff2), e_map),
                  pl.BlockSpec((1, dff2 // 2, d), e_map), pl.BlockSpec((1, 1, d), e_map)],
        out_specs=pl.BlockSpec((tm, d), lambda i, be, nu: (i, 0)),
        scratch_shapes=[pltpu.VMEM((d, dff2), BF16), pltpu.VMEM((dff2, d), BF16)])
    return pl.pallas_call(
        _expert_kernel, grid_spec=grid_spec,
        out_shape=jax.ShapeDtypeStruct((n_rows, d), F32),
        compiler_params=_cparams(("arbitrary",)), name="moe_experts",
    )(blk_e, n_used, xs, wgu, bgu, wd, bd)


def _combine_kernel(d_cur_ref, d_nxt_ref, y_hbm, r_ref, tg_ref, p_ref, wpg_ref, wpp_ref, gf_ref, o_ref, ybuf, sem):
    i = pl.program_id(0)
    n = pl.num_programs(0)
    tm = ybuf.shape[2]

    def start_gather(d_ref, slot):
        for r in range(tm):
            for k in range(TOP_K):
                pltpu.make_async_copy(y_hbm.at[pl.ds(d_ref[r * TOP_K + k], 1), :],
                                      ybuf.at[slot, k, pl.ds(r, 1), :], sem.at[slot]).start()

    def wait_gather(slot):
        for k in range(TOP_K):
            pltpu.make_async_copy(y_hbm.at[pl.ds(0, tm), :], ybuf.at[slot, k], sem.at[slot]).wait()

    @pl.when(i == 0)
    def _():
        start_gather(d_cur_ref, 0)

    def step(slot):
        wait_gather(slot)
        start_gather(d_nxt_ref, 1 - slot)
        tg = tg_ref[...]
        r = r_ref[...]
        for k in range(TOP_K):
            r = r + tg[:, k:k + 1] * ybuf[slot, k]
        gate = jax.nn.sigmoid(jnp.dot(r.astype(BF16), wpg_ref[...], preferred_element_type=F32))
        r = r + gate * jnp.dot(p_ref[...].astype(BF16), wpp_ref[...], preferred_element_type=F32)
        o_ref[...] = r * lax.rsqrt(jnp.mean(r * r, axis=-1, keepdims=True) + RMS_EPS) * gf_ref[...]

        @pl.when(i == n - 1)
        def _():
            wait_gather(1 - slot)

    for parity in range(2):
        pl.when(i % 2 == parity)(functools.partial(step, parity))


def _combine(y, dest, r1, tg, p, wpg, wpp, g_final, tm):
    t, d = r1.shape
    nt = t // tm
    row = lambda w_: pl.BlockSpec((tm, w_), lambda i: (i, 0))
    return pl.pallas_call(
        _combine_kernel, grid=(nt,),
        in_specs=[pl.BlockSpec((tm * TOP_K,), lambda i: (i,), memory_space=pltpu.SMEM),
                  pl.BlockSpec((tm * TOP_K,), lambda i: (jnp.minimum(i + 1, nt - 1),), memory_space=pltpu.SMEM),
                  pl.BlockSpec(memory_space=pl.ANY),
                  row(d), row(TOP_K), row(p.shape[1]),
                  _const_spec(wpg.shape), _const_spec(wpp.shape), _const_spec((1, d))],
        out_specs=row(d),
        out_shape=jax.ShapeDtypeStruct((t, d), F32),
        scratch_shapes=[pltpu.VMEM((2, TOP_K, tm, d), F32), pltpu.SemaphoreType.DMA((2,))],
        compiler_params=_cparams(("arbitrary",)), name="moe_combine",
    )(dest, dest, y, r1, tg, p, wpg, wpp, g_final.reshape(1, d))


def _ffn(groups, wts):
    mixed, counts = [], jnp.zeros((1, N_EXPERTS), F32)
    for x, om, on8, _, tm in groups:
        *outs, counts = _mix(x, om, on8, wts["wom"], wts["won8"], wts["g_ffn"], wts["w_router"], wts["b_router"],
                             counts, tm)
        mixed.append(outs)
    dest, blk_e, n_used, n_blocks = _route(jnp.concatenate([m[2] for m in mixed], axis=0),
                                           jnp.concatenate([m[4] for m in mixed], axis=0), counts, TM_E)
    xs = jnp.zeros((n_blocks * TM_E, groups[0][0].shape[1]), F32)
    dests, start = [], 0
    for (x, _, _, _, tm), m in zip(groups, mixed):
        n = x.shape[0] * TOP_K
        dests.append(dest[start:start + n])
        xs = _dispatch(m[1], dests[-1], xs, min(2 * tm, x.shape[0]))
        start += n
    y = _experts(xs, blk_e, n_used, wts["wgu"], wts["bgu"], wts["wd"], wts["bd"])
    return [_combine(y, d_g, m[0], m[3], p, wts["wpg"], wts["wpp"], wts["g_final"], tm)
            for (_, _, _, p, tm), m, d_g in zip(groups, mixed, dests)]


KMEAN_PAGES = 16
PAGES_PER_MOBA = MOBA_BLOCK // PAGE_SIZE
SEL_PER_PAGE = PAGE_SIZE // SEL_BLOCK


def _kmean_kernel(pt_ref, *refs):
    x_refs, o_ref = refs[:KMEAN_PAGES], refs[KMEAN_PAGES]
    i = pl.program_id(1)
    nb_step = KMEAN_PAGES // PAGES_PER_MOBA

    @pl.when(i == 0)
    def _():
        o_ref[...] = jnp.zeros_like(o_ref)

    blk_lane = lax.broadcasted_iota(jnp.int32, o_ref.shape, 2)
    acc = o_ref[...]
    for blk in range(nb_step):
        s = x_refs[blk * PAGES_PER_MOBA][...]
        for k in range(1, PAGES_PER_MOBA):
            s = s + x_refs[blk * PAGES_PER_MOBA + k][...]
        col = jnp.sum(s, axis=-1, keepdims=True) * (1.0 / MOBA_BLOCK)
        acc = jnp.where(blk_lane == i * nb_step + blk, col, acc)
    o_ref[...] = acc


def _moba_kmean(cache_t, page_table):
    b, n_pages = page_table.shape
    assert n_pages % KMEAN_PAGES == 0
    _, _, h, d, ps = cache_t.shape

    def page_spec(k):
        return pl.BlockSpec((None, None, h, d, ps), lambda bi, i, pt: (0, pt[bi, i * KMEAN_PAGES + k], 0, 0, 0))

    n_blk = n_pages // PAGES_PER_MOBA
    grid_spec = pltpu.PrefetchScalarGridSpec(
        num_scalar_prefetch=1, grid=(b, n_pages // KMEAN_PAGES),
        in_specs=[page_spec(k) for k in range(KMEAN_PAGES)],
        out_specs=pl.BlockSpec((None, h, d, n_blk), lambda bi, i, pt: (bi, 0, 0, 0)))
    return pl.pallas_call(
        _kmean_kernel, grid_spec=grid_spec,
        out_shape=jax.ShapeDtypeStruct((b, h, d, n_blk), F32),
        compiler_params=_cparams(("parallel", "arbitrary")), name="moba_kmean",
    )(page_table, *([cache_t] * KMEAN_PAGES))


def _dec_select_kernel(qm_ref, kmean_ref, qn_ref, kc_ref, vc_ref, mid_ref, oc_ref, pcg_ref, *, n_cmp):
    for hd in range(8):
        gate = jnp.sum(kmean_ref[hd] * qm_ref[hd], axis=0, keepdims=True)
        blk = lax.broadcasted_iota(jnp.int32, gate.shape, 1)
        _, ids = _topk_mask(gate, blk >= 0, MOBA_TOPK, blk, 1)
        for k in range(MOBA_TOPK):
            mid_ref[k:k + 1, hd:hd + 1] = ids[k]
    lc = lax.dot_general(qn_ref[...], kc_ref[...], NT, preferred_element_type=F32)
    ok = lax.broadcasted_iota(jnp.int32, lc.shape, 1) < n_cmp
    mc = jnp.max(jnp.where(ok, lc, NEG), axis=1, keepdims=True)
    pc = jnp.where(ok, jnp.exp(lc - mc), 0.0)
    lsum = jnp.sum(pc, axis=1, keepdims=True)
    pc = pc / jnp.where(lsum > 0.0, lsum, 1.0)
    oc_ref[...] = jnp.dot(pc.astype(BF16), vc_ref[...], preferred_element_type=F32)
    for g in range(2):
        pcg_ref[g:g + 1, :] = jnp.sum(pc[4 * g:4 * g + 4], axis=0, keepdims=True)


def _dec_select(qm_col, kmean_t, qn8, kcmp, vcmp, n_cmp):
    b, _, _, n_blk = kmean_t.shape
    n_cp = kcmp.shape[1]
    per_b = lambda *s: pl.BlockSpec((None,) + s, lambda bi: (bi,) + (0,) * len(s))
    return pl.pallas_call(
        functools.partial(_dec_select_kernel, n_cmp=n_cmp), grid=(b,),
        in_specs=[per_b(8, HEAD_DIM, 1), per_b(8, HEAD_DIM, n_blk), per_b(8, LANES), per_b(n_cp, LANES),
                  per_b(n_cp, LANES)],
        out_specs=[per_b(MOBA_TOPK, 8), per_b(8, LANES), per_b(2, n_cp)],
        out_shape=[jax.ShapeDtypeStruct((b, MOBA_TOPK, 8), jnp.int32), jax.ShapeDtypeStruct((b, 8, LANES), F32),
                   jax.ShapeDtypeStruct((b, 2, n_cp), F32)],
        compiler_params=_cparams(("parallel",)), name="dec_select",
    )(qm_col, kmean_t, qn8, kcmp, vcmp)


def _dec_topn_kernel(pcg_ref, ov_ref, o_ref, *, pos, n_sel):
    imp = jnp.dot(pcg_ref[...], ov_ref[...], precision=HI, preferred_element_type=F32)
    sid = lax.broadcasted_iota(jnp.int32, imp.shape, 1)
    cur = pos // SEL_BLOCK
    forced = (sid == 0) | (sid == cur) | (sid == cur - 1)
    valid = (sid * SEL_BLOCK <= pos) & (sid < n_sel)
    n_top = min(SEL_TOPN, n_sel)
    _, ids = _topk_mask(jnp.where(forced, FORCE, imp), valid, n_top, sid, 1)
    k_id = lax.broadcasted_iota(jnp.int32, o_ref.shape, 1)
    out = jnp.full(o_ref.shape, imp.shape[1], jnp.int32)
    for k in range(n_top):
        out = jnp.where(k_id == k, ids[k], out)
    o_ref[...] = out


def _dec_topn(pcg, pos, n_sel):
    rows, n_cp = pcg.shape
    n_sp = -(-n_sel // LANES) * LANES
    ov = jnp.asarray(_overlap_matrix(n_cp, n_sp))
    return pl.pallas_call(
        functools.partial(_dec_topn_kernel, pos=pos, n_sel=n_sel),
        out_shape=jax.ShapeDtypeStruct((rows, SEL_TOPN), jnp.int32), name="dec_topn",
        compiler_params=pltpu.CompilerParams(vmem_limit_bytes=VMEM_LIMIT),
    )(pcg, ov)


def _attend_t(s_self, v_self, logit_tiles, value_tiles_t):
    m = s_self
    for s in logit_tiles:
        m = jnp.maximum(m, jnp.max(s, axis=1, keepdims=True))
    p_self = jnp.exp(s_self - m)
    l = p_self
    acc = p_self * v_self
    for s, vt in zip(logit_tiles, value_tiles_t):
        p = jnp.exp(s - m)
        l = l + jnp.sum(p, axis=1, keepdims=True)
        acc = acc + lax.dot_general(p.astype(BF16), vt, NT, preferred_element_type=F32)
    return acc / l


def _dec_attn_kernel(pt_ref, mid_ref, sid_ref, qm_ref, kmn_ref, vmn_ref, qn_ref, new_ref, gt_ref, oc_ref,
                     wk_ref, wv_ref, pbm_ref, pbn_ref, wb_ref, b0_ref,
                     mk_hbm, mv_hbm, sk_hbm, sv_hbm, om_ref, on_ref,
                     kmbuf, vmbuf, ksbuf, vsbuf, sem, *, n_blk, n_selp):
    bi = pl.program_id(0)
    n_top = ksbuf.shape[1]

    def moba_copies(hd, k):
        blk = jnp.minimum(mid_ref[bi, k * 8 + hd], n_blk - 1)
        out = []
        for half in range(PAGES_PER_MOBA):
            page = pt_ref[bi, blk * PAGES_PER_MOBA + half]
            out.append(pltpu.make_async_copy(mk_hbm.at[0, page, hd], kmbuf.at[hd, k, half], sem.at[0]))
            out.append(pltpu.make_async_copy(mv_hbm.at[0, page, hd], vmbuf.at[hd, k, half], sem.at[0]))
        return out

    def sel_copies(g, k):
        s = jnp.minimum(sid_ref[bi, g * n_top + k], n_selp - 1)
        page = pt_ref[bi, s // SEL_PER_PAGE]
        return [pltpu.make_async_copy(sk_hbm.at[0, page, g], ksbuf.at[g, k], sem.at[1]),
                pltpu.make_async_copy(sv_hbm.at[0, page, g], vsbuf.at[g, k], sem.at[1])]

    copies = [c for hd in range(8) for k in range(MOBA_TOPK) for c in moba_copies(hd, k)]
    copies += [c for g in range(2) for k in range(n_top) for c in sel_copies(g, k)]
    for c in copies:
        c.start()
    for c in copies:
        c.wait()

    b0 = b0_ref[...]

    for hd in range(8):
        qf = qm_ref[hd:hd + 1, :] * SCALE
        qh = qf.astype(BF16)
        s_self = jnp.sum(qf * kmn_ref[hd:hd + 1, :], axis=1, keepdims=True) + b0[hd:hd + 1]
        ss, vv = [], []
        for k in range(MOBA_TOPK):
            raw = mid_ref[bi, k * 8 + hd]
            blk = jnp.minimum(raw, n_blk - 1)
            for half in range(PAGES_PER_MOBA):
                s = jnp.dot(qh, kmbuf[hd, k, half].astype(BF16), preferred_element_type=F32)
                s = s + pbm_ref[blk][hd:hd + 1, half * PAGE_SIZE:(half + 1) * PAGE_SIZE]
                ss.append(jnp.where(raw < n_blk, s, NEG))
                vv.append(vmbuf[hd, k, half].astype(BF16))
        om_ref[hd:hd + 1, :] = _attend_t(s_self, vmn_ref[hd:hd + 1, :], ss, vv)

    gt = gt_ref[...]
    for g in range(2):
        lanes = slice(g * HEAD_DIM, (g + 1) * HEAD_DIM)
        hs = slice(4 * g, 4 * g + 4)
        qh = qn_ref[hs, lanes]
        qf = qh.astype(F32)
        ks_new, vs_new = new_ref[0:1, lanes], new_ref[1:2, lanes]
        kw_new, vw_new = new_ref[2:3, lanes], new_ref[3:4, lanes]

        b0g = b0[8 + 4 * g:12 + 4 * g]
        row_half = lax.broadcasted_iota(jnp.int32, (4, PAGE_SIZE), 1) // SEL_BLOCK
        ss, vv = [], []
        for k in range(n_top):
            raw = sid_ref[bi, g * n_top + k]
            sblk = jnp.minimum(raw, n_selp - 1)
            s = jnp.dot(qh, ksbuf[g, k].astype(BF16), preferred_element_type=F32)
            s = s + pbn_ref[sblk // SEL_PER_PAGE][hs, :]
            ss.append(jnp.where((row_half == sblk % SEL_PER_PAGE) & (raw < n_selp), s, NEG))
            vv.append(vsbuf[g, k].astype(BF16))
        s_self = jnp.sum(qf * ks_new, axis=1, keepdims=True) + b0g
        osel = _attend_t(s_self, vs_new, ss, vv)

        sw = jnp.dot(qh, wk_ref[g].astype(BF16), preferred_element_type=F32) + wb_ref[hs, :]
        s_self = jnp.sum(qf * kw_new, axis=1, keepdims=True) + b0g
        ow = _attend_t(s_self, vw_new, [sw], [wv_ref[g].astype(BF16)])

        oc = oc_ref[hs, lanes]
        for r in range(4):
            hd = 4 * g + r
            o = (gt[:, hd * 3:hd * 3 + 1] * oc[r:r + 1] + gt[:, hd * 3 + 1:hd * 3 + 2] * osel[r:r + 1]
                 + gt[:, hd * 3 + 2:hd * 3 + 3] * ow[r:r + 1])
            on_ref[hd:hd + 1, :] = o


def _dec_attn(page_table, mids, sids, qm, kmn, vmn, qn8, new_rows, gates, oc, wk, wv, pbm, pbn, wb, b0,
              cache_mk, cache_mv, cache_sk, cache_sv):
    b = page_table.shape[0]
    n_blk = pbm.shape[0]
    n_selp = pbn.shape[0] * SEL_PER_PAGE
    n_top = sids.shape[1] // 2
    win = wk.shape[3]
    per_b = lambda *s: pl.BlockSpec((None,) + s, lambda bi, *_: (bi,) + (0,) * len(s))
    const = lambda a: pl.BlockSpec(a.shape, lambda bi, *_: (0,) * a.ndim)
    any_spec = pl.BlockSpec(memory_space=pl.ANY)
    grid_spec = pltpu.PrefetchScalarGridSpec(
        num_scalar_prefetch=3, grid=(b,),
        in_specs=[per_b(8, HEAD_DIM), per_b(8, HEAD_DIM), per_b(8, HEAD_DIM), per_b(8, LANES), per_b(4, LANES),
                  per_b(1, LANES), per_b(8, LANES), per_b(2, HEAD_DIM, win), per_b(2, HEAD_DIM, win),
                  const(pbm), const(pbn), const(wb), const(b0), any_spec, any_spec, any_spec, any_spec],
        out_specs=[per_b(8, HEAD_DIM), per_b(8, HEAD_DIM)],
        scratch_shapes=[pltpu.VMEM((8, MOBA_TOPK, PAGES_PER_MOBA, HEAD_DIM, PAGE_SIZE), F32),
                        pltpu.VMEM((8, MOBA_TOPK, PAGES_PER_MOBA, HEAD_DIM, PAGE_SIZE), F32),
                        pltpu.VMEM((2, n_top, HEAD_DIM, PAGE_SIZE), F32),
                        pltpu.VMEM((2, n_top, HEAD_DIM, PAGE_SIZE), F32),
                        pltpu.SemaphoreType.DMA((2,))])
    return pl.pallas_call(
        functools.partial(_dec_attn_kernel, n_blk=n_blk, n_selp=n_selp), grid_spec=grid_spec,
        out_shape=[jax.ShapeDtypeStruct((b, 8, HEAD_DIM), F32), jax.ShapeDtypeStruct((b, 8, HEAD_DIM), F32)],
        compiler_params=_cparams(("arbitrary",)), name="dec_attn",
    )(page_table, mids, sids, qm, kmn, vmn, qn8, new_rows, gates, oc, wk, wv, pbm, pbn, wb, b0,
      cache_mk, cache_mv, cache_sk, cache_sv)


def kernel(x_prompt, x_sample, p_prompt, p_sample, cache_moba_k, cache_moba_v, cache_nsa_cmp_k, cache_nsa_cmp_v, cache_nsa_sel_k, cache_nsa_sel_v, state_nsa_win_k, state_nsa_win_v, page_table, rel_bias, g_mix, w_in, w_out, cmp_w1_k, cmp_b1_k, cmp_w2_k, cmp_pos_k, cmp_w1_v, cmp_b1_v, cmp_w2_v, cmp_pos_v, g_ffn, w_router, b_router, w_gate_up, b_gate_up, w_down, b_down, w_ple_proj, w_ple_gate, g_final):
    assert rel_bias.shape == (N_BUCKETS, 16) and g_mix.shape[0] == 1, "one layer, 8 MoBA + 8 NSA heads"
    b, t, d = x_prompt.shape
    bd, dec_seq, _ = x_sample.shape
    assert dec_seq == 1 and t % (2 * MOBA_BLOCK) == 0
    n_pages = page_table.shape[1]
    past = n_pages * PAGE_SIZE
    win = state_nsa_win_k.shape[2]

    bv = rel_bias[_t5_bucket(jnp.arange(MAX_DISTANCE + 1))]
    n_tbm = -(-(MAX_DISTANCE + MOBA_BLOCK - 1) // TQ_M) + 1
    tbm = _toeplitz_tiles(bv[:, :8], TQ_M, MOBA_BLOCK, n_tbm, True)
    n_tbn = -(-(MAX_DISTANCE + TK_N - 1) // TQ_N) + 1
    tbn = _toeplitz_tiles(bv[:, 8:], TQ_N, TK_N, n_tbn, False)
    n_wm = -(-(WINDOW + TK_N) // TQ_N)
    dist = np.stack([TQ_N * m + np.arange(TQ_N)[:, None] - np.arange(TK_N)[None, :] for m in range(n_wm)])
    wm = jnp.asarray(np.where(dist <= WINDOW, 0.0, NEG).astype(np.float32))
    w_proj = _proj_weight(w_in[0])
    wo = w_out[0]
    won = wo[512:].reshape(NSA_HEADS, HEAD_DIM, d)
    zn = jnp.zeros_like(won)
    grp0 = (jnp.arange(NSA_HEADS) < 4)[:, None, None]
    wts = dict(
        wom=wo[:512].astype(BF16),
        won8=jnp.concatenate([jnp.where(grp0, won, zn), jnp.where(grp0, zn, won)], axis=1).astype(BF16),
        g_ffn=g_ffn[0], w_router=w_router[0], b_router=b_router[0],
        wgu=w_gate_up[0], bgu=b_gate_up[0][:, None, :], wd=w_down[0], bd=b_down[0][:, None, :],
        wpg=w_ple_gate[0].astype(BF16), wpp=w_ple_proj[0].astype(BF16), g_final=g_final)
    ck = _cmp_const(cmp_pos_k[0], cmp_w1_k[0], cmp_b1_k[0])
    cv = _cmp_const(cmp_pos_v[0], cmp_w1_v[0], cmp_b1_v[0])

    def pad_heads(o):
        z = jnp.zeros_like(o)
        return jnp.concatenate([jnp.where(grp0, o, z), jnp.where(grp0, z, o)], axis=-1).astype(BF16)

    xp = x_prompt.reshape(b * t, d)
    qm, km, vm, kmb, vmb, qn8, kv, kvb, gt, kmean = _project(xp, g_mix, w_proj, 2 * MOBA_BLOCK, True)
    kmean = jnp.pad(kmean.reshape(b, t // MOBA_BLOCK, 512), ((0, 0), (0, LANES - t // MOBA_BLOCK), (0, 0)))
    om = _moba_prompt(qm.reshape(b, t, 512), kmb.reshape(b, t, 512), vmb.reshape(b, t, 512), kmean, tbm)
    pt_p = jnp.arange(b * t // PAGE_SIZE, dtype=jnp.int32).reshape(b, t // PAGE_SIZE)
    pages_t = lambda a: jnp.transpose(a.reshape(-1, PAGE_SIZE, 2, HEAD_DIM), (0, 2, 3, 1))
    kcmp = _compress(_chunk_layout(pages_t(kv[0])), pt_p, cmp_w1_k[0], ck, cmp_w2_k[0])
    vcmp = _compress(_chunk_layout(pages_t(kv[1])), pt_p, cmp_w1_v[0], cv, cmp_w2_v[0])
    seq = lambda a: a.reshape(b, t, LANES)
    on8 = _nsa_prompt(qn8, gt, kcmp, vcmp, seq(kvb[2]), seq(kvb[3]), seq(kvb[4]), seq(kvb[5]), tbn, wm,
                      (t - CMP_LEN) // CMP_STRIDE + 1)

    xs = x_sample.reshape(bd, d)
    qm_s, km_s, vm_s, _, _, qn8_s, kv_s, _, gt_s = _project(xs, g_mix, w_proj, bd, False)
    rows_minor = lambda a: jnp.transpose(a, (0, 1, 3, 4, 2))
    kmean_s = _moba_kmean(rows_minor(cache_moba_k), page_table)
    kcmp_s = _compress(_chunk_layout(rows_minor(cache_nsa_cmp_k)[0]), page_table, cmp_w1_k[0], ck, cmp_w2_k[0])
    vcmp_s = _compress(_chunk_layout(rows_minor(cache_nsa_cmp_v)[0]), page_table, cmp_w1_v[0], cv, cmp_w2_v[0])
    heads = lambda a: a.reshape(bd, 8, HEAD_DIM)
    qn8_sb = jnp.transpose(qn8_s, (1, 0, 2))
    mids, oc_s, pcg = _dec_select(qm_s.reshape(bd, 8, HEAD_DIM, 1), kmean_s, qn8_sb, kcmp_s, vcmp_s,
                                  (past + 1 - CMP_LEN) // CMP_STRIDE + 1)
    n_sel_s = -(-(past + 1) // SEL_BLOCK)
    sids = _dec_topn(pcg.reshape(bd * 2, -1), past, n_sel_s)
    n_blk = past // MOBA_BLOCK
    kpos_m = np.arange(n_blk * MOBA_BLOCK).reshape(n_blk, MOBA_BLOCK)
    pbm = jnp.transpose(jnp.take(bv[:, :8], jnp.asarray(np.minimum(past - kpos_m, MAX_DISTANCE)), axis=0), (0, 2, 1))
    kpos_n = np.arange(past).reshape(n_pages, PAGE_SIZE)
    pbn = jnp.transpose(jnp.take(bv[:, 8:], jnp.asarray(np.minimum(past - kpos_n, MAX_DISTANCE)), axis=0), (0, 2, 1))
    wb = jnp.take(bv[:, 8:], jnp.asarray(np.minimum(win - np.arange(win), MAX_DISTANCE)), axis=0).T
    b0 = bv[0][:, None]
    new_rows = jnp.stack([kv_s[2], kv_s[3], kv_s[4], kv_s[5]], axis=1)
    om_s, on_s = _dec_attn(page_table, mids.reshape(bd, -1), sids.reshape(bd, -1), heads(qm_s), heads(km_s),
                           heads(vm_s), qn8_sb, new_rows, gt_s.reshape(bd, 1, LANES), oc_s,
                           rows_minor(state_nsa_win_k)[0], rows_minor(state_nsa_win_v)[0],
                           pbm, pbn, wb, b0, rows_minor(cache_moba_k), rows_minor(cache_moba_v),
                           rows_minor(cache_nsa_sel_k), rows_minor(cache_nsa_sel_v))
    y_p, y_s = _ffn([(xp, om.reshape(b * t, 512), on8, p_prompt[0].reshape(b * t, -1), 256),
                     (xs, om_s.reshape(bd, 512).astype(BF16), pad_heads(jnp.transpose(on_s, (1, 0, 2))),
                      p_sample[0].reshape(bd, -1), bd)], wts)

    wp = min(WINDOW, t)
    mh = lambda a, n, h: a.reshape(1, n, -1, h, HEAD_DIM)
    kw_new = kv_s[4].reshape(bd, 1, 2, HEAD_DIM)
    vw_new = kv_s[5].reshape(bd, 1, 2, HEAD_DIM)
    win_k = jnp.concatenate([state_nsa_win_k[0], kw_new], axis=1)[:, -win:][None]
    win_v = jnp.concatenate([state_nsa_win_v[0], vw_new], axis=1)[:, -win:][None]
    return (y_p.reshape(b, t, d), y_s.reshape(bd, 1, d),
            mh(km, b, 8), mh(vm, b, 8), mh(kv[0], b, 2), mh(kv[1], b, 2), mh(kv[2], b, 2), mh(kv[3], b, 2),
            mh(kv[4], b, 2)[:, :, -wp:], mh(kv[5], b, 2)[:, :, -wp:],
            mh(km_s, bd, 8), mh(vm_s, bd, 8), mh(kv_s[0], bd, 2), mh(kv_s[1], bd, 2), mh(kv_s[2], bd, 2),
            mh(kv_s[3], bd, 2), win_k, win_v)
```
